```python
import jax, jax.numpy as jnp
from jax import lax
import numpy as np

D_MODEL = 1024
BATCH = 8
SEQ = 16384
DEPTH = 4

MIX_WIDTH = D_MODEL
POOL_WIDTH = D_MODEL // 4
SCONV_WIDTH = 3 * D_MODEL // 8
CCONV_WIDTH = MIX_WIDTH - POOL_WIDTH - SCONV_WIDTH
HEAD_DIM = 64
POOL_WINDOWS = (2, 4, 8, 16)
N_POOL_GROUPS = len(POOL_WINDOWS)
POOL_GROUP = POOL_WIDTH // N_POOL_GROUPS
SCONV_K = 3
CCONV_K = 31
IN_COLS = POOL_WIDTH + 3 * SCONV_WIDTH + 2 * CCONV_WIDTH
D_FF = 2816
LN_EPS = 1e-5
DEEPNORM_ALPHA = (2.0 * DEPTH) ** 0.25
DEEPNORM_BETA = (8.0 * DEPTH) ** -0.25

kernel_name = "hybrid_pool_conv_conformer_encoder"


def layer_norm(x, g, b):
    xf = x.astype(jnp.float32)
    mu = jnp.mean(xf, axis=-1, keepdims=True)
    var = jnp.mean(jnp.square(xf - mu), axis=-1, keepdims=True)
    y = (xf - mu) * lax.rsqrt(var + LN_EPS)
    return (y * g.astype(jnp.float32) + b.astype(jnp.float32)).astype(x.dtype)


def swiglu_ffn(x, w_gate, w_up, w_down):
    return (jax.nn.silu(x @ w_gate) * (x @ w_up)) @ w_down


def depthwise_conv(u, w):
    k = w.shape[0]
    return lax.conv_general_dilated(
        u, w[:, None, :], window_strides=(1,), padding=[(k // 2, k // 2)],
        dimension_numbers=("NWC", "WIO", "NWC"), feature_group_count=u.shape[-1])


def centred_pool_minus_self(u, window):
    seq = u.shape[1]
    uf = u.astype(jnp.float32)
    csum = jnp.pad(jnp.cumsum(uf, axis=1), ((0, 0), (1, 0), (0, 0)))
    t = jnp.arange(seq)
    left = window // 2
    lo = jnp.clip(t - left, 0, seq)
    hi = jnp.clip(t - left + window, 0, seq)
    total = jnp.take(csum, hi, axis=1) - jnp.take(csum, lo, axis=1)
    count = (hi - lo).astype(jnp.float32)
    return (total / count[None, :, None] - uf).astype(u.dtype)


def hybrid_mixer(h, w_in, pool_w, pool_scale, sconv_w, cconv_w, cconv_b, cnorm_g, cnorm_b, w_out):
    bsz, seq = h.shape[0], h.shape[1]
    proj = h @ w_in
    cuts = [POOL_WIDTH,
            POOL_WIDTH + SCONV_WIDTH,
            POOL_WIDTH + 2 * SCONV_WIDTH,
            POOL_WIDTH + 3 * SCONV_WIDTH,
            POOL_WIDTH + 3 * SCONV_WIDTH + CCONV_WIDTH]
    u_pool, gate_b, gate_c, v, c_val, c_gate = jnp.split(proj, cuts, axis=-1)

    pooled = jnp.stack(
        [centred_pool_minus_self(u_pool[..., g * POOL_GROUP:(g + 1) * POOL_GROUP], w)
         for g, w in enumerate(POOL_WINDOWS)], axis=2)
    y_a = jnp.einsum("bsgc,gcd->bsgd", pooled, pool_w).reshape(bsz, seq, POOL_WIDTH) * pool_scale

    y_b = gate_b * depthwise_conv(gate_c * v, sconv_w)

    a = c_val * jax.nn.sigmoid(c_gate)
    a = depthwise_conv(a, cconv_w) + cconv_b
    y_c = jax.nn.silu(layer_norm(a, cnorm_g, cnorm_b))

    return jnp.concatenate([y_a, y_b, y_c], axis=-1) @ w_out


def _fwd_setup_inputs(seed: int = 0) -> dict:
    key = jax.random.key(seed)
    ks = jax.random.split(key, 22)

    def nrm(k, shape, scale):
        return jax.random.normal(k, shape, jnp.float32) * scale

    d, f = D_MODEL, D_FF
    return {
        "x": nrm(ks[0], (BATCH, SEQ, d), 1.0),
        "ln1_g": 1.0 + nrm(ks[1], (DEPTH, d), 0.05),
        "ln1_b": nrm(ks[2], (DEPTH, d), 0.02),
        "ffn1_w_gate": nrm(ks[3], (DEPTH, d, f), d ** -0.5),
        "ffn1_w_up": nrm(ks[4], (DEPTH, d, f), d ** -0.5),
        "ffn1_w_down": nrm(ks[5], (DEPTH, f, d), DEEPNORM_BETA * f ** -0.5),
        "mix_w_in": nrm(ks[6], (DEPTH, d, IN_COLS), d ** -0.5),
        "pool_w": nrm(ks[7], (DEPTH, N_POOL_GROUPS, POOL_GROUP, POOL_GROUP), POOL_GROUP ** -0.5),
        "pool_scale": 1.0 + nrm(ks[8], (DEPTH, POOL_WIDTH), 0.05),
        "sconv_w": nrm(ks[9], (DEPTH, SCONV_K, SCONV_WIDTH), SCONV_K ** -0.5),
        "cconv_w": nrm(ks[10], (DEPTH, CCONV_K, CCONV_WIDTH), CCONV_K ** -0.5),
        "cconv_b": nrm(ks[11], (DEPTH, CCONV_WIDTH), 0.02),
        "cnorm_g": 1.0 + nrm(ks[12], (DEPTH, CCONV_WIDTH), 0.05),
        "cnorm_b": nrm(ks[13], (DEPTH, CCONV_WIDTH), 0.02),
        "mix_w_out": nrm(ks[14], (DEPTH, MIX_WIDTH, d), DEEPNORM_BETA * MIX_WIDTH ** -0.5),
        "ln2_g": 1.0 + nrm(ks[15], (DEPTH, d), 0.05),
        "ln2_b": nrm(ks[16], (DEPTH, d), 0.02),
        "ffn2_w_gate": nrm(ks[17], (DEPTH, d, f), d ** -0.5),
        "ffn2_w_up": nrm(ks[18], (DEPTH, d, f), d ** -0.5),
        "ffn2_w_down": nrm(ks[19], (DEPTH, f, d), DEEPNORM_BETA * f ** -0.5),
        "ln3_g": 1.0 + nrm(ks[20], (DEPTH, d), 0.05),
        "ln3_b": nrm(ks[21], (DEPTH, d), 0.02),
    }


def _fwd_reference(x, ln1_g, ln1_b, ffn1_w_gate, ffn1_w_up, ffn1_w_down, mix_w_in, pool_w,
              pool_scale, sconv_w, cconv_w, cconv_b, cnorm_g, cnorm_b, mix_w_out,
              ln2_g, ln2_b, ffn2_w_gate, ffn2_w_up, ffn2_w_down, ln3_g, ln3_b):
    for l in range(DEPTH):
        x = layer_norm(DEEPNORM_ALPHA * x
                       + 0.5 * swiglu_ffn(x, ffn1_w_gate[l], ffn1_w_up[l], ffn1_w_down[l]),
                       ln1_g[l], ln1_b[l])
        x = layer_norm(DEEPNORM_ALPHA * x
                       + hybrid_mixer(x, mix_w_in[l], pool_w[l], pool_scale[l], sconv_w[l],
                                      cconv_w[l], cconv_b[l], cnorm_g[l], cnorm_b[l], mix_w_out[l]),
                       ln2_g[l], ln2_b[l])
        x = layer_norm(DEEPNORM_ALPHA * x
                       + 0.5 * swiglu_ffn(x, ffn2_w_gate[l], ffn2_w_up[l], ffn2_w_down[l]),
                       ln3_g[l], ln3_b[l])
    return x


import jax as _jax
import jax.numpy as _jnp

TWIN_FORMAT = 'train_step'
FWD_PARAMS = ['x', 'ln1_g', 'ln1_b', 'ffn1_w_gate', 'ffn1_w_up', 'ffn1_w_down', 'mix_w_in', 'pool_w', 'pool_scale', 'sconv_w', 'cconv_w', 'cconv_b', 'cnorm_g', 'cnorm_b', 'mix_w_out', 'ln2_g', 'ln2_b', 'ffn2_w_gate', 'ffn2_w_up', 'ffn2_w_down', 'ln3_g', 'ln3_b']
TWIN_WEIGHTS = ['ln1_g', 'ln1_b', 'ffn1_w_gate', 'ffn1_w_up', 'ffn1_w_down', 'mix_w_in', 'pool_w', 'pool_scale', 'sconv_w', 'cconv_w', 'cconv_b', 'cnorm_g', 'cnorm_b', 'mix_w_out', 'ln2_g', 'ln2_b', 'ffn2_w_gate', 'ffn2_w_up', 'ffn2_w_down', 'ln3_g', 'ln3_b']
TWIN_DIFF_INPUT = 'x'
TWIN_INPUTS = ['x', 'ln1_g', 'ln1_b', 'ffn1_w_gate', 'ffn1_w_up', 'ffn1_w_down', 'mix_w_in', 'pool_w', 'pool_scale', 'sconv_w', 'cconv_w', 'cconv_b', 'cnorm_g', 'cnorm_b', 'mix_w_out', 'ln2_g', 'ln2_b', 'ffn2_w_gate', 'ffn2_w_up', 'ffn2_w_down', 'ln3_g', 'ln3_b', 'loss_target', 'm_ln1_g', 'm_ln1_b', 'm_ffn1_w_gate', 'm_ffn1_w_up', 'm_ffn1_w_down', 'm_mix_w_in', 'm_pool_w', 'm_pool_scale', 'm_sconv_w', 'm_cconv_w', 'm_cconv_b', 'm_cnorm_g', 'm_cnorm_b', 'm_mix_w_out', 'm_ln2_g', 'm_ln2_b', 'm_ffn2_w_gate', 'm_ffn2_w_up', 'm_ffn2_w_down', 'm_ln3_g', 'm_ln3_b', 'v_ln1_g', 'v_ln1_b', 'v_ffn1_w_gate', 'v_ffn1_w_up', 'v_ffn1_w_down', 'v_mix_w_in', 'v_pool_w', 'v_pool_scale', 'v_sconv_w', 'v_cconv_w', 'v_cconv_b', 'v_cnorm_g', 'v_cnorm_b', 'v_mix_w_out', 'v_ln2_g', 'v_ln2_b', 'v_ffn2_w_gate', 'v_ffn2_w_up', 'v_ffn2_w_down', 'v_ln3_g', 'v_ln3_b']
TWIN_OUTPUTS = ['loss', 'grad_x', 'grad_ln1_g', 'grad_ln1_b', 'grad_ffn1_w_gate', 'grad_ffn1_w_up', 'grad_ffn1_w_down', 'grad_mix_w_in', 'grad_pool_w', 'grad_pool_scale', 'grad_sconv_w', 'grad_cconv_w', 'grad_cconv_b', 'grad_cnorm_g', 'grad_cnorm_b', 'grad_mix_w_out', 'grad_ln2_g', 'grad_ln2_b', 'grad_ffn2_w_gate', 'grad_ffn2_w_up', 'grad_ffn2_w_down', 'grad_ln3_g', 'grad_ln3_b', 'delta_ln1_g', 'delta_ln1_b', 'delta_ffn1_w_gate', 'delta_ffn1_w_up', 'delta_ffn1_w_down', 'delta_mix_w_in', 'delta_pool_w', 'delta_pool_scale', 'delta_sconv_w', 'delta_cconv_w', 'delta_cconv_b', 'delta_cnorm_g', 'delta_cnorm_b', 'delta_mix_w_out', 'delta_ln2_g', 'delta_ln2_b', 'delta_ffn2_w_gate', 'delta_ffn2_w_up', 'delta_ffn2_w_down', 'delta_ln3_g', 'delta_ln3_b', 'new_m_ln1_g', 'new_m_ln1_b', 'new_m_ffn1_w_gate', 'new_m_ffn1_w_up', 'new_m_ffn1_w_down', 'new_m_mix_w_in', 'new_m_pool_w', 'new_m_pool_scale', 'new_m_sconv_w', 'new_m_cconv_w', 'new_m_cconv_b', 'new_m_cnorm_g', 'new_m_cnorm_b', 'new_m_mix_w_out', 'new_m_ln2_g', 'new_m_ln2_b', 'new_m_ffn2_w_gate', 'new_m_ffn2_w_up', 'new_m_ffn2_w_down', 'new_m_ln3_g', 'new_m_ln3_b', 'new_v_ln1_g', 'new_v_ln1_b', 'new_v_ffn1_w_gate', 'new_v_ffn1_w_up', 'new_v_ffn1_w_down', 'new_v_mix_w_in', 'new_v_pool_w', 'new_v_pool_scale', 'new_v_sconv_w', 'new_v_cconv_w', 'new_v_cconv_b', 'new_v_cnorm_g', 'new_v_cnorm_b', 'new_v_mix_w_out', 'new_v_ln2_g', 'new_v_ln2_b', 'new_v_ffn2_w_gate', 'new_v_ffn2_w_up', 'new_v_ffn2_w_down', 'new_v_ln3_g', 'new_v_ln3_b']
TWIN_LEAF_KINDS = {'loss': 'loss', 'grad_x': 'grad_x', 'grad_ln1_g': 'grad_w', 'grad_ln1_b': 'grad_w', 'grad_ffn1_w_gate': 'grad_w', 'grad_ffn1_w_up': 'grad_w', 'grad_ffn1_w_down': 'grad_w', 'grad_mix_w_in': 'grad_w', 'grad_pool_w': 'grad_w', 'grad_pool_scale': 'grad_w', 'grad_sconv_w': 'grad_w', 'grad_cconv_w': 'grad_w', 'grad_cconv_b': 'grad_w', 'grad_cnorm_g': 'grad_w', 'grad_cnorm_b': 'grad_w', 'grad_mix_w_out': 'grad_w', 'grad_ln2_g': 'grad_w', 'grad_ln2_b': 'grad_w', 'grad_ffn2_w_gate': 'grad_w', 'grad_ffn2_w_up': 'grad_w', 'grad_ffn2_w_down': 'grad_w', 'grad_ln3_g': 'grad_w', 'grad_ln3_b': 'grad_w', 'delta_ln1_g': 'delta_w', 'delta_ln1_b': 'delta_w', 'delta_ffn1_w_gate': 'delta_w', 'delta_ffn1_w_up': 'delta_w', 'delta_ffn1_w_down': 'delta_w', 'delta_mix_w_in': 'delta_w', 'delta_pool_w': 'delta_w', 'delta_pool_scale': 'delta_w', 'delta_sconv_w': 'delta_w', 'delta_cconv_w': 'delta_w', 'delta_cconv_b': 'delta_w', 'delta_cnorm_g': 'delta_w', 'delta_cnorm_b': 'delta_w', 'delta_mix_w_out': 'delta_w', 'delta_ln2_g': 'delta_w', 'delta_ln2_b': 'delta_w', 'delta_ffn2_w_gate': 'delta_w', 'delta_ffn2_w_up': 'delta_w', 'delta_ffn2_w_down': 'delta_w', 'delta_ln3_g': 'delta_w', 'delta_ln3_b': 'delta_w', 'new_m_ln1_g': 'new_m', 'new_m_ln1_b': 'new_m', 'new_m_ffn1_w_gate': 'new_m', 'new_m_ffn1_w_up': 'new_m', 'new_m_ffn1_w_down': 'new_m', 'new_m_mix_w_in': 'new_m', 'new_m_pool_w': 'new_m', 'new_m_pool_scale': 'new_m', 'new_m_sconv_w': 'new_m', 'new_m_cconv_w': 'new_m', 'new_m_cconv_b': 'new_m', 'new_m_cnorm_g': 'new_m', 'new_m_cnorm_b': 'new_m', 'new_m_mix_w_out': 'new_m', 'new_m_ln2_g': 'new_m', 'new_m_ln2_b': 'new_m', 'new_m_ffn2_w_gate': 'new_m', 'new_m_ffn2_w_up': 'new_m', 'new_m_ffn2_w_down': 'new_m', 'new_m_ln3_g': 'new_m', 'new_m_ln3_b': 'new_m', 'new_v_ln1_g': 'new_v', 'new_v_ln1_b': 'new_v', 'new_v_ffn1_w_gate': 'new_v', 'new_v_ffn1_w_up': 'new_v', 'new_v_ffn1_w_down': 'new_v', 'new_v_mix_w_in': 'new_v', 'new_v_pool_w': 'new_v', 'new_v_pool_scale': 'new_v', 'new_v_sconv_w': 'new_v', 'new_v_cconv_w': 'new_v', 'new_v_cconv_b': 'new_v', 'new_v_cnorm_g': 'new_v', 'new_v_cnorm_b': 'new_v', 'new_v_mix_w_out': 'new_v', 'new_v_ln2_g': 'new_v', 'new_v_ln2_b': 'new_v', 'new_v_ffn2_w_gate': 'new_v', 'new_v_ffn2_w_up': 'new_v', 'new_v_ffn2_w_down': 'new_v', 'new_v_ln3_g': 'new_v', 'new_v_ln3_b': 'new_v'}


def _forward(args):
    return _fwd_reference(*[args[k] for k in FWD_PARAMS])


def _output_shape():
    def fwd():
        inp = _fwd_setup_inputs(0)
        return _fwd_reference(*[inp[k] for k in FWD_PARAMS])
    out = _jax.eval_shape(fwd)
    return out.shape, out.dtype

N_MICROBATCH = 1
ADAM_LR = 0.001
ADAM_B1 = 0.9
ADAM_B2 = 0.999
ADAM_EPS = 1e-08
ADAM_WD = 0.01
ADAM_STEP = 10
PER_EXAMPLE_BATCH_AXIS = {'x': 0, 'loss_target': 0}
SHARED_INPUTS = []
_WEIGHT_DTYPES = {'ln1_g': _jnp.float32, 'ln1_b': _jnp.float32, 'ffn1_w_gate': _jnp.float32, 'ffn1_w_up': _jnp.float32, 'ffn1_w_down': _jnp.float32, 'mix_w_in': _jnp.float32, 'pool_w': _jnp.float32, 'pool_scale': _jnp.float32, 'sconv_w': _jnp.float32, 'cconv_w': _jnp.float32, 'cconv_b': _jnp.float32, 'cnorm_g': _jnp.float32, 'cnorm_b': _jnp.float32, 'mix_w_out': _jnp.float32, 'ln2_g': _jnp.float32, 'ln2_b': _jnp.float32, 'ffn2_w_gate': _jnp.float32, 'ffn2_w_up': _jnp.float32, 'ffn2_w_down': _jnp.float32, 'ln3_g': _jnp.float32, 'ln3_b': _jnp.float32}
MOMENT_SCALE = {'ln1_g': 1.155990e+01, 'ln1_b': 1.909008e+00, 'ffn1_w_gate': 1.779552e-02, 'ffn1_w_up': 1.749477e-02, 'ffn1_w_down': 6.887661e-02, 'mix_w_in': 7.806248e-02, 'pool_w': 8.553342e-02, 'pool_scale': 8.660794e-02, 'sconv_w': 9.125371e-02, 'cconv_w': 5.943698e-02, 'cconv_b': 3.746788e-01, 'cnorm_g': 1.328291e-01, 'cnorm_b': 2.098835e-01, 'mix_w_out': 2.076653e-01, 'ln2_g': 1.203644e+01, 'ln2_b': 1.978008e+00, 'ffn2_w_gate': 1.728818e-02, 'ffn2_w_up': 1.704140e-02, 'ffn2_w_down': 6.722260e-02, 'ln3_g': 6.773973e+01, 'ln3_b': 5.665244e+00}


def _to_microbatches(a, axis):
    t = _jnp.moveaxis(a, axis, 0)
    t = t.reshape((N_MICROBATCH, t.shape[0] // N_MICROBATCH) + t.shape[1:])
    return _jnp.moveaxis(t, 1, axis + 1)


def setup_inputs(seed: int = 0) -> dict:
    inp = _fwd_setup_inputs(seed)
    key = _jax.random.fold_in(_jax.random.key(seed), 7919)
    shape, _ = _output_shape()
    out = dict(inp)
    out["loss_target"] = _jax.random.normal(_jax.random.fold_in(key, 0), shape, _jnp.float32)
    for i, name in enumerate(TWIN_WEIGHTS):
        w = inp[name].astype(_jnp.float32)
        if MOMENT_SCALE is None:
            s = _jnp.sqrt(_jnp.mean(_jnp.square(w)) + 1e-30)
        else:
            s = MOMENT_SCALE[name]
        km, kv = _jax.random.split(_jax.random.fold_in(key, i + 1))
        out[name] = w
        out["m_" + name] = s * _jax.random.normal(km, w.shape, _jnp.float32)
        out["v_" + name] = (s * s) * _jax.random.uniform(kv, w.shape, _jnp.float32, 0.5, 1.5)
    if N_MICROBATCH > 1:
        for name, axis in PER_EXAMPLE_BATCH_AXIS.items():
            out[name] = _to_microbatches(out[name], axis)
    return {'x': out['x'], 'ln1_g': out['ln1_g'], 'ln1_b': out['ln1_b'], 'ffn1_w_gate': out['ffn1_w_gate'], 'ffn1_w_up': out['ffn1_w_up'], 'ffn1_w_down': out['ffn1_w_down'], 'mix_w_in': out['mix_w_in'], 'pool_w': out['pool_w'], 'pool_scale': out['pool_scale'], 'sconv_w': out['sconv_w'], 'cconv_w': out['cconv_w'], 'cconv_b': out['cconv_b'], 'cnorm_g': out['cnorm_g'], 'cnorm_b': out['cnorm_b'], 'mix_w_out': out['mix_w_out'], 'ln2_g': out['ln2_g'], 'ln2_b': out['ln2_b'], 'ffn2_w_gate': out['ffn2_w_gate'], 'ffn2_w_up': out['ffn2_w_up'], 'ffn2_w_down': out['ffn2_w_down'], 'ln3_g': out['ln3_g'], 'ln3_b': out['ln3_b'], 'loss_target': out['loss_target'], 'm_ln1_g': out['m_ln1_g'], 'm_ln1_b': out['m_ln1_b'], 'm_ffn1_w_gate': out['m_ffn1_w_gate'], 'm_ffn1_w_up': out['m_ffn1_w_up'], 'm_ffn1_w_down': out['m_ffn1_w_down'], 'm_mix_w_in': out['m_mix_w_in'], 'm_pool_w': out['m_pool_w'], 'm_pool_scale': out['m_pool_scale'], 'm_sconv_w': out['m_sconv_w'], 'm_cconv_w': out['m_cconv_w'], 'm_cconv_b': out['m_cconv_b'], 'm_cnorm_g': out['m_cnorm_g'], 'm_cnorm_b': out['m_cnorm_b'], 'm_mix_w_out': out['m_mix_w_out'], 'm_ln2_g': out['m_ln2_g'], 'm_ln2_b': out['m_ln2_b'], 'm_ffn2_w_gate': out['m_ffn2_w_gate'], 'm_ffn2_w_up': out['m_ffn2_w_up'], 'm_ffn2_w_down': out['m_ffn2_w_down'], 'm_ln3_g': out['m_ln3_g'], 'm_ln3_b': out['m_ln3_b'], 'v_ln1_g': out['v_ln1_g'], 'v_ln1_b': out['v_ln1_b'], 'v_ffn1_w_gate': out['v_ffn1_w_gate'], 'v_ffn1_w_up': out['v_ffn1_w_up'], 'v_ffn1_w_down': out['v_ffn1_w_down'], 'v_mix_w_in': out['v_mix_w_in'], 'v_pool_w': out['v_pool_w'], 'v_pool_scale': out['v_pool_scale'], 'v_sconv_w': out['v_sconv_w'], 'v_cconv_w': out['v_cconv_w'], 'v_cconv_b': out['v_cconv_b'], 'v_cnorm_g': out['v_cnorm_g'], 'v_cnorm_b': out['v_cnorm_b'], 'v_mix_w_out': out['v_mix_w_out'], 'v_ln2_g': out['v_ln2_g'], 'v_ln2_b': out['v_ln2_b'], 'v_ffn2_w_gate': out['v_ffn2_w_gate'], 'v_ffn2_w_up': out['v_ffn2_w_up'], 'v_ffn2_w_down': out['v_ffn2_w_down'], 'v_ln3_g': out['v_ln3_g'], 'v_ln3_b': out['v_ln3_b']}


def _loss(weights, diff, rest, loss_target):
    with _jax.named_scope("forward"):
        args = {**rest, TWIN_DIFF_INPUT: diff, **{k: w.astype(_WEIGHT_DTYPES[k]) for k, w in weights.items()}}
        y = _forward(args)
    with _jax.named_scope("loss_head"):
        err = _jnp.square(y.astype(_jnp.float32) - loss_target)
        return 0.5 * _jnp.sum(_jnp.mean(err, axis=-1)) if err.ndim else 0.5 * err


def _adamw(w, g, m, v):
    m = ADAM_B1 * m + (1.0 - ADAM_B1) * g
    v = ADAM_B2 * v + (1.0 - ADAM_B2) * _jnp.square(g)
    m_hat = m / (1.0 - ADAM_B1 ** ADAM_STEP)
    v_hat = v / (1.0 - ADAM_B2 ** ADAM_STEP)
    delta = -ADAM_LR * (m_hat / (_jnp.sqrt(v_hat) + ADAM_EPS) + ADAM_WD * w)
    return delta, m, v


def reference(x, ln1_g, ln1_b, ffn1_w_gate, ffn1_w_up, ffn1_w_down, mix_w_in, pool_w, pool_scale, sconv_w, cconv_w, cconv_b, cnorm_g, cnorm_b, mix_w_out, ln2_g, ln2_b, ffn2_w_gate, ffn2_w_up, ffn2_w_down, ln3_g, ln3_b, loss_target, m_ln1_g, m_ln1_b, m_ffn1_w_gate, m_ffn1_w_up, m_ffn1_w_down, m_mix_w_in, m_pool_w, m_pool_scale, m_sconv_w, m_cconv_w, m_cconv_b, m_cnorm_g, m_cnorm_b, m_mix_w_out, m_ln2_g, m_ln2_b, m_ffn2_w_gate, m_ffn2_w_up, m_ffn2_w_down, m_ln3_g, m_ln3_b, v_ln1_g, v_ln1_b, v_ffn1_w_gate, v_ffn1_w_up, v_ffn1_w_down, v_mix_w_in, v_pool_w, v_pool_scale, v_sconv_w, v_cconv_w, v_cconv_b, v_cnorm_g, v_cnorm_b, v_mix_w_out, v_ln2_g, v_ln2_b, v_ffn2_w_gate, v_ffn2_w_up, v_ffn2_w_down, v_ln3_g, v_ln3_b):
    given = dict(x=x, ln1_g=ln1_g, ln1_b=ln1_b, ffn1_w_gate=ffn1_w_gate, ffn1_w_up=ffn1_w_up, ffn1_w_down=ffn1_w_down, mix_w_in=mix_w_in, pool_w=pool_w, pool_scale=pool_scale, sconv_w=sconv_w, cconv_w=cconv_w, cconv_b=cconv_b, cnorm_g=cnorm_g, cnorm_b=cnorm_b, mix_w_out=mix_w_out, ln2_g=ln2_g, ln2_b=ln2_b, ffn2_w_gate=ffn2_w_gate, ffn2_w_up=ffn2_w_up, ffn2_w_down=ffn2_w_down, ln3_g=ln3_g, ln3_b=ln3_b, loss_target=loss_target, m_ln1_g=m_ln1_g, m_ln1_b=m_ln1_b, m_ffn1_w_gate=m_ffn1_w_gate, m_ffn1_w_up=m_ffn1_w_up, m_ffn1_w_down=m_ffn1_w_down, m_mix_w_in=m_mix_w_in, m_pool_w=m_pool_w, m_pool_scale=m_pool_scale, m_sconv_w=m_sconv_w, m_cconv_w=m_cconv_w, m_cconv_b=m_cconv_b, m_cnorm_g=m_cnorm_g, m_cnorm_b=m_cnorm_b, m_mix_w_out=m_mix_w_out, m_ln2_g=m_ln2_g, m_ln2_b=m_ln2_b, m_ffn2_w_gate=m_ffn2_w_gate, m_ffn2_w_up=m_ffn2_w_up, m_ffn2_w_down=m_ffn2_w_down, m_ln3_g=m_ln3_g, m_ln3_b=m_ln3_b, v_ln1_g=v_ln1_g, v_ln1_b=v_ln1_b, v_ffn1_w_gate=v_ffn1_w_gate, v_ffn1_w_up=v_ffn1_w_up, v_ffn1_w_down=v_ffn1_w_down, v_mix_w_in=v_mix_w_in, v_pool_w=v_pool_w, v_pool_scale=v_pool_scale, v_sconv_w=v_sconv_w, v_cconv_w=v_cconv_w, v_cconv_b=v_cconv_b, v_cnorm_g=v_cnorm_g, v_cnorm_b=v_cnorm_b, v_mix_w_out=v_mix_w_out, v_ln2_g=v_ln2_g, v_ln2_b=v_ln2_b, v_ffn2_w_gate=v_ffn2_w_gate, v_ffn2_w_up=v_ffn2_w_up, v_ffn2_w_down=v_ffn2_w_down, v_ln3_g=v_ln3_g, v_ln3_b=v_ln3_b)
    weights = {n: given[n] for n in TWIN_WEIGHTS}
    shared = {n: given[n] for n in SHARED_INPUTS}
    per_example = {n: given[n] for n in ['x']}
    grad_fn = _jax.value_and_grad(_loss, argnums=(0, 1))

    def one_microbatch(ex, loss_target):
        ex = dict(ex)
        diff = ex.pop(TWIN_DIFF_INPUT)
        return grad_fn(weights, diff, {**shared, **ex}, loss_target)

    if N_MICROBATCH == 1:
        loss, (grad_w, grad_x) = one_microbatch(per_example, given["loss_target"])
    else:
        def body(carry, xs):
            loss_sum, grad_sum = carry
            l_k, (gw_k, gx_k) = one_microbatch(xs[0], xs[1])
            with _jax.named_scope("update"):
                return (loss_sum + l_k, _jax.tree.map(_jnp.add, grad_sum, gw_k)), gx_k

        init = (_jnp.zeros((), _jnp.float32), _jax.tree.map(_jnp.zeros_like, weights))
        (loss, grad_w), grad_x = _jax.lax.scan(body, init, (per_example, given["loss_target"]))
    with _jax.named_scope("update"):
        delta_w, new_m, new_v = {}, {}, {}
        for n in TWIN_WEIGHTS:
            delta_w[n], new_m[n], new_v[n] = _adamw(weights[n], grad_w[n], given["m_" + n], given["v_" + n])
    return (loss, grad_x, *[grad_w[n] for n in TWIN_WEIGHTS], *[delta_w[n] for n in TWIN_WEIGHTS],
            *[new_m[n] for n in TWIN_WEIGHTS], *[new_v[n] for n in TWIN_WEIGHTS])
```

```python
import functools

import jax
import jax.numpy as jnp
from jax import lax
from jax.experimental import pallas as pl
from jax.experimental.pallas import tpu as pltpu

F32 = jnp.float32
MM = jnp.bfloat16
LN_EPS = 1e-5
N_DEV = 8
MESH_AXES = ("x", "y", "c")
HALO = 32
CCONV_K = 31
SCONV_K = 3
CONV_ROWS = 32
VMEM_LIMIT = 56 * 1024 * 1024
ADAM_LR, ADAM_B1, ADAM_B2, ADAM_EPS, ADAM_WD, ADAM_STEP = 0.001, 0.9, 0.999, 1e-08, 0.01, 10
MESH = pl.DeviceIdType.MESH


def _call(body, *, name, out_shape, in_specs, out_specs, grid=None, scratch=(), sem=None):
    kw = {}
    if grid is not None:
        kw["grid"] = grid
    params = dict(vmem_limit_bytes=VMEM_LIMIT)
    if sem is not None:
        params["dimension_semantics"] = sem
    return pl.pallas_call(body, name=name, out_shape=out_shape, in_specs=in_specs, out_specs=out_specs,
                          scratch_shapes=list(scratch), compiler_params=pltpu.CompilerParams(**params), **kw)


def _rows(tm, n):
    return pl.BlockSpec((tm, n), lambda i: (i, 0))


def _const(shape):
    nd = len(shape)
    return pl.BlockSpec(shape, lambda *_: (0,) * nd, pipeline_mode=pl.Buffered(1))


def _acc(shape):
    nd = len(shape)
    return pl.BlockSpec(shape, lambda *_: (0,) * nd)


def _row_tile(rows, cap):
    best = None
    for t in range(8, min(rows, cap) + 1, 8):
        if rows % t == 0:
            best = t
    return best if best is not None else rows


def _sig(x):
    return 1.0 / (1.0 + jnp.exp(-x))


def _ln_stats(z):
    mu = jnp.mean(z, axis=-1, keepdims=True)
    zc = z - mu
    var = jnp.mean(zc * zc, axis=-1, keepdims=True)
    rstd = lax.rsqrt(var + LN_EPS)
    return zc * rstd, rstd


def _ln_bwd(dout, xhat, rstd, gamma):
    dxh = dout * gamma
    m1 = jnp.mean(dxh, axis=-1, keepdims=True)
    m2 = jnp.mean(dxh * xhat, axis=-1, keepdims=True)
    return rstd * (dxh - m1 - xhat * m2)


def _colsum(v):
    return jnp.sum(v, axis=0, keepdims=True)


def _f_chunks(f):
    n = 2 if f >= 2048 and f % 256 == 0 else 1
    return n, f // n


def ffn_up(xin, gam, bet, wg, wu, *, tm):
    t, d = xin.shape
    f = wg.shape[1]
    nch, fc = _f_chunks(f)

    def body(x_ref, g_ref, b_ref, wg_ref, wu_ref, xb_ref, go_ref, uo_ref, h_ref):
        xb = (x_ref[...] * g_ref[...] + b_ref[...]).astype(MM)
        xb_ref[...] = xb
        for c in range(nch):
            sl = slice(c * fc, (c + 1) * fc)
            g = jnp.dot(xb, wg_ref[:, sl], preferred_element_type=F32)
            u = jnp.dot(xb, wu_ref[:, sl], preferred_element_type=F32)
            go_ref[:, sl] = g.astype(MM)
            uo_ref[:, sl] = u.astype(MM)
            h_ref[:, sl] = (g * _sig(g) * u).astype(MM)

    return _call(
        body, name="ffn_up", grid=(t // tm,), sem=("parallel",),
        in_specs=[_rows(tm, d), _const((1, d)), _const((1, d)), _const((d, f)), _const((d, f))],
        out_specs=[_rows(tm, d), _rows(tm, f), _rows(tm, f), _rows(tm, f)],
        out_shape=[jax.ShapeDtypeStruct((t, d), MM)] + [jax.ShapeDtypeStruct((t, f), MM)] * 3,
    )(xin, gam, bet, wg, wu)


def mm_res_ln(a, w, xin, gam, bet, *, alpha, scale, tm):
    t, k = a.shape
    d = w.shape[1]

    def body(a_ref, w_ref, x_ref, g_ref, b_ref, xh_ref, rs_ref):
        y = jnp.dot(a_ref[...], w_ref[...], preferred_element_type=F32)
        x = x_ref[...] * g_ref[...] + b_ref[...]
        xh, rstd = _ln_stats(alpha * x + scale * y)
        xh_ref[...] = xh
        rs_ref[...] = rstd

    return _call(
        body, name="mm_res_ln", grid=(t // tm,), sem=("parallel",),
        in_specs=[_rows(tm, k), _const((k, d)), _rows(tm, d), _const((1, d)), _const((1, d))],
        out_specs=[_rows(tm, d), _rows(tm, 1)],
        out_shape=[jax.ShapeDtypeStruct((t, d), F32), jax.ShapeDtypeStruct((t, 1), F32)],
    )(a, w, xin, gam, bet)


def mm_in(xin, gam, bet, w, *, tm):
    t, d = xin.shape
    n = w.shape[1]

    def body(x_ref, g_ref, b_ref, w_ref, o_ref, xb_ref):
        xb = (x_ref[...] * g_ref[...] + b_ref[...]).astype(MM)
        xb_ref[...] = xb
        o_ref[...] = jnp.dot(xb, w_ref[...], preferred_element_type=F32)

    return _call(
        body, name="mm_in", grid=(t // tm,), sem=("parallel",),
        in_specs=[_rows(tm, d), _const((1, d)), _const((1, d)), _const((d, n))],
        out_specs=[_rows(tm, n), _rows(tm, d)],
        out_shape=[jax.ShapeDtypeStruct((t, n), F32), jax.ShapeDtypeStruct((t, d), MM)],
    )(xin, gam, bet, w)


def loss_head(xh, gam, bet, target, *, tm):
    t, d = xh.shape

    def body(x_ref, g_ref, b_ref, t_ref, dy_ref, l_ref):
        @pl.when(pl.program_id(0) == 0)
        def _():
            l_ref[...] = jnp.zeros_like(l_ref)

        e = x_ref[...] * g_ref[...] + b_ref[...] - t_ref[...]
        dy_ref[...] = e * (1.0 / d)
        l_ref[...] += jnp.sum(_colsum(e * e), axis=1, keepdims=True)

    return _call(
        body, name="loss_head", grid=(t // tm,), sem=("arbitrary",),
        in_specs=[_rows(tm, d), _const((1, d)), _const((1, d)), _rows(tm, d)],
        out_specs=[_rows(tm, d), _acc((1, 1))],
        out_shape=[jax.ShapeDtypeStruct((t, d), F32), jax.ShapeDtypeStruct((1, 1), F32)],
    )(xh, gam, bet, target)


def _halo_specs(tt, ncols, t):
    per, last = tt // HALO, t // HALO - 1
    return [pl.BlockSpec((HALO, ncols), lambda i: (jnp.maximum(i * per - 1, 0), 0)),
            pl.BlockSpec((tt, ncols), lambda i: (i, 0)),
            pl.BlockSpec((HALO, ncols), lambda i: (jnp.minimum((i + 1) * per, last), 0))]


def _fill_ext(ext, prev_ref, cur_ref, next_ref, i, nt, tt):
    ext[0:HALO, :] = jnp.where(i > 0, prev_ref[...], 0.0)
    ext[HALO:HALO + tt, :] = cur_ref[...]
    ext[HALO + tt:HALO + tt + HALO, :] = jnp.where(i < nt - 1, next_ref[...], 0.0)


def _conv_chunks(src, w_ref, ktaps, src0, nrows, flip=False):
    for r0 in range(0, nrows, CONV_ROWS):
        acc = None
        for k in range(ktaps):
            kk = ktaps - 1 - k if flip else k
            term = w_ref[kk:kk + 1, :] * src[src0 + r0 + k:src0 + r0 + k + CONV_ROWS, :]
            acc = term if acc is None else acc + term
        yield r0, acc


def _pool_groups(nrows, p):
    lane = lax.broadcasted_iota(jnp.int32, (nrows, p), 1)
    g = p // 4
    return lane < g, lane < 2 * g, lane < 3 * g


def _pool_select(groups, v2, v4, v8, v16):
    g0, g1, g2 = groups
    return jnp.where(g0, v2, jnp.where(g1, v4, jnp.where(g2, v8, v16)))


def _pool_count(groups, i, tt, row0, nrows, p, t):
    pos = i * tt + (row0 - HALO) + lax.broadcasted_iota(jnp.int32, (nrows, p), 0)
    half = _pool_select(groups, 1, 2, 4, 8)
    lo = jnp.clip(pos - half, 0, t)
    hi = jnp.clip(pos + half, 0, t)
    return jnp.maximum(hi - lo, 1).astype(F32)


def _pool_forward(ext, s2, s4, s8, groups, cnt, tt, p):
    e = tt + 2 * HALO
    s2[8:e - 8, :] = ext[7:e - 9, 0:p] + ext[8:e - 8, 0:p]
    s4[16:e - 16, :] = s2[15:e - 17, :] + s2[17:e - 15, :]
    s8[24:e - 24, :] = s4[22:e - 26, :] + s4[26:e - 22, :]
    s16 = s8[28:e - 36, :] + s8[36:e - 28, :]
    c = slice(HALO, HALO + tt)
    tot = _pool_select(groups, s2[c, :], s4[c, :], s8[c, :], s16)
    return tot / cnt - ext[c, 0:p]


def mixer_fwd(proj, wbd, pscale, sw, cw, cb, cg, cbt, *, d, tt):
    t, ncols = proj.shape
    p, s = d // 4, 3 * d // 8
    o_gb, o_gc, o_v, o_cv, o_cg = p, p + s, p + 2 * s, p + 3 * s, p + 4 * s
    nt, e = t // tt, tt + 2 * HALO

    def body(prev_ref, cur_ref, next_ref, wbd_ref, ps_ref, sw_ref, cw_ref, cb_ref, cg_ref, cbt_ref,
             cat_ref, ext, a_s, cv_s, s2, s4, s8):
        i = pl.program_id(0)
        _fill_ext(ext, prev_ref, cur_ref, next_ref, i, nt, tt)
        c = slice(HALO, HALO + tt)
        a_s[...] = ext[:, o_cv:o_cv + s] * _sig(ext[:, o_cg:o_cg + s])
        for r0, acc in _conv_chunks(a_s, cw_ref, CCONV_K, HALO - CCONV_K // 2, tt):
            n, _ = _ln_stats(acc + cb_ref[...])
            yn = n * cg_ref[...] + cbt_ref[...]
            cat_ref[r0:r0 + CONV_ROWS, p + s:d] = (yn * _sig(yn)).astype(MM)
        cv_s[...] = ext[:, o_gc:o_gc + s] * ext[:, o_v:o_v + s]
        for r0, acc in _conv_chunks(cv_s, sw_ref, SCONV_K, HALO - SCONV_K // 2, tt):
            gb = ext[HALO + r0:HALO + r0 + CONV_ROWS, o_gb:o_gb + s]
            cat_ref[r0:r0 + CONV_ROWS, p:p + s] = (gb * acc).astype(MM)
        groups = _pool_groups(tt, p)
        cnt = _pool_count(groups, i, tt, HALO, tt, p, t)
        pooled = _pool_forward(ext, s2, s4, s8, groups, cnt, tt, p)
        ya = jnp.dot(pooled.astype(MM), wbd_ref[...], preferred_element_type=F32) * ps_ref[...]
        cat_ref[:, 0:p] = ya.astype(MM)

    return _call(
        body, name="mixer_fwd", grid=(nt,), sem=("parallel",),
        in_specs=_halo_specs(tt, ncols, t) + [_const((p, p)), _const((1, p)), _const((8, s)), _const((32, s)),
                                               _const((1, s)), _const((1, s)), _const((1, s))],
        out_specs=_rows(tt, d),
        out_shape=jax.ShapeDtypeStruct((t, d), MM),
        scratch=[pltpu.VMEM((e, ncols), F32), pltpu.VMEM((e, s), F32), pltpu.VMEM((e, s), F32),
                 pltpu.VMEM((e, p), F32), pltpu.VMEM((e, p), F32), pltpu.VMEM((e, p), F32)],
    )(proj, proj, proj, wbd, pscale, sw, cw, cb, cg, cbt)


def mixer_bwd(proj, dcat, wbd, wbdt, pscale, sw, cw, cb, cg, cbt, *, d, tt):
    t, ncols = proj.shape
    p, s = d // 4, 3 * d // 8
    o_gb, o_gc, o_v, o_cv, o_cg = p, p + s, p + 2 * s, p + 3 * s, p + 4 * s
    nt, e = t // tt, tt + 2 * HALO
    pad = 16

    def body(pp_ref, pc_ref, pn_ref, dp_ref, dc_ref, dn_ref, wbd_ref, wbdt_ref, ps_ref, sw_ref, cw_ref,
             cb_ref, cg_ref, cbt_ref,
             dproj_ref, dwbd_ref, dps_ref, dsw_ref, dcw_ref, dcb_ref, dcg_ref, dcbt_ref,
             ext, dext, a_s, sg_s, b_s, cv_s, ds_s, q_s, r2, r4, r8):
        i = pl.program_id(0)

        @pl.when(i == 0)
        def _():
            for ref in (dwbd_ref, dps_ref, dsw_ref, dcw_ref, dcb_ref, dcg_ref, dcbt_ref):
                ref[...] = jnp.zeros_like(ref)

        _fill_ext(ext, pp_ref, pc_ref, pn_ref, i, nt, tt)
        _fill_ext(dext, dp_ref, dc_ref, dn_ref, i, nt, tt)
        c = slice(HALO, HALO + tt)

        sg_s[...] = _sig(ext[:, o_cg:o_cg + s])
        a_s[0:pad, :] = jnp.zeros((pad, s), F32)
        a_s[pad + e:pad + e + pad, :] = jnp.zeros((pad, s), F32)
        a_s[pad:pad + e, :] = ext[:, o_cv:o_cv + s] * sg_s[...]
        for r0, acc in _conv_chunks(a_s, cw_ref, CCONV_K, pad - CCONV_K // 2, e):
            b = acc + cb_ref[...]
            n, rstd = _ln_stats(b)
            yn = n * cg_ref[...] + cbt_ref[...]
            sy = _sig(yn)
            rows = slice(r0, r0 + CONV_ROWS)
            dyn = dext[rows, p + s:d] * (sy * (1.0 + yn * (1.0 - sy)))
            db = _ln_bwd(dyn, n, rstd, cg_ref[...])
            b_s[rows, :] = db
            if HALO <= r0 < HALO + tt:
                dcg_ref[...] += _colsum(dyn * n)
                dcbt_ref[...] += _colsum(dyn)
                dcb_ref[...] += _colsum(db)
        for r0, da in _conv_chunks(b_s, cw_ref, CCONV_K, HALO - CCONV_K // 2, tt, flip=True):
            rows = slice(HALO + r0, HALO + r0 + CONV_ROWS)
            sg = sg_s[rows, :]
            dproj_ref[r0:r0 + CONV_ROWS, o_cv:o_cv + s] = (da * sg).astype(MM)
            dproj_ref[r0:r0 + CONV_ROWS, o_cg:o_cg + s] = (
                da * ext[rows, o_cv:o_cv + s] * sg * (1.0 - sg)).astype(MM)
        for k in range(CCONV_K):
            lo = pad + HALO + k - CCONV_K // 2
            dcw_ref[k:k + 1, :] += _colsum(b_s[c, :] * a_s[lo:lo + tt, :])

        cv_s[...] = ext[:, o_gc:o_gc + s] * ext[:, o_v:o_v + s]
        ds_s[...] = dext[:, p:p + s] * ext[:, o_gb:o_gb + s]
        for r0, acc in _conv_chunks(cv_s, sw_ref, SCONV_K, HALO - SCONV_K // 2, tt):
            rows = slice(HALO + r0, HALO + r0 + CONV_ROWS)
            dproj_ref[r0:r0 + CONV_ROWS, o_gb:o_gb + s] = (dext[rows, p:p + s] * acc).astype(MM)
        for r0, dcv in _conv_chunks(ds_s, sw_ref, SCONV_K, HALO - SCONV_K // 2, tt, flip=True):
            rows = slice(HALO + r0, HALO + r0 + CONV_ROWS)
            dproj_ref[r0:r0 + CONV_ROWS, o_gc:o_gc + s] = (dcv * ext[rows, o_v:o_v + s]).astype(MM)
            dproj_ref[r0:r0 + CONV_ROWS, o_v:o_v + s] = (dcv * ext[rows, o_gc:o_gc + s]).astype(MM)
        for k in range(SCONV_K):
            lo = HALO + k - SCONV_K // 2
            dsw_ref[k:k + 1, :] += _colsum(ds_s[c, :] * cv_s[lo:lo + tt, :])

        groups = _pool_groups(tt, p)
        cnt = _pool_count(groups, i, tt, HALO, tt, p, t)
        pooled = _pool_forward(ext, r2, r4, r8, groups, cnt, tt, p).astype(MM)
        ta = jnp.dot(pooled, wbd_ref[...], preferred_element_type=F32)
        dps_ref[...] += _colsum(dext[c, 0:p] * ta)
        dta = (dext[:, 0:p] * ps_ref[...]).astype(MM)
        dwbd_ref[...] += lax.dot_general(pooled, dta[HALO:HALO + tt, :], (((0,), (0,)), ((), ())),
                                         preferred_element_type=F32)
        dpool = jnp.dot(dta, wbdt_ref[...], preferred_element_type=F32)
        groups_e = _pool_groups(e, p)
        q_s[...] = dpool / _pool_count(groups_e, i, tt, 0, e, p, t)
        r2[8:e - 8, :] = q_s[8:e - 8, :] + q_s[9:e - 7, :]
        r4[16:e - 16, :] = r2[15:e - 17, :] + r2[17:e - 15, :]
        r8[24:e - 24, :] = r4[22:e - 26, :] + r4[26:e - 22, :]
        r16 = r8[28:e - 36, :] + r8[36:e - 28, :]
        du = _pool_select(groups, r2[c, :], r4[c, :], r8[c, :], r16) - dpool[HALO:HALO + tt, :]
        dproj_ref[:, 0:p] = du.astype(MM)

    small = [(p, p), (1, p), (8, s), (32, s), (1, s), (1, s), (1, s)]
    return _call(
        body, name="mixer_bwd", grid=(nt,), sem=("arbitrary",),
        in_specs=_halo_specs(tt, ncols, t) + _halo_specs(tt, d, t) + [
            _const((p, p)), _const((p, p)), _const((1, p)), _const((8, s)), _const((32, s)),
            _const((1, s)), _const((1, s)), _const((1, s))],
        out_specs=[_rows(tt, ncols)] + [_acc(sh) for sh in small],
        out_shape=[jax.ShapeDtypeStruct((t, ncols), MM)] + [jax.ShapeDtypeStruct(sh, F32) for sh in small],
        scratch=[pltpu.VMEM((e, ncols), F32), pltpu.VMEM((e, d), F32), pltpu.VMEM((e + 2 * pad, s), F32)]
        + [pltpu.VMEM((e, s), F32)] * 4 + [pltpu.VMEM((e, p), F32)] * 4,
    )(proj, proj, proj, dcat, dcat, dcat, wbd, wbdt, pscale, sw, cw, cb, cg, cbt)


def ffn_bwd(dout, xh, rs, gam, wdt, wgt, wut, g, u, *, alpha, tm):
    t, d = dout.shape
    f = g.shape[1]
    nch, fc = _f_chunks(f)

    def body(do_ref, xh_ref, rs_ref, gm_ref, wdt_ref, wgt_ref, wut_ref, g_ref, u_ref,
             dx_ref, dyb_ref, dg_ref, du_ref, dgam_ref, dbet_ref):
        @pl.when(pl.program_id(0) == 0)
        def _():
            dgam_ref[...] = jnp.zeros_like(dgam_ref)
            dbet_ref[...] = jnp.zeros_like(dbet_ref)

        dout_v, xhat = do_ref[...], xh_ref[...]
        dgam_ref[...] += _colsum(dout_v * xhat)
        dbet_ref[...] += _colsum(dout_v)
        dz = _ln_bwd(dout_v, xhat, rs_ref[...], gm_ref[...])
        dyb = (0.5 * dz).astype(MM)
        dyb_ref[...] = dyb
        acc = alpha * dz
        for c in range(nch):
            sl = slice(c * fc, (c + 1) * fc)
            dh = jnp.dot(dyb, wdt_ref[:, sl], preferred_element_type=F32)
            gv, uv = g_ref[:, sl].astype(F32), u_ref[:, sl].astype(F32)
            sg = _sig(gv)
            dg = (dh * uv * (sg * (1.0 + gv * (1.0 - sg)))).astype(MM)
            du = (dh * (gv * sg)).astype(MM)
            dg_ref[:, sl] = dg
            du_ref[:, sl] = du
            acc += jnp.dot(dg, wgt_ref[sl, :], preferred_element_type=F32)
            acc += jnp.dot(du, wut_ref[sl, :], preferred_element_type=F32)
        dx_ref[...] = acc

    return _call(
        body, name="ffn_bwd", grid=(t // tm,), sem=("arbitrary",),
        in_specs=[_rows(tm, d), _rows(tm, d), _rows(tm, 1), _const((1, d)), _const((d, f)), _const((f, d)),
                  _const((f, d)), _rows(tm, f), _rows(tm, f)],
        out_specs=[_rows(tm, d), _rows(tm, d), _rows(tm, f), _rows(tm, f), _acc((1, d)), _acc((1, d))],
        out_shape=[jax.ShapeDtypeStruct((t, d), F32), jax.ShapeDtypeStruct((t, d), MM),
                   jax.ShapeDtypeStruct((t, f), MM), jax.ShapeDtypeStruct((t, f), MM),
                   jax.ShapeDtypeStruct((1, d), F32), jax.ShapeDtypeStruct((1, d), F32)],
    )(dout, xh, rs, gam, wdt, wgt, wut, g, u)


def lnbwd_mm(dout, xh, rs, gam, wt, *, tm):
    t, d = dout.shape
    n = wt.shape[1]

    def body(do_ref, xh_ref, rs_ref, gm_ref, wt_ref, dz_ref, dzb_ref, da_ref, dgam_ref, dbet_ref):
        @pl.when(pl.program_id(0) == 0)
        def _():
            dgam_ref[...] = jnp.zeros_like(dgam_ref)
            dbet_ref[...] = jnp.zeros_like(dbet_ref)

        dout_v, xhat = do_ref[...], xh_ref[...]
        dgam_ref[...] += _colsum(dout_v * xhat)
        dbet_ref[...] += _colsum(dout_v)
        dz = _ln_bwd(dout_v, xhat, rs_ref[...], gm_ref[...])
        dz_ref[...] = dz
        dzb = dz.astype(MM)
        dzb_ref[...] = dzb
        da_ref[...] = jnp.dot(dzb, wt_ref[...], preferred_element_type=F32)

    return _call(
        body, name="lnbwd_mm", grid=(t // tm,), sem=("arbitrary",),
        in_specs=[_rows(tm, d), _rows(tm, d), _rows(tm, 1), _const((1, d)), _const((d, n))],
        out_specs=[_rows(tm, d), _rows(tm, d), _rows(tm, n), _acc((1, d)), _acc((1, d))],
        out_shape=[jax.ShapeDtypeStruct((t, d), F32), jax.ShapeDtypeStruct((t, d), MM),
                   jax.ShapeDtypeStruct((t, n), F32), jax.ShapeDtypeStruct((1, d), F32),
                   jax.ShapeDtypeStruct((1, d), F32)],
    )(dout, xh, rs, gam, wt)


def mm_add(a, w, r, *, alpha, tm):
    t, k = a.shape
    d = w.shape[1]

    def body(a_ref, w_ref, r_ref, o_ref):
        o_ref[...] = jnp.dot(a_ref[...], w_ref[...], preferred_element_type=F32) + alpha * r_ref[...]

    return _call(
        body, name="mm_add", grid=(t // tm,), sem=("parallel",),
        in_specs=[_rows(tm, k), _const((k, d)), _rows(tm, d)],
        out_specs=_rows(tm, d),
        out_shape=jax.ShapeDtypeStruct((t, d), F32),
    )(a, w, r)


def tn_matmul(a, b, *, tm):
    t, n = a.shape
    d = b.shape[1]
    nch, nc = _f_chunks(n)

    def body(a_ref, b_ref, o_ref):
        @pl.when(pl.program_id(1) == 0)
        def _():
            o_ref[...] = jnp.zeros_like(o_ref)

        o_ref[...] += lax.dot_general(a_ref[...], b_ref[...], (((0,), (0,)), ((), ())),
                                      preferred_element_type=F32)

    return _call(
        body, name="tn_matmul", grid=(nch, t // tm), sem=("parallel", "arbitrary"),
        in_specs=[pl.BlockSpec((tm, nc), lambda j, i: (i, j)), pl.BlockSpec((tm, d), lambda j, i: (i, 0))],
        out_specs=pl.BlockSpec((nc, d), lambda j, i: (j, 0)),
        out_shape=jax.ShapeDtypeStruct((n, d), F32),
    )(a, b)


def sum_parts(own, recv, keep):
    r, d = own.shape
    tr = _row_tile(r, 256)

    def body(keep_ref, own_ref, recv_ref, o_ref):
        acc = own_ref[...]
        for j in range(N_DEV):
            acc = acc + recv_ref[j].astype(F32) * keep_ref[j]
        o_ref[...] = acc

    return _call(
        body, name="sum_parts", grid=(r // tr,), sem=("parallel",),
        in_specs=[pl.BlockSpec(memory_space=pltpu.SMEM), _rows(tr, d),
                  pl.BlockSpec((N_DEV, tr, d), lambda i: (0, i, 0))],
        out_specs=_rows(tr, d),
        out_shape=jax.ShapeDtypeStruct((r, d), F32),
    )(keep, own, recv)


def sum_gathered(parts):
    _, r, n = parts.shape

    def body(p_ref, o_ref):
        acc = p_ref[0]
        for j in range(1, N_DEV):
            acc = acc + p_ref[j]
        o_ref[...] = acc

    return _call(
        body, name="sum_gathered",
        in_specs=[pl.BlockSpec(memory_space=pltpu.VMEM)], out_specs=pl.BlockSpec(memory_space=pltpu.VMEM),
        out_shape=jax.ShapeDtypeStruct((r, n), F32),
    )(parts)


def adamw(w, g, m, v):
    r, c = w.shape
    tr = _row_tile(r, 512)

    def body(w_ref, g_ref, m_ref, v_ref, d_ref, mo_ref, vo_ref):
        gv = g_ref[...]
        mn = ADAM_B1 * m_ref[...] + (1.0 - ADAM_B1) * gv
        vn = ADAM_B2 * v_ref[...] + (1.0 - ADAM_B2) * (gv * gv)
        m_hat = mn / (1.0 - ADAM_B1 ** ADAM_STEP)
        v_hat = vn / (1.0 - ADAM_B2 ** ADAM_STEP)
        d_ref[...] = -ADAM_LR * (m_hat / (jnp.sqrt(v_hat) + ADAM_EPS) + ADAM_WD * w_ref[...])
        mo_ref[...] = mn
        vo_ref[...] = vn

    return _call(
        body, name="adamw", grid=(r // tr,), sem=("parallel",),
        in_specs=[_rows(tr, c)] * 4, out_specs=[_rows(tr, c)] * 3,
        out_shape=[jax.ShapeDtypeStruct((r, c), F32)] * 3,
    )(w, g, m, v)


def _mesh_pos():
    return lax.axis_index("x"), lax.axis_index("y"), lax.axis_index("c")


def _two_level_gather(x_ref, out_ref, send_sems, recv_sems, local_sem):
    x, y, c = _mesh_pos()
    me, sibling = (x, y, c), (x, y, 1 - c)
    chips = [(1 - x, y), (x, 1 - y), (1 - x, 1 - y)]

    def slot(px, py, pc):
        return out_ref.at[4 * px + 2 * py + pc]

    def copy(k, block, to, src=None):
        return pltpu.make_async_remote_copy(
            src_ref=slot(*block) if src is None else src, dst_ref=slot(*block),
            send_sem=send_sems.at[k], recv_sem=recv_sems.at[k], device_id=to, device_id_type=MESH)

    mine = pltpu.make_async_copy(x_ref, slot(*me), local_sem)
    mine.start()
    first = [copy(1 + j, me, (*chip, c), src=x_ref) for j, chip in enumerate(chips)]
    first.append(copy(0, me, sibling, src=x_ref))
    for cp in first:
        cp.start()
    passed = [copy(4 + j, (*chip, c), sibling) for j, chip in enumerate(chips)]
    for j, chip in enumerate(chips):
        copy(1 + j, (*chip, c), me).wait_recv()
        passed[j].start()
    copy(0, sibling, me).wait_recv()
    for j, chip in enumerate(chips):
        copy(4 + j, (*chip, 1 - c), me).wait_recv()
    for cp in first + passed:
        cp.wait_send()
    mine.wait()


def gather_small(x):
    return _call(
        _two_level_gather_body(), name="gather_small",
        in_specs=[pl.BlockSpec(memory_space=pltpu.VMEM)], out_specs=pl.BlockSpec(memory_space=pltpu.VMEM),
        out_shape=jax.ShapeDtypeStruct((N_DEV,) + x.shape, x.dtype),
        scratch=[pltpu.SemaphoreType.DMA((7,)), pltpu.SemaphoreType.DMA((7,)), pltpu.SemaphoreType.DMA(())],
    )(x)


def gather_big(x):
    return _call(
        _two_level_gather_body(), name="gather_big",
        in_specs=[pl.BlockSpec(memory_space=pl.ANY)], out_specs=pl.BlockSpec(memory_space=pl.ANY),
        out_shape=jax.ShapeDtypeStruct((N_DEV,) + x.shape, x.dtype),
        scratch=[pltpu.SemaphoreType.DMA((7,)), pltpu.SemaphoreType.DMA((7,)), pltpu.SemaphoreType.DMA(())],
    )(x)


def _two_level_gather_body():
    def body(x_ref, out_ref, send_sems, recv_sems, local_sem):
        _two_level_gather(x_ref, out_ref, send_sems, recv_sems, local_sem)
    return body


def exchange_big(send):
    def body(s_ref, r_ref, send_sems, recv_sems, local_sem):
        x, y, c = _mesh_pos()
        me = 4 * x + 2 * y + c
        mine = pltpu.make_async_copy(s_ref.at[me], r_ref.at[me], local_sem)
        mine.start()
        copies = []
        for k in (4, 2, 6, 5, 3, 7, 1):
            px = 1 - x if k & 4 else x
            py = 1 - y if k & 2 else y
            pc = 1 - c if k & 1 else c
            cp = pltpu.make_async_remote_copy(
                src_ref=s_ref.at[4 * px + 2 * py + pc], dst_ref=r_ref.at[me],
                send_sem=send_sems.at[k - 1], recv_sem=recv_sems.at[k - 1],
                device_id=(px, py, pc), device_id_type=MESH)
            cp.start()
            copies.append(cp)
        for cp in copies:
            cp.wait()
        mine.wait()

    return _call(
        body, name="exchange_big",
        in_specs=[pl.BlockSpec(memory_space=pl.ANY)], out_specs=pl.BlockSpec(memory_space=pl.ANY),
        out_shape=jax.ShapeDtypeStruct(send.shape, send.dtype),
        scratch=[pltpu.SemaphoreType.DMA((7,)), pltpu.SemaphoreType.DMA((7,)), pltpu.SemaphoreType.DMA(())],
    )(send)


WEIGHTS = ['ln1_g', 'ln1_b', 'ffn1_w_gate', 'ffn1_w_up', 'ffn1_w_down', 'mix_w_in', 'pool_w', 'pool_scale',
           'sconv_w', 'cconv_w', 'cconv_b', 'cnorm_g', 'cnorm_b', 'mix_w_out', 'ln2_g', 'ln2_b',
           'ffn2_w_gate', 'ffn2_w_up', 'ffn2_w_down', 'ln3_g', 'ln3_b']


def _pack_small(pieces):
    flat = jnp.concatenate([p.reshape(-1) for p in pieces])
    pad = -flat.shape[0] % 1024
    return jnp.pad(flat, (0, pad)).reshape(-1, 128)


def _unpack_small(flat, shapes):
    out, off = [], 0
    for sh in shapes:
        n = 1
        for s in sh:
            n *= s
        out.append(flat[off:off + n].reshape(sh))
        off += n
    return out


def _pad_rows(a, rows):
    return jnp.pad(a, ((0, rows - a.shape[0]), (0, 0)))


def kernel(x, ln1_g, ln1_b, ffn1_w_gate, ffn1_w_up, ffn1_w_down, mix_w_in, pool_w, pool_scale, sconv_w, cconv_w, cconv_b, cnorm_g, cnorm_b, mix_w_out, ln2_g, ln2_b, ffn2_w_gate, ffn2_w_up, ffn2_w_down, ln3_g, ln3_b, loss_target, m_ln1_g, m_ln1_b, m_ffn1_w_gate, m_ffn1_w_up, m_ffn1_w_down, m_mix_w_in, m_pool_w, m_pool_scale, m_sconv_w, m_cconv_w, m_cconv_b, m_cnorm_g, m_cnorm_b, m_mix_w_out, m_ln2_g, m_ln2_b, m_ffn2_w_gate, m_ffn2_w_up, m_ffn2_w_down, m_ln3_g, m_ln3_b, v_ln1_g, v_ln1_b, v_ffn1_w_gate, v_ffn1_w_up, v_ffn1_w_down, v_mix_w_in, v_pool_w, v_pool_scale, v_sconv_w, v_cconv_w, v_cconv_b, v_cnorm_g, v_cnorm_b, v_mix_w_out, v_ln2_g, v_ln2_b, v_ffn2_w_gate, v_ffn2_w_up, v_ffn2_w_down, v_ln3_g, v_ln3_b):
    a = dict(locals())
    depth, d = ln1_g.shape
    t = x.shape[1]
    fs = ffn1_w_gate.shape[2]
    f = fs * N_DEV
    ins = mix_w_in.shape[2]
    ncols = ins * N_DEV
    outs = mix_w_out.shape[1]
    p, s = d // 4, 3 * d // 8
    pg = p // 4
    cs = sconv_w.shape[2]
    alpha = (2.0 * depth) ** 0.25
    me = 4 * lax.axis_index("x") + 2 * lax.axis_index("y") + lax.axis_index("c")
    tm = min(512, t)
    tm_bwd = min(256, t)
    tm_tn = min(1024, t)
    tt_fwd = min(512, t)
    tt_bwd = min(256, t)

    per_layer = [("wg1", fs), ("wu1", fs), ("wd1", fs), ("win", ins), ("wout", outs),
                 ("wg2", fs), ("wu2", fs), ("wd2", fs)]
    rows_l = sum(n for _, n in per_layer)

    def layer_rows(l):
        return [ffn1_w_gate[l].T, ffn1_w_up[l].T, ffn1_w_down[l], mix_w_in[l].T, mix_w_out[l],
                ffn2_w_gate[l].T, ffn2_w_up[l].T, ffn2_w_down[l]]

    wsend = jnp.concatenate([blk.astype(MM) for l in range(depth) for blk in layer_rows(l)], axis=0)
    wall = gather_big(wsend)

    def gathered(l, key):
        off = l * rows_l
        for name, n in per_layer:
            if name == key:
                return wall[:, off:off + n, :].reshape(N_DEV * n, d)
            off += n
        raise KeyError(key)

    conv_all = gather_small(_pack_small([sconv_w, cconv_w]))
    sw_parts, cw_parts = [], []
    for j in range(N_DEV):
        sw_j, cw_j = _unpack_small(conv_all[j].reshape(-1), [sconv_w.shape, cconv_w.shape])
        sw_parts.append(sw_j)
        cw_parts.append(cw_j)
    sconv_full = jnp.concatenate(sw_parts, axis=2)
    cconv_full = jnp.concatenate(cw_parts, axis=2)

    eye = jnp.eye(4, dtype=F32)
    wbd_all = (pool_w[:, :, :, None, :] * eye[None, :, None, :, None]).reshape(depth, p, p)

    def row(v):
        return v.reshape(1, -1)

    saved = []
    cur, gam, bet = x[0], jnp.ones((1, d), F32), jnp.zeros((1, d), F32)
    for l in range(depth):
        w = {k: gathered(l, k) for k, _ in per_layer}
        sv = {"in1": (cur, gam, bet), "w": w}
        sv["xb1"], sv["g1"], sv["u1"], sv["h1"] = ffn_up(cur, gam, bet, w["wg1"].T, w["wu1"].T, tm=tm)
        sv["xh1"], sv["rs1"] = mm_res_ln(sv["h1"], w["wd1"], cur, gam, bet, alpha=alpha, scale=0.5, tm=tm)
        g1, b1 = row(ln1_g[l]), row(ln1_b[l])
        sv["proj"], sv["xb2"] = mm_in(sv["xh1"], g1, b1, w["win"].T, tm=tm)
        sv["wbd"] = wbd_all[l].astype(MM)
        sv["sw"], sv["cw"] = _pad_rows(sconv_full[l], 8), _pad_rows(cconv_full[l], 32)
        sv["cat"] = mixer_fwd(sv["proj"], sv["wbd"], row(pool_scale[l]), sv["sw"], sv["cw"], row(cconv_b[l]),
                              row(cnorm_g[l]), row(cnorm_b[l]), d=d, tt=tt_fwd)
        sv["xh2"], sv["rs2"] = mm_res_ln(sv["cat"], w["wout"], sv["xh1"], g1, b1, alpha=alpha, scale=1.0, tm=tm)
        g2, b2 = row(ln2_g[l]), row(ln2_b[l])
        sv["xb3"], sv["g3"], sv["u3"], sv["h3"] = ffn_up(sv["xh2"], g2, b2, w["wg2"].T, w["wu2"].T, tm=tm)
        sv["xh3"], sv["rs3"] = mm_res_ln(sv["h3"], w["wd2"], sv["xh2"], g2, b2, alpha=alpha, scale=0.5, tm=tm)
        saved.append(sv)
        cur, gam, bet = sv["xh3"], row(ln3_g[l]), row(ln3_b[l])

    dcur, lsum = loss_head(cur, gam, bet, loss_target[0], tm=tm)
    loss = lax.psum(lsum[0, 0] * (0.5 / d), MESH_AXES)

    big = [None] * depth
    small = [None] * depth
    for l in reversed(range(depth)):
        sv = saved[l]
        w = sv["w"]
        dx, dyb, dg, du, dg3, db3 = ffn_bwd(dcur, sv["xh3"], sv["rs3"], row(ln3_g[l]), w["wd2"].T, w["wg2"], w["wu2"],
                                             sv["g3"], sv["u3"], alpha=alpha, tm=tm_bwd)
        gw_g2, gw_u2 = tn_matmul(dg, sv["xb3"], tm=tm_tn), tn_matmul(du, sv["xb3"], tm=tm_tn)
        gw_d2 = tn_matmul(sv["h3"], dyb, tm=tm_tn)
        dz, dzb, dcat, dg2, db2 = lnbwd_mm(dx, sv["xh2"], sv["rs2"], row(ln2_g[l]), w["wout"].T, tm=tm)
        dproj, dwbd, dps, dsw, dcw, dcb, dcg, dcbt = mixer_bwd(
            sv["proj"], dcat, sv["wbd"], sv["wbd"].T, row(pool_scale[l]), sv["sw"], sv["cw"], row(cconv_b[l]),
            row(cnorm_g[l]), row(cnorm_b[l]), d=d, tt=tt_bwd)
        gw_out = tn_matmul(sv["cat"], dzb, tm=tm_tn)
        gw_in = tn_matmul(dproj, sv["xb2"], tm=tm_tn)
        dx = mm_add(dproj, w["win"], dz, alpha=alpha, tm=tm)
        dx, dyb, dg, du, dg1, db1 = ffn_bwd(dx, sv["xh1"], sv["rs1"], row(ln1_g[l]), w["wd1"].T, w["wg1"], w["wu1"],
                                             sv["g1"], sv["u1"], alpha=alpha, tm=tm_bwd)
        gw_g1, gw_u1 = tn_matmul(dg, sv["xb1"], tm=tm_tn), tn_matmul(du, sv["xb1"], tm=tm_tn)
        gw_d1 = tn_matmul(sv["h1"], dyb, tm=tm_tn)
        dcur = dx
        big[l] = [gw_g1, gw_u1, gw_d1, gw_in, gw_out, gw_g2, gw_u2, gw_d2]
        dpw = jnp.stack([dwbd[g * pg:(g + 1) * pg, g * pg:(g + 1) * pg] for g in range(4)])
        small[l] = [dg1, db1, dg2, db2, dg3, db3, dpw, dps, dsw[:SCONV_K], dcw[:CCONV_K], dcb, dcg, dcbt]
    grad_x = dcur[None]

    parts = jnp.concatenate([g.reshape(N_DEV, -1, d) for l in range(depth) for g in big[l]], axis=1)
    recv = exchange_big(parts.astype(MM))
    own = lax.dynamic_index_in_dim(parts, me, axis=0, keepdims=False)
    keep = (jnp.arange(N_DEV) != me).astype(F32)
    gsum = sum_parts(own, recv, keep)

    small_shapes = [g.shape for g in small[0]]
    small_flat = _pack_small([g for l in range(depth) for g in small[l]])
    small_sum = sum_gathered(gather_small(small_flat)).reshape(-1)
    small_g = _unpack_small(small_sum, small_shapes * depth)
    n_small = len(small_shapes)

    grads = {}
    off = 0
    by_key = {k: [] for k, _ in per_layer}
    for l in range(depth):
        for k, n in per_layer:
            by_key[k].append(gsum[off:off + n])
            off += n
    grads["ffn1_w_gate"] = jnp.stack([g.T for g in by_key["wg1"]])
    grads["ffn1_w_up"] = jnp.stack([g.T for g in by_key["wu1"]])
    grads["ffn1_w_down"] = jnp.stack(by_key["wd1"])
    grads["mix_w_in"] = jnp.stack([g.T for g in by_key["win"]])
    grads["mix_w_out"] = jnp.stack(by_key["wout"])
    grads["ffn2_w_gate"] = jnp.stack([g.T for g in by_key["wg2"]])
    grads["ffn2_w_up"] = jnp.stack([g.T for g in by_key["wu2"]])
    grads["ffn2_w_down"] = jnp.stack(by_key["wd2"])
    small_names = ["ln1_g", "ln1_b", "ln2_g", "ln2_b", "ln3_g", "ln3_b", "pool_w", "pool_scale", "sconv_w",
                   "cconv_w", "cconv_b", "cnorm_g", "cnorm_b"]
    for idx, name in enumerate(small_names):
        full = jnp.stack([small_g[l * n_small + idx] for l in range(depth)])
        if name in ("sconv_w", "cconv_w"):
            full = lax.dynamic_slice_in_dim(full, me * cs, cs, axis=2)
        grads[name] = full.reshape(a[name].shape)

    deltas, new_m, new_v = {}, {}, {}
    for name in WEIGHTS:
        w = a[name]
        c = w.shape[-1]
        dl, mn, vn = adamw(w.reshape(-1, c), grads[name].reshape(-1, c), a["m_" + name].reshape(-1, c),
                           a["v_" + name].reshape(-1, c))
        deltas[name], new_m[name], new_v[name] = dl.reshape(w.shape), mn.reshape(w.shape), vn.reshape(w.shape)

    return (loss, grad_x, *[grads[n] for n in WEIGHTS], *[deltas[n] for n in WEIGHTS],
            *[new_m[n] for n in WEIGHTS], *[new_v[n] for n in WEIGHTS])
```

```python
import functools

import jax
import jax.numpy as jnp
from jax import lax
from jax.experimental import pallas as pl
from jax.experimental.pallas import tpu as pltpu

F32 = jnp.float32
MM = jnp.bfloat16
LN_EPS = 1e-5
N_DEV = 8
MESH_AXES = ("x", "y", "c")
HALO = 32
CCONV_K = 31
SCONV_K = 3
CONV_ROWS = 32
VMEM_LIMIT = 56 * 1024 * 1024
ADAM_LR, ADAM_B1, ADAM_B2, ADAM_EPS, ADAM_WD, ADAM_STEP = 0.001, 0.9, 0.999, 1e-08, 0.01, 10
MESH = pl.DeviceIdType.MESH


def _call(body, *, name, out_shape, in_specs, out_specs, grid=None, scratch=(), sem=None):
    kw = {}
    if grid is not None:
        kw["grid"] = grid
    params = dict(vmem_limit_bytes=VMEM_LIMIT)
    if sem is not None:
        params["dimension_semantics"] = sem
    return pl.pallas_call(body, name=name, out_shape=out_shape, in_specs=in_specs, out_specs=out_specs,
                          scratch_shapes=list(scratch), compiler_params=pltpu.CompilerParams(**params), **kw)


def _rows(tm, n):
    return pl.BlockSpec((tm, n), lambda i: (i, 0))


def _const(shape):
    nd = len(shape)
    return pl.BlockSpec(shape, lambda *_: (0,) * nd, pipeline_mode=pl.Buffered(1))


def _acc(shape):
    nd = len(shape)
    return pl.BlockSpec(shape, lambda *_: (0,) * nd)


def _row_tile(rows, cap):
    best = None
    for t in range(8, min(rows, cap) + 1, 8):
        if rows % t == 0:
            best = t
    return best if best is not None else rows


def _sig(x):
    return 1.0 / (1.0 + jnp.exp(-x))


def _ln_stats(z):
    mu = jnp.mean(z, axis=-1, keepdims=True)
    zc = z - mu
    var = jnp.mean(zc * zc, axis=-1, keepdims=True)
    rstd = lax.rsqrt(var + LN_EPS)
    return zc * rstd, rstd


def _ln_bwd(dout, xhat, rstd, gamma):
    dxh = dout * gamma
    m1 = jnp.mean(dxh, axis=-1, keepdims=True)
    m2 = jnp.mean(dxh * xhat, axis=-1, keepdims=True)
    return rstd * (dxh - m1 - xhat * m2)


def _colsum(v):
    return jnp.sum(v, axis=0, keepdims=True)


def _f_chunks(f):
    n = 2 if f >= 2048 and f % 256 == 0 else 1
    return n, f // n


def ffn_up(xin, gam, bet, wg, wu, *, tm):
    t, d = xin.shape
    f = wg.shape[1]
    nch, fc = _f_chunks(f)

    def body(x_ref, g_ref, b_ref, wg_ref, wu_ref, xb_ref, go_ref, uo_ref, h_ref):
        xb = (x_ref[...] * g_ref[...] + b_ref[...]).astype(MM)
        xb_ref[...] = xb
        for c in range(nch):
            sl = slice(c * fc, (c + 1) * fc)
            g = jnp.dot(xb, wg_ref[:, sl], preferred_element_type=F32)
            u = jnp.dot(xb, wu_ref[:, sl], preferred_element_type=F32)
            go_ref[:, sl] = g.astype(MM)
            uo_ref[:, sl] = u.astype(MM)
            h_ref[:, sl] = (g * _sig(g) * u).astype(MM)

    return _call(
        body, name="ffn_up", grid=(t // tm,), sem=("parallel",),
        in_specs=[_rows(tm, d), _const((1, d)), _const((1, d)), _const((d, f)), _const((d, f))],
        out_specs=[_rows(tm, d), _rows(tm, f), _rows(tm, f), _rows(tm, f)],
        out_shape=[jax.ShapeDtypeStruct((t, d), MM)] + [jax.ShapeDtypeStruct((t, f), MM)] * 3,
    )(xin, gam, bet, wg, wu)


def mm_res_ln(a, w, xin, gam, bet, *, alpha, scale, tm):
    t, k = a.shape
    d = w.shape[1]

    def body(a_ref, w_ref, x_ref, g_ref, b_ref, xh_ref, rs_ref):
        y = jnp.dot(a_ref[...], w_ref[...], preferred_element_type=F32)
        x = x_ref[...] * g_ref[...] + b_ref[...]
        xh, rstd = _ln_stats(alpha * x + scale * y)
        xh_ref[...] = xh
        rs_ref[...] = rstd

    return _call(
        body, name="mm_res_ln", grid=(t // tm,), sem=("parallel",),
        in_specs=[_rows(tm, k), _const((k, d)), _rows(tm, d), _const((1, d)), _const((1, d))],
        out_specs=[_rows(tm, d), _rows(tm, 1)],
        out_shape=[jax.ShapeDtypeStruct((t, d), F32), jax.ShapeDtypeStruct((t, 1), F32)],
    )(a, w, xin, gam, bet)


def mm_in(xin, gam, bet, w, *, tm):
    t, d = xin.shape
    n = w.shape[1]

    def body(x_ref, g_ref, b_ref, w_ref, o_ref, xb_ref):
        xb = (x_ref[...] * g_ref[...] + b_ref[...]).astype(MM)
        xb_ref[...] = xb
        o_ref[...] = jnp.dot(xb, w_ref[...], preferred_element_type=F32)

    return _call(
        body, name="mm_in", grid=(t // tm,), sem=("parallel",),
        in_specs=[_rows(tm, d), _const((1, d)), _const((1, d)), _const((d, n))],
        out_specs=[_rows(tm, n), _rows(tm, d)],
        out_shape=[jax.ShapeDtypeStruct((t, n), F32), jax.ShapeDtypeStruct((t, d), MM)],
    )(xin, gam, bet, w)


def loss_head(xh, gam, bet, target, *, tm):
    t, d = xh.shape

    def body(x_ref, g_ref, b_ref, t_ref, dy_ref, l_ref):
        @pl.when(pl.program_id(0) == 0)
        def _():
            l_ref[...] = jnp.zeros_like(l_ref)

        e = x_ref[...] * g_ref[...] + b_ref[...] - t_ref[...]
        dy_ref[...] = e * (1.0 / d)
        l_ref[...] += jnp.sum(_colsum(e * e), axis=1, keepdims=True)

    return _call(
        body, name="loss_head", grid=(t // tm,), sem=("arbitrary",),
        in_specs=[_rows(tm, d), _const((1, d)), _const((1, d)), _rows(tm, d)],
        out_specs=[_rows(tm, d), _acc((1, 1))],
        out_shape=[jax.ShapeDtypeStruct((t, d), F32), jax.ShapeDtypeStruct((1, 1), F32)],
    )(xh, gam, bet, target)


def _halo_specs(tt, ncols, t):
    per, last = tt // HALO, t // HALO - 1
    return [pl.BlockSpec((HALO, ncols), lambda i: (jnp.maximum(i * per - 1, 0), 0)),
            pl.BlockSpec((tt, ncols), lambda i: (i, 0)),
            pl.BlockSpec((HALO, ncols), lambda i: (jnp.minimum((i + 1) * per, last), 0))]


def _fill_ext(ext, prev_ref, cur_ref, next_ref, i, nt, tt):
    ext[0:HALO, :] = jnp.where(i > 0, prev_ref[...], 0.0)
    ext[HALO:HALO + tt, :] = cur_ref[...]
    ext[HALO + tt:HALO + tt + HALO, :] = jnp.where(i < nt - 1, next_ref[...], 0.0)


def _conv_chunks(src, w_ref, ktaps, src0, nrows, flip=False):
    for r0 in range(0, nrows, CONV_ROWS):
        acc = None
        for k in range(ktaps):
            kk = ktaps - 1 - k if flip else k
            term = w_ref[kk:kk + 1, :] * src[src0 + r0 + k:src0 + r0 + k + CONV_ROWS, :]
            acc = term if acc is None else acc + term
        yield r0, acc


def _pool_groups(nrows, p):
    lane = lax.broadcasted_iota(jnp.int32, (nrows, p), 1)
    g = p // 4
    return lane < g, lane < 2 * g, lane < 3 * g


def _pool_select(groups, v2, v4, v8, v16):
    g0, g1, g2 = groups
    return jnp.where(g0, v2, jnp.where(g1, v4, jnp.where(g2, v8, v16)))


def _pool_count(groups, i, tt, row0, nrows, p, t):
    pos = i * tt + (row0 - HALO) + lax.broadcasted_iota(jnp.int32, (nrows, p), 0)
    half = _pool_select(groups, 1, 2, 4, 8)
    lo = jnp.clip(pos - half, 0, t)
    hi = jnp.clip(pos + half, 0, t)
    return jnp.maximum(hi - lo, 1).astype(F32)


def _pool_forward(ext, s2, s4, s8, groups, cnt, tt, p):
    e = tt + 2 * HALO
    s2[8:e - 8, :] = ext[7:e - 9, 0:p] + ext[8:e - 8, 0:p]
    s4[16:e - 16, :] = s2[15:e - 17, :] + s2[17:e - 15, :]
    s8[24:e - 24, :] = s4[22:e - 26, :] + s4[26:e - 22, :]
    s16 = s8[28:e - 36, :] + s8[36:e - 28, :]
    c = slice(HALO, HALO + tt)
    tot = _pool_select(groups, s2[c, :], s4[c, :], s8[c, :], s16)
    return tot / cnt - ext[c, 0:p]


def mixer_fwd(proj, wbd, pscale, sw, cw, cb, cg, cbt, *, d, tt):
    t, ncols = proj.shape
    p, s = d // 4, 3 * d // 8
    o_gb, o_gc, o_v, o_cv, o_cg = p, p + s, p + 2 * s, p + 3 * s, p + 4 * s
    nt, e = t // tt, tt + 2 * HALO

    def body(prev_ref, cur_ref, next_ref, wbd_ref, ps_ref, sw_ref, cw_ref, cb_ref, cg_ref, cbt_ref,
             cat_ref, ext, a_s, cv_s, s2, s4, s8):
        i = pl.program_id(0)
        _fill_ext(ext, prev_ref, cur_ref, next_ref, i, nt, tt)
        c = slice(HALO, HALO + tt)
        a_s[...] = ext[:, o_cv:o_cv + s] * _sig(ext[:, o_cg:o_cg + s])
        for r0, acc in _conv_chunks(a_s, cw_ref, CCONV_K, HALO - CCONV_K // 2, tt):
            n, _ = _ln_stats(acc + cb_ref[...])
            yn = n * cg_ref[...] + cbt_ref[...]
            cat_ref[r0:r0 + CONV_ROWS, p + s:d] = (yn * _sig(yn)).astype(MM)
        cv_s[...] = ext[:, o_gc:o_gc + s] * ext[:, o_v:o_v + s]
        for r0, acc in _conv_chunks(cv_s, sw_ref, SCONV_K, HALO - SCONV_K // 2, tt):
            gb = ext[HALO + r0:HALO + r0 + CONV_ROWS, o_gb:o_gb + s]
            cat_ref[r0:r0 + CONV_ROWS, p:p + s] = (gb * acc).astype(MM)
        groups = _pool_groups(tt, p)
        cnt = _pool_count(groups, i, tt, HALO, tt, p, t)
        pooled = _pool_forward(ext, s2, s4, s8, groups, cnt, tt, p)
        ya = jnp.dot(pooled.astype(MM), wbd_ref[...], preferred_element_type=F32) * ps_ref[...]
        cat_ref[:, 0:p] = ya.astype(MM)

    return _call(
        body, name="mixer_fwd", grid=(nt,), sem=("parallel",),
        in_specs=_halo_specs(tt, ncols, t) + [_const((p, p)), _const((1, p)), _const((8, s)), _const((32, s)),
                                               _const((1, s)), _const((1, s)), _const((1, s))],
        out_specs=_rows(tt, d),
        out_shape=jax.ShapeDtypeStruct((t, d), MM),
        scratch=[pltpu.VMEM((e, ncols), F32), pltpu.VMEM((e, s), F32), pltpu.VMEM((e, s), F32),
                 pltpu.VMEM((e, p), F32), pltpu.VMEM((e, p), F32), pltpu.VMEM((e, p), F32)],
    )(proj, proj, proj, wbd, pscale, sw, cw, cb, cg, cbt)


def mixer_bwd(proj, dcat, wbd, wbdt, pscale, sw, cw, cb, cg, cbt, *, d, tt):
    t, ncols = proj.shape
    p, s = d // 4, 3 * d // 8
    o_gb, o_gc, o_v, o_cv, o_cg = p, p + s, p + 2 * s, p + 3 * s, p + 4 * s
    nt, e = t // tt, tt + 2 * HALO
    pad = 16

    def body(pp_ref, pc_ref, pn_ref, dp_ref, dc_ref, dn_ref, wbd_ref, wbdt_ref, ps_ref, sw_ref, cw_ref,
             cb_ref, cg_ref, cbt_ref,
             dproj_ref, dwbd_ref, dps_ref, dsw_ref, dcw_ref, dcb_ref, dcg_ref, dcbt_ref,
             ext, dext, a_s, sg_s, b_s, cv_s, ds_s, q_s, r2, r4, r8):
        i = pl.program_id(0)

        @pl.when(i == 0)
        def _():
            for ref in (dwbd_ref, dps_ref, dsw_ref, dcw_ref, dcb_ref, dcg_ref, dcbt_ref):
                ref[...] = jnp.zeros_like(ref)

        _fill_ext(ext, pp_ref, pc_ref, pn_ref, i, nt, tt)
        _fill_ext(dext, dp_ref, dc_ref, dn_ref, i, nt, tt)
        c = slice(HALO, HALO + tt)

        sg_s[...] = _sig(ext[:, o_cg:o_cg + s])
        a_s[0:pad, :] = jnp.zeros((pad, s), F32)
        a_s[pad + e:pad + e + pad, :] = jnp.zeros((pad, s), F32)
        a_s[pad:pad + e, :] = ext[:, o_cv:o_cv + s] * sg_s[...]
        for r0, acc in _conv_chunks(a_s, cw_ref, CCONV_K, pad - CCONV_K // 2, e):
            b = acc + cb_ref[...]
            n, rstd = _ln_stats(b)
            yn = n * cg_ref[...] + cbt_ref[...]
            sy = _sig(yn)
            rows = slice(r0, r0 + CONV_ROWS)
            dyn = dext[rows, p + s:d] * (sy * (1.0 + yn * (1.0 - sy)))
            db = _ln_bwd(dyn, n, rstd, cg_ref[...])
            b_s[rows, :] = db
            if HALO <= r0 < HALO + tt:
                dcg_ref[...] += _colsum(dyn * n)
                dcbt_ref[...] += _colsum(dyn)
                dcb_ref[...] += _colsum(db)
        for r0, da in _conv_chunks(b_s, cw_ref, CCONV_K, HALO - CCONV_K // 2, tt, flip=True):
            rows = slice(HALO + r0, HALO + r0 + CONV_ROWS)
            sg = sg_s[rows, :]
            dproj_ref[r0:r0 + CONV_ROWS, o_cv:o_cv + s] = (da * sg).astype(MM)
            dproj_ref[r0:r0 + CONV_ROWS, o_cg:o_cg + s] = (
                da * ext[rows, o_cv:o_cv + s] * sg * (1.0 - sg)).astype(MM)
        for k in range(CCONV_K):
            lo = pad + HALO + k - CCONV_K // 2
            dcw_ref[k:k + 1, :] += _colsum(b_s[c, :] * a_s[lo:lo + tt, :])

        cv_s[...] = ext[:, o_gc:o_gc + s] * ext[:, o_v:o_v + s]
        ds_s[...] = dext[:, p:p + s] * ext[:, o_gb:o_gb + s]
        for r0, acc in _conv_chunks(cv_s, sw_ref, SCONV_K, HALO - SCONV_K // 2, tt):
            rows = slice(HALO + r0, HALO + r0 + CONV_ROWS)
            dproj_ref[r0:r0 + CONV_ROWS, o_gb:o_gb + s] = (dext[rows, p:p + s] * acc).astype(MM)
        for r0, dcv in _conv_chunks(ds_s, sw_ref, SCONV_K, HALO - SCONV_K // 2, tt, flip=True):
            rows = slice(HALO + r0, HALO + r0 + CONV_ROWS)
            dproj_ref[r0:r0 + CONV_ROWS, o_gc:o_gc + s] = (dcv * ext[rows, o_v:o_v + s]).astype(MM)
            dproj_ref[r0:r0 + CONV_ROWS, o_v:o_v + s] = (dcv * ext[rows, o_gc:o_gc + s]).astype(MM)
        for k in range(SCONV_K):
            lo = HALO + k - SCONV_K // 2
            dsw_ref[k:k + 1, :] += _colsum(ds_s[c, :] * cv_s[lo:lo + tt, :])

        groups = _pool_groups(tt, p)
        cnt = _pool_count(groups, i, tt, HALO, tt, p, t)
        pooled = _pool_forward(ext, r2, r4, r8, groups, cnt, tt, p).astype(MM)
        ta = jnp.dot(pooled, wbd_ref[...], preferred_element_type=F32)
        dps_ref[...] += _colsum(dext[c, 0:p] * ta)
        dta = (dext[:, 0:p] * ps_ref[...]).astype(MM)
        dwbd_ref[...] += lax.dot_general(pooled, dta[HALO:HALO + tt, :], (((0,), (0,)), ((), ())),
                                         preferred_element_type=F32)
        dpool = jnp.dot(dta, wbdt_ref[...], preferred_element_type=F32)
        groups_e = _pool_groups(e, p)
        q_s[...] = dpool / _pool_count(groups_e, i, tt, 0, e, p, t)
        r2[8:e - 8, :] = q_s[8:e - 8, :] + q_s[9:e - 7, :]
        r4[16:e - 16, :] = r2[15:e - 17, :] + r2[17:e - 15, :]
        r8[24:e - 24, :] = r4[22:e - 26, :] + r4[26:e - 22, :]
        r16 = r8[28:e - 36, :] + r8[36:e - 28, :]
        du = _pool_select(groups, r2[c, :], r4[c, :], r8[c, :], r16) - dpool[HALO:HALO + tt, :]
        dproj_ref[:, 0:p] = du.astype(MM)

    small = [(p, p), (1, p), (8, s), (32, s), (1, s), (1, s), (1, s)]
    return _call(
        body, name="mixer_bwd", grid=(nt,), sem=("arbitrary",),
        in_specs=_halo_specs(tt, ncols, t) + _halo_specs(tt, d, t) + [
            _const((p, p)), _const((p, p)), _const((1, p)), _const((8, s)), _const((32, s)),
            _const((1, s)), _const((1, s)), _const((1, s))],
        out_specs=[_rows(tt, ncols)] + [_acc(sh) for sh in small],
        out_shape=[jax.ShapeDtypeStruct((t, ncols), MM)] + [jax.ShapeDtypeStruct(sh, F32) for sh in small],
        scratch=[pltpu.VMEM((e, ncols), F32), pltpu.VMEM((e, d), F32), pltpu.VMEM((e + 2 * pad, s), F32)]
        + [pltpu.VMEM((e, s), F32)] * 4 + [pltpu.VMEM((e, p), F32)] * 4,
    )(proj, proj, proj, dcat, dcat, dcat, wbd, wbdt, pscale, sw, cw, cb, cg, cbt)


def ffn_bwd(dout, xh, rs, gam, wdt, wgt, wut, g, u, *, alpha, tm):
    t, d = dout.shape
    f = g.shape[1]
    nch, fc = _f_chunks(f)

    def body(do_ref, xh_ref, rs_ref, gm_ref, wdt_ref, wgt_ref, wut_ref, g_ref, u_ref,
             dx_ref, dyb_ref, dg_ref, du_ref, dgam_ref, dbet_ref):
        @pl.when(pl.program_id(0) == 0)
        def _():
            dgam_ref[...] = jnp.zeros_like(dgam_ref)
            dbet_ref[...] = jnp.zeros_like(dbet_ref)

        dout_v, xhat = do_ref[...], xh_ref[...]
        dgam_ref[...] += _colsum(dout_v * xhat)
        dbet_ref[...] += _colsum(dout_v)
        dz = _ln_bwd(dout_v, xhat, rs_ref[...], gm_ref[...])
        dyb = (0.5 * dz).astype(MM)
        dyb_ref[...] = dyb
        acc = alpha * dz
        for c in range(nch):
            sl = slice(c * fc, (c + 1) * fc)
            dh = jnp.dot(dyb, wdt_ref[:, sl], preferred_element_type=F32)
            gv, uv = g_ref[:, sl].astype(F32), u_ref[:, sl].astype(F32)
            sg = _sig(gv)
            dg = (dh * uv * (sg * (1.0 + gv * (1.0 - sg)))).astype(MM)
            du = (dh * (gv * sg)).astype(MM)
            dg_ref[:, sl] = dg
            du_ref[:, sl] = du
            acc += jnp.dot(dg, wgt_ref[sl, :], preferred_element_type=F32)
            acc += jnp.dot(du, wut_ref[sl, :], preferred_element_type=F32)
        dx_ref[...] = acc

    return _call(
        body, name="ffn_bwd", grid=(t // tm,), sem=("arbitrary",),
        in_specs=[_rows(tm, d), _rows(tm, d), _rows(tm, 1), _const((1, d)), _const((d, f)), _const((f, d)),
                  _const((f, d)), _rows(tm, f), _rows(tm, f)],
        out_specs=[_rows(tm, d), _rows(tm, d), _rows(tm, f), _rows(tm, f), _acc((1, d)), _acc((1, d))],
        out_shape=[jax.ShapeDtypeStruct((t, d), F32), jax.ShapeDtypeStruct((t, d), MM),
                   jax.ShapeDtypeStruct((t, f), MM), jax.ShapeDtypeStruct((t, f), MM),
                   jax.ShapeDtypeStruct((1, d), F32), jax.ShapeDtypeStruct((1, d), F32)],
    )(dout, xh, rs, gam, wdt, wgt, wut, g, u)


def lnbwd_mm(dout, xh, rs, gam, wt, *, tm):
    t, d = dout.shape
    n = wt.shape[1]

    def body(do_ref, xh_ref, rs_ref, gm_ref, wt_ref, dz_ref, dzb_ref, da_ref, dgam_ref, dbet_ref):
        @pl.when(pl.program_id(0) == 0)
        def _():
            dgam_ref[...] = jnp.zeros_like(dgam_ref)
            dbet_ref[...] = jnp.zeros_like(dbet_ref)

        dout_v, xhat = do_ref[...], xh_ref[...]
        dgam_ref[...] += _colsum(dout_v * xhat)
        dbet_ref[...] += _colsum(dout_v)
        dz = _ln_bwd(dout_v, xhat, rs_ref[...], gm_ref[...])
        dz_ref[...] = dz
        dzb = dz.astype(MM)
        dzb_ref[...] = dzb
        da_ref[...] = jnp.dot(dzb, wt_ref[...], preferred_element_type=F32)

    return _call(
        body, name="lnbwd_mm", grid=(t // tm,), sem=("arbitrary",),
        in_specs=[_rows(tm, d), _rows(tm, d), _rows(tm, 1), _const((1, d)), _const((d, n))],
        out_specs=[_rows(tm, d), _rows(tm, d), _rows(tm, n), _acc((1, d)), _acc((1, d))],
        out_shape=[jax.ShapeDtypeStruct((t, d), F32), jax.ShapeDtypeStruct((t, d), MM),
                   jax.ShapeDtypeStruct((t, n), F32), jax.ShapeDtypeStruct((1, d), F32),
                   jax.ShapeDtypeStruct((1, d), F32)],
    )(dout, xh, rs, gam, wt)


def mm_add(a, w, r, *, alpha, tm):
    t, k = a.shape
    d = w.shape[1]

    def body(a_ref, w_ref, r_ref, o_ref):
        o_ref[...] = jnp.dot(a_ref[...], w_ref[...], preferred_element_type=F32) + alpha * r_ref[...]

    return _call(
        body, name="mm_add", grid=(t // tm,), sem=("parallel",),
        in_specs=[_rows(tm, k), _const((k, d)), _rows(tm, d)],
        out_specs=_rows(tm, d),
        out_shape=jax.ShapeDtypeStruct((t, d), F32),
    )(a, w, r)


def tn_matmul(a, b, *, tm):
    t, n = a.shape
    d = b.shape[1]
    nch, nc = _f_chunks(n)

    def body(a_ref, b_ref, o_ref):
        @pl.when(pl.program_id(1) == 0)
        def _():
            o_ref[...] = jnp.zeros_like(o_ref)

        o_ref[...] += lax.dot_general(a_ref[...], b_ref[...], (((0,), (0,)), ((), ())),
                                      preferred_element_type=F32)

    return _call(
        body, name="tn_matmul", grid=(nch, t // tm), sem=("parallel", "arbitrary"),
        in_specs=[pl.BlockSpec((tm, nc), lambda j, i: (i, j)), pl.BlockSpec((tm, d), lambda j, i: (i, 0))],
        out_specs=pl.BlockSpec((nc, d), lambda j, i: (j, 0)),
        out_shape=jax.ShapeDtypeStruct((n, d), F32),
    )(a, b)


def sum_parts(own, recv):
    r, d = own.shape
    n = recv.shape[0]
    tr = _row_tile(r, 256)

    def body(own_ref, recv_ref, o_ref):
        acc = own_ref[...]
        for k in range(n):
            acc = acc + recv_ref[k].astype(F32)
        o_ref[...] = acc

    return _call(
        body, name="sum_parts", grid=(r // tr,), sem=("parallel",),
        in_specs=[_rows(tr, d), pl.BlockSpec((n, tr, d), lambda i: (0, i, 0))],
        out_specs=_rows(tr, d),
        out_shape=jax.ShapeDtypeStruct((r, d), F32),
    )(own, recv)


def sum_gathered(parts):
    _, r, n = parts.shape

    def body(p_ref, o_ref):
        acc = p_ref[0]
        for j in range(1, N_DEV):
            acc = acc + p_ref[j]
        o_ref[...] = acc

    return _call(
        body, name="sum_gathered",
        in_specs=[pl.BlockSpec(memory_space=pltpu.VMEM)], out_specs=pl.BlockSpec(memory_space=pltpu.VMEM),
        out_shape=jax.ShapeDtypeStruct((r, n), F32),
    )(parts)


def adamw(w, g, m, v):
    r, c = w.shape
    tr = _row_tile(r, 512)

    def body(w_ref, g_ref, m_ref, v_ref, d_ref, mo_ref, vo_ref):
        gv = g_ref[...]
        mn = ADAM_B1 * m_ref[...] + (1.0 - ADAM_B1) * gv
        vn = ADAM_B2 * v_ref[...] + (1.0 - ADAM_B2) * (gv * gv)
        m_hat = mn / (1.0 - ADAM_B1 ** ADAM_STEP)
        v_hat = vn / (1.0 - ADAM_B2 ** ADAM_STEP)
        d_ref[...] = -ADAM_LR * (m_hat / (jnp.sqrt(v_hat) + ADAM_EPS) + ADAM_WD * w_ref[...])
        mo_ref[...] = mn
        vo_ref[...] = vn

    return _call(
        body, name="adamw", grid=(r // tr,), sem=("parallel",),
        in_specs=[_rows(tr, c)] * 4, out_specs=[_rows(tr, c)] * 3,
        out_shape=[jax.ShapeDtypeStruct((r, c), F32)] * 3,
    )(w, g, m, v)


def _mesh_pos():
    return lax.axis_index("x"), lax.axis_index("y"), lax.axis_index("c")


def _two_level_gather(x_ref, out_ref, send_sems, recv_sems, local_sem):
    x, y, c = _mesh_pos()
    me, sibling = (x, y, c), (x, y, 1 - c)
    chips = [(1 - x, y), (x, 1 - y), (1 - x, 1 - y)]

    def slot(px, py, pc):
        return out_ref.at[4 * px + 2 * py + pc]

    def copy(k, block, to, src=None):
        return pltpu.make_async_remote_copy(
            src_ref=slot(*block) if src is None else src, dst_ref=slot(*block),
            send_sem=send_sems.at[k], recv_sem=recv_sems.at[k], device_id=to, device_id_type=MESH)

    mine = pltpu.make_async_copy(x_ref, slot(*me), local_sem)
    mine.start()
    first = [copy(1 + j, me, (*chip, c), src=x_ref) for j, chip in enumerate(chips)]
    first.append(copy(0, me, sibling, src=x_ref))
    for cp in first:
        cp.start()
    passed = [copy(4 + j, (*chip, c), sibling) for j, chip in enumerate(chips)]
    for j, chip in enumerate(chips):
        copy(1 + j, (*chip, c), me).wait_recv()
        passed[j].start()
    copy(0, sibling, me).wait_recv()
    for j, chip in enumerate(chips):
        copy(4 + j, (*chip, 1 - c), me).wait_recv()
    for cp in first + passed:
        cp.wait_send()
    mine.wait()


def gather_small(x):
    return _call(
        _two_level_gather_body(), name="gather_small",
        in_specs=[pl.BlockSpec(memory_space=pltpu.VMEM)], out_specs=pl.BlockSpec(memory_space=pltpu.VMEM),
        out_shape=jax.ShapeDtypeStruct((N_DEV,) + x.shape, x.dtype),
        scratch=[pltpu.SemaphoreType.DMA((7,)), pltpu.SemaphoreType.DMA((7,)), pltpu.SemaphoreType.DMA(())],
    )(x)


def _two_level_gather_body():
    def body(x_ref, out_ref, send_sems, recv_sems, local_sem):
        _two_level_gather(x_ref, out_ref, send_sems, recv_sems, local_sem)
    return body


_HBM = pl.BlockSpec(memory_space=pltpu.HBM)
_SEM = pl.BlockSpec(memory_space=pltpu.SEMAPHORE)
_EFFECT = pltpu.SideEffectType.DATAFLOW_SIDE_EFFECTING
_RELATIONS = (4, 2, 6, 5, 3, 7, 1)


def _split_copy(src_ref, land_ref, send_sems, recv_sems, k, gather):
    x, y, c = _mesh_pos()
    px, py, pc = (1 - x if k & 4 else x), (1 - y if k & 2 else y), (1 - c if k & 1 else c)
    if gather:
        src, dst = src_ref, land_ref.at[4 * x + 2 * y + c]
    else:
        src, dst = src_ref.at[4 * px + 2 * py + pc], land_ref.at[k - 1]
    return pltpu.make_async_remote_copy(src_ref=src, dst_ref=dst, send_sem=send_sems.at[k - 1],
                                        recv_sem=recv_sems.at[k - 1], device_id=(px, py, pc), device_id_type=MESH)


def copy_start(src, land_shape, *, gather, name):
    def body(src_ref, land_ref, send_sems, recv_sems, src_thru, land_thru, token):
        for k in _RELATIONS:
            _split_copy(src_ref, land_ref, send_sems, recv_sems, k, gather).start()
        token[...] = jnp.zeros_like(token)

    return pl.pallas_call(
        body, name=name,
        out_shape=(pltpu.SemaphoreType.DMA((7,)), pltpu.SemaphoreType.DMA((7,)), pltpu.HBM(src.shape, src.dtype),
                   pltpu.HBM(land_shape, src.dtype), jax.ShapeDtypeStruct((8, 128), F32)),
        in_specs=(_HBM, _HBM), out_specs=(_SEM, _SEM, _HBM, _HBM, pl.BlockSpec(memory_space=pltpu.VMEM)),
        input_output_aliases={0: 2, 1: 3},
        compiler_params=pltpu.CompilerParams(has_side_effects=_EFFECT),
    )(pltpu.with_memory_space_constraint(src, pltpu.HBM),
      pltpu.with_memory_space_constraint(lax.empty(land_shape, src.dtype), pltpu.HBM))


def copy_wait(started, after, *, gather, name):
    send_sems, recv_sems, src_thru, land_thru, _ = started

    def body(src_ref, land_ref, send_sems, recv_sems, after_ref, src_dead, got_ref):
        for k in _RELATIONS:
            cp = _split_copy(src_ref, land_ref, send_sems, recv_sems, k, gather)
            cp.wait_send()
            cp.wait_recv()

    return pl.pallas_call(
        body, name=name,
        out_shape=(pltpu.HBM(src_thru.shape, src_thru.dtype), pltpu.HBM(land_thru.shape, land_thru.dtype)),
        in_specs=(_HBM, _HBM, _SEM, _SEM, pl.BlockSpec(memory_space=pl.ANY)), out_specs=(_HBM, _HBM),
        input_output_aliases={0: 0, 1: 1},
        compiler_params=pltpu.CompilerParams(has_side_effects=_EFFECT),
    )(src_thru, land_thru, send_sems, recv_sems, after)


WEIGHTS = ['ln1_g', 'ln1_b', 'ffn1_w_gate', 'ffn1_w_up', 'ffn1_w_down', 'mix_w_in', 'pool_w', 'pool_scale',
           'sconv_w', 'cconv_w', 'cconv_b', 'cnorm_g', 'cnorm_b', 'mix_w_out', 'ln2_g', 'ln2_b',
           'ffn2_w_gate', 'ffn2_w_up', 'ffn2_w_down', 'ln3_g', 'ln3_b']


def _pack_small(pieces):
    flat = jnp.concatenate([p.reshape(-1) for p in pieces])
    pad = -flat.shape[0] % 1024
    return jnp.pad(flat, (0, pad)).reshape(-1, 128)


def _unpack_small(flat, shapes):
    out, off = [], 0
    for sh in shapes:
        n = 1
        for s in sh:
            n *= s
        out.append(flat[off:off + n].reshape(sh))
        off += n
    return out


def _pad_rows(a, rows):
    return jnp.pad(a, ((0, rows - a.shape[0]), (0, 0)))


def kernel(x, ln1_g, ln1_b, ffn1_w_gate, ffn1_w_up, ffn1_w_down, mix_w_in, pool_w, pool_scale, sconv_w, cconv_w, cconv_b, cnorm_g, cnorm_b, mix_w_out, ln2_g, ln2_b, ffn2_w_gate, ffn2_w_up, ffn2_w_down, ln3_g, ln3_b, loss_target, m_ln1_g, m_ln1_b, m_ffn1_w_gate, m_ffn1_w_up, m_ffn1_w_down, m_mix_w_in, m_pool_w, m_pool_scale, m_sconv_w, m_cconv_w, m_cconv_b, m_cnorm_g, m_cnorm_b, m_mix_w_out, m_ln2_g, m_ln2_b, m_ffn2_w_gate, m_ffn2_w_up, m_ffn2_w_down, m_ln3_g, m_ln3_b, v_ln1_g, v_ln1_b, v_ffn1_w_gate, v_ffn1_w_up, v_ffn1_w_down, v_mix_w_in, v_pool_w, v_pool_scale, v_sconv_w, v_cconv_w, v_cconv_b, v_cnorm_g, v_cnorm_b, v_mix_w_out, v_ln2_g, v_ln2_b, v_ffn2_w_gate, v_ffn2_w_up, v_ffn2_w_down, v_ln3_g, v_ln3_b):
    a = dict(locals())
    depth, d = ln1_g.shape
    t = x.shape[1]
    fs = ffn1_w_gate.shape[2]
    f = fs * N_DEV
    ins = mix_w_in.shape[2]
    ncols = ins * N_DEV
    outs = mix_w_out.shape[1]
    p, s = d // 4, 3 * d // 8
    pg = p // 4
    cs = sconv_w.shape[2]
    alpha = (2.0 * depth) ** 0.25
    me = 4 * lax.axis_index("x") + 2 * lax.axis_index("y") + lax.axis_index("c")
    tm = min(512, t)
    tm_bwd = min(256, t)
    tm_tn = min(1024, t)
    tt_fwd = min(512, t)
    tt_bwd = min(256, t)

    per_layer = [("wg1", fs), ("wu1", fs), ("wd1", fs), ("win", ins), ("wout", outs),
                 ("wg2", fs), ("wu2", fs), ("wd2", fs)]
    rows_l = sum(n for _, n in per_layer)

    def layer_rows(l):
        return [ffn1_w_gate[l].T, ffn1_w_up[l].T, ffn1_w_down[l], mix_w_in[l].T, mix_w_out[l],
                ffn2_w_gate[l].T, ffn2_w_up[l].T, ffn2_w_down[l]]

    gathers = [copy_start(jnp.concatenate([blk.astype(MM) for blk in layer_rows(l)], axis=0), (N_DEV, rows_l, d),
                          gather=True, name=f"gather_start_{l}") for l in range(depth)]
    started = sum(st[4][0:1, 0:1] for st in gathers)

    def gathered_layer(l, after):
        mine, land = copy_wait(gathers[l], after, gather=True, name=f"gather_wait_{l}")
        wall = lax.dynamic_update_slice(land, mine[None], (me, 0, 0))
        out, off = {}, 0
        for name, n in per_layer:
            out[name] = wall[:, off:off + n, :].reshape(N_DEV * n, d)
            off += n
        return out

    conv_all = gather_small(_pack_small([sconv_w, cconv_w]))
    sw_parts, cw_parts = [], []
    for j in range(N_DEV):
        sw_j, cw_j = _unpack_small(conv_all[j].reshape(-1), [sconv_w.shape, cconv_w.shape])
        sw_parts.append(sw_j)
        cw_parts.append(cw_j)
    sconv_full = jnp.concatenate(sw_parts, axis=2)
    cconv_full = jnp.concatenate(cw_parts, axis=2)

    eye = jnp.eye(4, dtype=F32)
    wbd_all = (pool_w[:, :, :, None, :] * eye[None, :, None, :, None]).reshape(depth, p, p)

    def row(v):
        return v.reshape(1, -1)

    saved = []
    cur, gam, bet = x[0], jnp.ones((1, d), F32), jnp.zeros((1, d), F32) + started
    for l in range(depth):
        w = gathered_layer(l, started if l == 0 else cur)
        sv = {"in1": (cur, gam, bet), "w": w}
        sv["xb1"], sv["g1"], sv["u1"], sv["h1"] = ffn_up(cur, gam, bet, w["wg1"].T, w["wu1"].T, tm=tm)
        sv["xh1"], sv["rs1"] = mm_res_ln(sv["h1"], w["wd1"], cur, gam, bet, alpha=alpha, scale=0.5, tm=tm)
        g1, b1 = row(ln1_g[l]), row(ln1_b[l])
        sv["proj"], sv["xb2"] = mm_in(sv["xh1"], g1, b1, w["win"].T, tm=tm)
        sv["wbd"] = wbd_all[l].astype(MM)
        sv["sw"], sv["cw"] = _pad_rows(sconv_full[l], 8), _pad_rows(cconv_full[l], 32)
        sv["cat"] = mixer_fwd(sv["proj"], sv["wbd"], row(pool_scale[l]), sv["sw"], sv["cw"], row(cconv_b[l]),
                              row(cnorm_g[l]), row(cnorm_b[l]), d=d, tt=tt_fwd)
        sv["xh2"], sv["rs2"] = mm_res_ln(sv["cat"], w["wout"], sv["xh1"], g1, b1, alpha=alpha, scale=1.0, tm=tm)
        g2, b2 = row(ln2_g[l]), row(ln2_b[l])
        sv["xb3"], sv["g3"], sv["u3"], sv["h3"] = ffn_up(sv["xh2"], g2, b2, w["wg2"].T, w["wu2"].T, tm=tm)
        sv["xh3"], sv["rs3"] = mm_res_ln(sv["h3"], w["wd2"], sv["xh2"], g2, b2, alpha=alpha, scale=0.5, tm=tm)
        saved.append(sv)
        cur, gam, bet = sv["xh3"], row(ln3_g[l]), row(ln3_b[l])

    dcur, lsum = loss_head(cur, gam, bet, loss_target[0], tm=tm)
    loss = lax.psum(lsum[0, 0] * (0.5 / d), MESH_AXES)

    exchanges = [None] * depth
    own = [None] * depth
    small = [None] * depth
    sent = None
    for l in reversed(range(depth)):
        sv = saved[l]
        w = sv["w"]
        g3 = row(ln3_g[l]) if sent is None else row(ln3_g[l]) + sent
        dx, dyb, dg, du, dg3, db3 = ffn_bwd(dcur, sv["xh3"], sv["rs3"], g3, w["wd2"].T, w["wg2"], w["wu2"],
                                             sv["g3"], sv["u3"], alpha=alpha, tm=tm_bwd)
        gw_g2, gw_u2 = tn_matmul(dg, sv["xb3"], tm=tm_tn), tn_matmul(du, sv["xb3"], tm=tm_tn)
        gw_d2 = tn_matmul(sv["h3"], dyb, tm=tm_tn)
        dz, dzb, dcat, dg2, db2 = lnbwd_mm(dx, sv["xh2"], sv["rs2"], row(ln2_g[l]), w["wout"].T, tm=tm)
        dproj, dwbd, dps, dsw, dcw, dcb, dcg, dcbt = mixer_bwd(
            sv["proj"], dcat, sv["wbd"], sv["wbd"].T, row(pool_scale[l]), sv["sw"], sv["cw"], row(cconv_b[l]),
            row(cnorm_g[l]), row(cnorm_b[l]), d=d, tt=tt_bwd)
        gw_out = tn_matmul(sv["cat"], dzb, tm=tm_tn)
        gw_in = tn_matmul(dproj, sv["xb2"], tm=tm_tn)
        dx = mm_add(dproj, w["win"], dz, alpha=alpha, tm=tm)
        dx, dyb, dg, du, dg1, db1 = ffn_bwd(dx, sv["xh1"], sv["rs1"], row(ln1_g[l]), w["wd1"].T, w["wg1"], w["wu1"],
                                             sv["g1"], sv["u1"], alpha=alpha, tm=tm_bwd)
        gw_g1, gw_u1 = tn_matmul(dg, sv["xb1"], tm=tm_tn), tn_matmul(du, sv["xb1"], tm=tm_tn)
        gw_d1 = tn_matmul(sv["h1"], dyb, tm=tm_tn)
        dcur = dx
        big = [gw_g1, gw_u1, gw_d1, gw_in, gw_out, gw_g2, gw_u2, gw_d2]
        parts = jnp.concatenate([g.astype(MM).reshape(N_DEV, n, d) for g, (_, n) in zip(big, per_layer)], axis=1)
        own[l] = jnp.concatenate([lax.dynamic_slice_in_dim(g, me * n, n, axis=0)
                                  for g, (_, n) in zip(big, per_layer)], axis=0)
        exchanges[l] = copy_start(parts, (N_DEV - 1, rows_l, d), gather=False, name=f"exchange_start_{l}")
        sent = exchanges[l][4][0:1, 0:1]
        dpw = jnp.stack([dwbd[g * pg:(g + 1) * pg, g * pg:(g + 1) * pg] for g in range(4)])
        small[l] = [dg1, db1, dg2, db2, dg3, db3, dpw, dps, dsw[:SCONV_K], dcw[:CCONV_K], dcb, dcg, dcbt]
    grad_x = dcur[None]

    gsum = [sum_parts(own[l], copy_wait(exchanges[l], dcur, gather=False, name=f"exchange_wait_{l}")[1])
            for l in reversed(range(depth))][::-1]

    small_shapes = [g.shape for g in small[0]]
    small_flat = _pack_small([g for l in range(depth) for g in small[l]])
    small_sum = sum_gathered(gather_small(small_flat)).reshape(-1)
    small_g = _unpack_small(small_sum, small_shapes * depth)
    n_small = len(small_shapes)

    grads = {}
    by_key = {k: [] for k, _ in per_layer}
    for l in range(depth):
        off = 0
        for k, n in per_layer:
            by_key[k].append(gsum[l][off:off + n])
            off += n
    grads["ffn1_w_gate"] = jnp.stack([g.T for g in by_key["wg1"]])
    grads["ffn1_w_up"] = jnp.stack([g.T for g in by_key["wu1"]])
    grads["ffn1_w_down"] = jnp.stack(by_key["wd1"])
    grads["mix_w_in"] = jnp.stack([g.T for g in by_key["win"]])
    grads["mix_w_out"] = jnp.stack(by_key["wout"])
    grads["ffn2_w_gate"] = jnp.stack([g.T for g in by_key["wg2"]])
    grads["ffn2_w_up"] = jnp.stack([g.T for g in by_key["wu2"]])
    grads["ffn2_w_down"] = jnp.stack(by_key["wd2"])
    small_names = ["ln1_g", "ln1_b", "ln2_g", "ln2_b", "ln3_g", "ln3_b", "pool_w", "pool_scale", "sconv_w",
                   "cconv_w", "cconv_b", "cnorm_g", "cnorm_b"]
    for idx, name in enumerate(small_names):
        full = jnp.stack([small_g[l * n_small + idx] for l in range(depth)])
        if name in ("sconv_w", "cconv_w"):
            full = lax.dynamic_slice_in_dim(full, me * cs, cs, axis=2)
        grads[name] = full.reshape(a[name].shape)

    deltas, new_m, new_v = {}, {}, {}
    for name in WEIGHTS:
        w = a[name]
        c = w.shape[-1]
        dl, mn, vn = adamw(w.reshape(-1, c), grads[name].reshape(-1, c), a["m_" + name].reshape(-1, c),
                           a["v_" + name].reshape(-1, c))
        deltas[name], new_m[name], new_v[name] = dl.reshape(w.shape), mn.reshape(w.shape), vn.reshape(w.shape)

    return (loss, grad_x, *[grads[n] for n in WEIGHTS], *[deltas[n] for n in WEIGHTS],
            *[new_m[n] for n in WEIGHTS], *[new_v[n] for n in WEIGHTS])
```

```python
import functools

import jax
import jax.numpy as jnp
from jax import lax
from jax.experimental import pallas as pl
from jax.experimental.pallas import tpu as pltpu

F32 = jnp.float32
MM = jnp.bfloat16
LN_EPS = 1e-5
N_DEV = 8
MESH_AXES = ("x", "y", "c")
HALO = 32
CCONV_K = 31
SCONV_K = 3
CONV_ROWS = 32
VMEM_LIMIT = 56 * 1024 * 1024
ADAM_LR, ADAM_B1, ADAM_B2, ADAM_EPS, ADAM_WD, ADAM_STEP = 0.001, 0.9, 0.999, 1e-08, 0.01, 10
MESH = pl.DeviceIdType.MESH


def _call(body, *, name, out_shape, in_specs, out_specs, grid=None, scratch=(), sem=None):
    kw = {}
    if grid is not None:
        kw["grid"] = grid
    params = dict(vmem_limit_bytes=VMEM_LIMIT)
    if sem is not None:
        params["dimension_semantics"] = sem
    return pl.pallas_call(body, name=name, out_shape=out_shape, in_specs=in_specs, out_specs=out_specs,
                          scratch_shapes=list(scratch), compiler_params=pltpu.CompilerParams(**params), **kw)


def _rows(tm, n):
    return pl.BlockSpec((tm, n), lambda i: (i, 0))


def _const(shape):
    nd = len(shape)
    return pl.BlockSpec(shape, lambda *_: (0,) * nd, pipeline_mode=pl.Buffered(1))


def _acc(shape):
    nd = len(shape)
    return pl.BlockSpec(shape, lambda *_: (0,) * nd)


def _row_tile(rows, cap):
    best = None
    for t in range(8, min(rows, cap) + 1, 8):
        if rows % t == 0:
            best = t
    return best if best is not None else rows


def _sig(x):
    return 1.0 / (1.0 + jnp.exp(-x))


def _ln_stats(z):
    mu = jnp.mean(z, axis=-1, keepdims=True)
    zc = z - mu
    var = jnp.mean(zc * zc, axis=-1, keepdims=True)
    rstd = lax.rsqrt(var + LN_EPS)
    return zc * rstd, rstd


def _ln_bwd(dout, xhat, rstd, gamma):
    dxh = dout * gamma
    m1 = jnp.mean(dxh, axis=-1, keepdims=True)
    m2 = jnp.mean(dxh * xhat, axis=-1, keepdims=True)
    return rstd * (dxh - m1 - xhat * m2)


def _colsum(v):
    return jnp.sum(v, axis=0, keepdims=True)


def _f_chunks(f):
    n = 2 if f >= 2048 and f % 256 == 0 else 1
    return n, f // n


def ffn_up(xin, gam, bet, wg, wu, *, tm):
    t, d = xin.shape
    f = wg.shape[1]
    nch, fc = _f_chunks(f)

    def body(x_ref, g_ref, b_ref, wg_ref, wu_ref, xb_ref, go_ref, uo_ref, h_ref):
        xb = (x_ref[...] * g_ref[...] + b_ref[...]).astype(MM)
        xb_ref[...] = xb
        for c in range(nch):
            sl = slice(c * fc, (c + 1) * fc)
            g = jnp.dot(xb, wg_ref[:, sl], preferred_element_type=F32)
            u = jnp.dot(xb, wu_ref[:, sl], preferred_element_type=F32)
            go_ref[:, sl] = g.astype(MM)
            uo_ref[:, sl] = u.astype(MM)
            h_ref[:, sl] = (g * _sig(g) * u).astype(MM)

    return _call(
        body, name="ffn_up", grid=(t // tm,), sem=("parallel",),
        in_specs=[_rows(tm, d), _const((1, d)), _const((1, d)), _const((d, f)), _const((d, f))],
        out_specs=[_rows(tm, d), _rows(tm, f), _rows(tm, f), _rows(tm, f)],
        out_shape=[jax.ShapeDtypeStruct((t, d), MM)] + [jax.ShapeDtypeStruct((t, f), MM)] * 3,
    )(xin, gam, bet, wg, wu)


def mm_res_ln(a, w, xin, gam, bet, *, alpha, scale, tm):
    t, k = a.shape
    d = w.shape[1]

    def body(a_ref, w_ref, x_ref, g_ref, b_ref, xh_ref, rs_ref):
        y = jnp.dot(a_ref[...], w_ref[...], preferred_element_type=F32)
        x = x_ref[...] * g_ref[...] + b_ref[...]
        xh, rstd = _ln_stats(alpha * x + scale * y)
        xh_ref[...] = xh
        rs_ref[...] = rstd

    return _call(
        body, name="mm_res_ln", grid=(t // tm,), sem=("parallel",),
        in_specs=[_rows(tm, k), _const((k, d)), _rows(tm, d), _const((1, d)), _const((1, d))],
        out_specs=[_rows(tm, d), _rows(tm, 1)],
        out_shape=[jax.ShapeDtypeStruct((t, d), F32), jax.ShapeDtypeStruct((t, 1), F32)],
    )(a, w, xin, gam, bet)


def mm_in(xin, gam, bet, w, *, tm):
    t, d = xin.shape
    n = w.shape[1]

    def body(x_ref, g_ref, b_ref, w_ref, o_ref, xb_ref):
        xb = (x_ref[...] * g_ref[...] + b_ref[...]).astype(MM)
        xb_ref[...] = xb
        o_ref[...] = jnp.dot(xb, w_ref[...], preferred_element_type=F32)

    return _call(
        body, name="mm_in", grid=(t // tm,), sem=("parallel",),
        in_specs=[_rows(tm, d), _const((1, d)), _const((1, d)), _const((d, n))],
        out_specs=[_rows(tm, n), _rows(tm, d)],
        out_shape=[jax.ShapeDtypeStruct((t, n), F32), jax.ShapeDtypeStruct((t, d), MM)],
    )(xin, gam, bet, w)


def loss_head(xh, gam, bet, target, *, tm):
    t, d = xh.shape

    def body(x_ref, g_ref, b_ref, t_ref, dy_ref, l_ref):
        @pl.when(pl.program_id(0) == 0)
        def _():
            l_ref[...] = jnp.zeros_like(l_ref)

        e = x_ref[...] * g_ref[...] + b_ref[...] - t_ref[...]
        dy_ref[...] = e * (1.0 / d)
        l_ref[...] += jnp.sum(_colsum(e * e), axis=1, keepdims=True)

    return _call(
        body, name="loss_head", grid=(t // tm,), sem=("arbitrary",),
        in_specs=[_rows(tm, d), _const((1, d)), _const((1, d)), _rows(tm, d)],
        out_specs=[_rows(tm, d), _acc((1, 1))],
        out_shape=[jax.ShapeDtypeStruct((t, d), F32), jax.ShapeDtypeStruct((1, 1), F32)],
    )(xh, gam, bet, target)


def _halo_specs(tt, ncols, t):
    per, last = tt // HALO, t // HALO - 1
    return [pl.BlockSpec((HALO, ncols), lambda i: (jnp.maximum(i * per - 1, 0), 0)),
            pl.BlockSpec((tt, ncols), lambda i: (i, 0)),
            pl.BlockSpec((HALO, ncols), lambda i: (jnp.minimum((i + 1) * per, last), 0))]


def _fill_ext(ext, prev_ref, cur_ref, next_ref, i, nt, tt):
    ext[0:HALO, :] = jnp.where(i > 0, prev_ref[...], 0.0)
    ext[HALO:HALO + tt, :] = cur_ref[...]
    ext[HALO + tt:HALO + tt + HALO, :] = jnp.where(i < nt - 1, next_ref[...], 0.0)


def _make_shifts(sh, n):
    for r in range(1, 8):
        sh[r, 0:n, :] = sh[0, r:r + n, :]


def _shift_reader(sh):
    def read(o, rows):
        r = o % 8
        return sh[r, o - r:o - r + rows, :]
    return read


def _ref_reader(ref):
    def read(o, rows):
        return ref[o:o + rows, :]
    return read


def _conv_chunks(read, w_ref, ktaps, src0, nrows, flip=False):
    for r0 in range(0, nrows, CONV_ROWS):
        acc = None
        for k in range(ktaps):
            kk = ktaps - 1 - k if flip else k
            term = w_ref[kk:kk + 1, :] * read(src0 + r0 + k, CONV_ROWS)
            acc = term if acc is None else acc + term
        yield r0, acc


def _pool_groups(nrows, p):
    lane = lax.broadcasted_iota(jnp.int32, (nrows, p), 1)
    g = p // 4
    return lane < g, lane < 2 * g, lane < 3 * g


def _pool_select(groups, v2, v4, v8, v16):
    g0, g1, g2 = groups
    return jnp.where(g0, v2, jnp.where(g1, v4, jnp.where(g2, v8, v16)))


def _pool_count(groups, i, tt, row0, nrows, p, t):
    pos = i * tt + (row0 - HALO) + lax.broadcasted_iota(jnp.int32, (nrows, p), 0)
    half = _pool_select(groups, 1, 2, 4, 8)
    lo = jnp.clip(pos - half, 0, t)
    hi = jnp.clip(pos + half, 0, t)
    return jnp.maximum(hi - lo, 1).astype(F32)


def _pool_forward(ext, s2, s4, s8, groups, cnt, tt, p):
    e = tt + 2 * HALO
    s2[8:e - 8, :] = ext[7:e - 9, 0:p] + ext[8:e - 8, 0:p]
    s4[16:e - 16, :] = s2[15:e - 17, :] + s2[17:e - 15, :]
    s8[24:e - 24, :] = s4[22:e - 26, :] + s4[26:e - 22, :]
    s16 = s8[28:e - 36, :] + s8[36:e - 28, :]
    c = slice(HALO, HALO + tt)
    tot = _pool_select(groups, s2[c, :], s4[c, :], s8[c, :], s16)
    return tot / cnt - ext[c, 0:p]


def mixer_fwd(proj, wbd, pscale, sw, cw, cb, cg, cbt, *, d, tt):
    t, ncols = proj.shape
    p, s = d // 4, 3 * d // 8
    o_gb, o_gc, o_v, o_cv, o_cg = p, p + s, p + 2 * s, p + 3 * s, p + 4 * s
    nt, e = t // tt, tt + 2 * HALO

    def body(prev_ref, cur_ref, next_ref, wbd_ref, ps_ref, sw_ref, cw_ref, cb_ref, cg_ref, cbt_ref,
             cat_ref, ext, a_sh, cv_s, s2, s4, s8):
        i = pl.program_id(0)
        _fill_ext(ext, prev_ref, cur_ref, next_ref, i, nt, tt)
        a_sh[0, 0:e, :] = ext[:, o_cv:o_cv + s] * _sig(ext[:, o_cg:o_cg + s])
        a_sh[0, e:e + 8, :] = jnp.zeros((8, s), F32)
        _make_shifts(a_sh, e)
        for r0, acc in _conv_chunks(_shift_reader(a_sh), cw_ref, CCONV_K, HALO - CCONV_K // 2, tt):
            n, _ = _ln_stats(acc + cb_ref[...])
            yn = n * cg_ref[...] + cbt_ref[...]
            cat_ref[r0:r0 + CONV_ROWS, p + s:d] = (yn * _sig(yn)).astype(MM)
        cv_s[...] = ext[:, o_gc:o_gc + s] * ext[:, o_v:o_v + s]
        for r0, acc in _conv_chunks(_ref_reader(cv_s), sw_ref, SCONV_K, HALO - SCONV_K // 2, tt):
            gb = ext[HALO + r0:HALO + r0 + CONV_ROWS, o_gb:o_gb + s]
            cat_ref[r0:r0 + CONV_ROWS, p:p + s] = (gb * acc).astype(MM)
        groups = _pool_groups(tt, p)
        cnt = _pool_count(groups, i, tt, HALO, tt, p, t)
        pooled = _pool_forward(ext, s2, s4, s8, groups, cnt, tt, p)
        ya = jnp.dot(pooled.astype(MM), wbd_ref[...], preferred_element_type=F32) * ps_ref[...]
        cat_ref[:, 0:p] = ya.astype(MM)

    return _call(
        body, name="mixer_fwd", grid=(nt,), sem=("parallel",),
        in_specs=_halo_specs(tt, ncols, t) + [_const((p, p)), _const((1, p)), _const((8, s)), _const((32, s)),
                                               _const((1, s)), _const((1, s)), _const((1, s))],
        out_specs=_rows(tt, d),
        out_shape=jax.ShapeDtypeStruct((t, d), MM),
        scratch=[pltpu.VMEM((e, ncols), F32), pltpu.VMEM((8, e + 8, s), F32), pltpu.VMEM((e, s), F32),
                 pltpu.VMEM((e, p), F32), pltpu.VMEM((e, p), F32), pltpu.VMEM((e, p), F32)],
    )(proj, proj, proj, wbd, pscale, sw, cw, cb, cg, cbt)


def mixer_bwd(proj, dcat, wbd, wbdt, pscale, sw, cw, cb, cg, cbt, *, d, tt):
    t, ncols = proj.shape
    p, s = d // 4, 3 * d // 8
    o_gb, o_gc, o_v, o_cv, o_cg = p, p + s, p + 2 * s, p + 3 * s, p + 4 * s
    nt, e = t // tt, tt + 2 * HALO
    pad = 16

    def body(pp_ref, pc_ref, pn_ref, dp_ref, dc_ref, dn_ref, wbd_ref, wbdt_ref, ps_ref, sw_ref, cw_ref,
             cb_ref, cg_ref, cbt_ref,
             dproj_ref, dwbd_ref, dps_ref, dsw_ref, dcw_ref, dcb_ref, dcg_ref, dcbt_ref,
             ext, dext, a_sh, b_sh, sg_s, cv_s, ds_s, q_s, r2, r4, r8):
        i = pl.program_id(0)

        @pl.when(i == 0)
        def _():
            for ref in (dwbd_ref, dps_ref, dsw_ref, dcw_ref, dcb_ref, dcg_ref, dcbt_ref):
                ref[...] = jnp.zeros_like(ref)

        _fill_ext(ext, pp_ref, pc_ref, pn_ref, i, nt, tt)
        _fill_ext(dext, dp_ref, dc_ref, dn_ref, i, nt, tt)
        c = slice(HALO, HALO + tt)

        sg_s[...] = _sig(ext[:, o_cg:o_cg + s])
        a_sh[0, 0:pad, :] = jnp.zeros((pad, s), F32)
        a_sh[0, pad + e:pad + e + pad + 8, :] = jnp.zeros((pad + 8, s), F32)
        a_sh[0, pad:pad + e, :] = ext[:, o_cv:o_cv + s] * sg_s[...]
        _make_shifts(a_sh, e + 2 * pad)
        read_a = _shift_reader(a_sh)
        for r0, acc in _conv_chunks(read_a, cw_ref, CCONV_K, pad - CCONV_K // 2, e):
            b = acc + cb_ref[...]
            n, rstd = _ln_stats(b)
            yn = n * cg_ref[...] + cbt_ref[...]
            sy = _sig(yn)
            rows = slice(r0, r0 + CONV_ROWS)
            dyn = dext[rows, p + s:d] * (sy * (1.0 + yn * (1.0 - sy)))
            db = _ln_bwd(dyn, n, rstd, cg_ref[...])
            b_sh[0, rows, :] = db
            if HALO <= r0 < HALO + tt:
                dcg_ref[...] += _colsum(dyn * n)
                dcbt_ref[...] += _colsum(dyn)
                dcb_ref[...] += _colsum(db)
        b_sh[0, e:e + 8, :] = jnp.zeros((8, s), F32)
        _make_shifts(b_sh, e)
        for r0, da in _conv_chunks(_shift_reader(b_sh), cw_ref, CCONV_K, HALO - CCONV_K // 2, tt, flip=True):
            rows = slice(HALO + r0, HALO + r0 + CONV_ROWS)
            sg = sg_s[rows, :]
            dproj_ref[r0:r0 + CONV_ROWS, o_cv:o_cv + s] = (da * sg).astype(MM)
            dproj_ref[r0:r0 + CONV_ROWS, o_cg:o_cg + s] = (
                da * ext[rows, o_cv:o_cv + s] * sg * (1.0 - sg)).astype(MM)
        for k in range(CCONV_K):
            lo = pad + HALO + k - CCONV_K // 2
            dcw_ref[k:k + 1, :] += _colsum(b_sh[0, c, :] * read_a(lo, tt))

        cv_s[...] = ext[:, o_gc:o_gc + s] * ext[:, o_v:o_v + s]
        ds_s[...] = dext[:, p:p + s] * ext[:, o_gb:o_gb + s]
        for r0, acc in _conv_chunks(_ref_reader(cv_s), sw_ref, SCONV_K, HALO - SCONV_K // 2, tt):
            rows = slice(HALO + r0, HALO + r0 + CONV_ROWS)
            dproj_ref[r0:r0 + CONV_ROWS, o_gb:o_gb + s] = (dext[rows, p:p + s] * acc).astype(MM)
        for r0, dcv in _conv_chunks(_ref_reader(ds_s), sw_ref, SCONV_K, HALO - SCONV_K // 2, tt, flip=True):
            rows = slice(HALO + r0, HALO + r0 + CONV_ROWS)
            dproj_ref[r0:r0 + CONV_ROWS, o_gc:o_gc + s] = (dcv * ext[rows, o_v:o_v + s]).astype(MM)
            dproj_ref[r0:r0 + CONV_ROWS, o_v:o_v + s] = (dcv * ext[rows, o_gc:o_gc + s]).astype(MM)
        for k in range(SCONV_K):
            lo = HALO + k - SCONV_K // 2
            dsw_ref[k:k + 1, :] += _colsum(ds_s[c, :] * cv_s[lo:lo + tt, :])

        groups = _pool_groups(tt, p)
        cnt = _pool_count(groups, i, tt, HALO, tt, p, t)
        pooled = _pool_forward(ext, r2, r4, r8, groups, cnt, tt, p).astype(MM)
        ta = jnp.dot(pooled, wbd_ref[...], preferred_element_type=F32)
        dps_ref[...] += _colsum(dext[c, 0:p] * ta)
        dta = (dext[:, 0:p] * ps_ref[...]).astype(MM)
        dwbd_ref[...] += lax.dot_general(pooled, dta[HALO:HALO + tt, :], (((0,), (0,)), ((), ())),
                                         preferred_element_type=F32)
        dpool = jnp.dot(dta, wbdt_ref[...], preferred_element_type=F32)
        groups_e = _pool_groups(e, p)
        q_s[...] = dpool / _pool_count(groups_e, i, tt, 0, e, p, t)
        r2[8:e - 8, :] = q_s[8:e - 8, :] + q_s[9:e - 7, :]
        r4[16:e - 16, :] = r2[15:e - 17, :] + r2[17:e - 15, :]
        r8[24:e - 24, :] = r4[22:e - 26, :] + r4[26:e - 22, :]
        r16 = r8[28:e - 36, :] + r8[36:e - 28, :]
        du = _pool_select(groups, r2[c, :], r4[c, :], r8[c, :], r16) - dpool[HALO:HALO + tt, :]
        dproj_ref[:, 0:p] = du.astype(MM)

    small = [(p, p), (1, p), (8, s), (32, s), (1, s), (1, s), (1, s)]
    return _call(
        body, name="mixer_bwd", grid=(nt,), sem=("arbitrary",),
        in_specs=_halo_specs(tt, ncols, t) + _halo_specs(tt, d, t) + [
            _const((p, p)), _const((p, p)), _const((1, p)), _const((8, s)), _const((32, s)),
            _const((1, s)), _const((1, s)), _const((1, s))],
        out_specs=[_rows(tt, ncols)] + [_acc(sh) for sh in small],
        out_shape=[jax.ShapeDtypeStruct((t, ncols), MM)] + [jax.ShapeDtypeStruct(sh, F32) for sh in small],
        scratch=[pltpu.VMEM((e, ncols), F32), pltpu.VMEM((e, d), F32), pltpu.VMEM((8, e + 2 * pad + 8, s), F32),
                 pltpu.VMEM((8, e + 8, s), F32)] + [pltpu.VMEM((e, s), F32)] * 3 + [pltpu.VMEM((e, p), F32)] * 4,
    )(proj, proj, proj, dcat, dcat, dcat, wbd, wbdt, pscale, sw, cw, cb, cg, cbt)


def ffn_bwd(dout, xh, rs, gam, wdt, wgt, wut, g, u, *, alpha, tm):
    t, d = dout.shape
    f = g.shape[1]
    nch, fc = _f_chunks(f)

    def body(do_ref, xh_ref, rs_ref, gm_ref, wdt_ref, wgt_ref, wut_ref, g_ref, u_ref,
             dx_ref, dyb_ref, dg_ref, du_ref, dgam_ref, dbet_ref):
        @pl.when(pl.program_id(0) == 0)
        def _():
            dgam_ref[...] = jnp.zeros_like(dgam_ref)
            dbet_ref[...] = jnp.zeros_like(dbet_ref)

        dout_v, xhat = do_ref[...], xh_ref[...]
        dgam_ref[...] += _colsum(dout_v * xhat)
        dbet_ref[...] += _colsum(dout_v)
        dz = _ln_bwd(dout_v, xhat, rs_ref[...], gm_ref[...])
        dyb = (0.5 * dz).astype(MM)
        dyb_ref[...] = dyb
        acc = alpha * dz
        for c in range(nch):
            sl = slice(c * fc, (c + 1) * fc)
            dh = jnp.dot(dyb, wdt_ref[:, sl], preferred_element_type=F32)
            gv, uv = g_ref[:, sl].astype(F32), u_ref[:, sl].astype(F32)
            sg = _sig(gv)
            dg = (dh * uv * (sg * (1.0 + gv * (1.0 - sg)))).astype(MM)
            du = (dh * (gv * sg)).astype(MM)
            dg_ref[:, sl] = dg
            du_ref[:, sl] = du
            acc += jnp.dot(dg, wgt_ref[sl, :], preferred_element_type=F32)
            acc += jnp.dot(du, wut_ref[sl, :], preferred_element_type=F32)
        dx_ref[...] = acc

    return _call(
        body, name="ffn_bwd", grid=(t // tm,), sem=("arbitrary",),
        in_specs=[_rows(tm, d), _rows(tm, d), _rows(tm, 1), _const((1, d)), _const((d, f)), _const((f, d)),
                  _const((f, d)), _rows(tm, f), _rows(tm, f)],
        out_specs=[_rows(tm, d), _rows(tm, d), _rows(tm, f), _rows(tm, f), _acc((1, d)), _acc((1, d))],
        out_shape=[jax.ShapeDtypeStruct((t, d), F32), jax.ShapeDtypeStruct((t, d), MM),
                   jax.ShapeDtypeStruct((t, f), MM), jax.ShapeDtypeStruct((t, f), MM),
                   jax.ShapeDtypeStruct((1, d), F32), jax.ShapeDtypeStruct((1, d), F32)],
    )(dout, xh, rs, gam, wdt, wgt, wut, g, u)


def lnbwd_mm(dout, xh, rs, gam, wt, *, tm):
    t, d = dout.shape
    n = wt.shape[1]

    def body(do_ref, xh_ref, rs_ref, gm_ref, wt_ref, dz_ref, dzb_ref, da_ref, dgam_ref, dbet_ref):
        @pl.when(pl.program_id(0) == 0)
        def _():
            dgam_ref[...] = jnp.zeros_like(dgam_ref)
            dbet_ref[...] = jnp.zeros_like(dbet_ref)

        dout_v, xhat = do_ref[...], xh_ref[...]
        dgam_ref[...] += _colsum(dout_v * xhat)
        dbet_ref[...] += _colsum(dout_v)
        dz = _ln_bwd(dout_v, xhat, rs_ref[...], gm_ref[...])
        dz_ref[...] = dz
        dzb = dz.astype(MM)
        dzb_ref[...] = dzb
        da_ref[...] = jnp.dot(dzb, wt_ref[...], preferred_element_type=F32)

    return _call(
        body, name="lnbwd_mm", grid=(t // tm,), sem=("arbitrary",),
        in_specs=[_rows(tm, d), _rows(tm, d), _rows(tm, 1), _const((1, d)), _const((d, n))],
        out_specs=[_rows(tm, d), _rows(tm, d), _rows(tm, n), _acc((1, d)), _acc((1, d))],
        out_shape=[jax.ShapeDtypeStruct((t, d), F32), jax.ShapeDtypeStruct((t, d), MM),
                   jax.ShapeDtypeStruct((t, n), F32), jax.ShapeDtypeStruct((1, d), F32),
                   jax.ShapeDtypeStruct((1, d), F32)],
    )(dout, xh, rs, gam, wt)


def mm_add(a, w, r, *, alpha, tm):
    t, k = a.shape
    d = w.shape[1]

    def body(a_ref, w_ref, r_ref, o_ref):
        o_ref[...] = jnp.dot(a_ref[...], w_ref[...], preferred_element_type=F32) + alpha * r_ref[...]

    return _call(
        body, name="mm_add", grid=(t // tm,), sem=("parallel",),
        in_specs=[_rows(tm, k), _const((k, d)), _rows(tm, d)],
        out_specs=_rows(tm, d),
        out_shape=jax.ShapeDtypeStruct((t, d), F32),
    )(a, w, r)


def tn_matmul(a, b, *, tm, after=None):
    t, n = a.shape
    d = b.shape[1]
    nch, nc = _f_chunks(n)

    def body(a_ref, b_ref, *rest):
        o_ref = rest[-1]

        @pl.when(pl.program_id(1) == 0)
        def _():
            o_ref[...] = jnp.zeros_like(o_ref)

        o_ref[...] += lax.dot_general(a_ref[...], b_ref[...], (((0,), (0,)), ((), ())),
                                      preferred_element_type=F32)

    extra = [] if after is None else [after]
    return _call(
        body, name="tn_matmul", grid=(nch, t // tm), sem=("parallel", "arbitrary"),
        in_specs=[pl.BlockSpec((tm, nc), lambda j, i: (i, j)), pl.BlockSpec((tm, d), lambda j, i: (i, 0))]
        + [pl.BlockSpec(memory_space=pl.ANY)] * len(extra),
        out_specs=pl.BlockSpec((nc, d), lambda j, i: (j, 0)),
        out_shape=jax.ShapeDtypeStruct((n, d), F32),
    )(a, b, *extra)


def sum_parts(own, recv):
    r, d = own.shape
    n = recv.shape[0]
    tr = _row_tile(r, 256)

    def body(own_ref, recv_ref, o_ref):
        acc = own_ref[...]
        for k in range(n):
            acc = acc + recv_ref[k].astype(F32)
        o_ref[...] = acc

    return _call(
        body, name="sum_parts", grid=(r // tr,), sem=("parallel",),
        in_specs=[_rows(tr, d), pl.BlockSpec((n, tr, d), lambda i: (0, i, 0))],
        out_specs=_rows(tr, d),
        out_shape=jax.ShapeDtypeStruct((r, d), F32),
    )(own, recv)


def sum_gathered(parts):
    _, r, n = parts.shape

    def body(p_ref, o_ref):
        acc = p_ref[0]
        for j in range(1, N_DEV):
            acc = acc + p_ref[j]
        o_ref[...] = acc

    return _call(
        body, name="sum_gathered",
        in_specs=[pl.BlockSpec(memory_space=pltpu.VMEM)], out_specs=pl.BlockSpec(memory_space=pltpu.VMEM),
        out_shape=jax.ShapeDtypeStruct((r, n), F32),
    )(parts)


def adamw(w, g, m, v):
    r, c = w.shape
    tr = _row_tile(r, 512)

    def body(w_ref, g_ref, m_ref, v_ref, d_ref, mo_ref, vo_ref):
        gv = g_ref[...]
        mn = ADAM_B1 * m_ref[...] + (1.0 - ADAM_B1) * gv
        vn = ADAM_B2 * v_ref[...] + (1.0 - ADAM_B2) * (gv * gv)
        m_hat = mn / (1.0 - ADAM_B1 ** ADAM_STEP)
        v_hat = vn / (1.0 - ADAM_B2 ** ADAM_STEP)
        d_ref[...] = -ADAM_LR * (m_hat / (jnp.sqrt(v_hat) + ADAM_EPS) + ADAM_WD * w_ref[...])
        mo_ref[...] = mn
        vo_ref[...] = vn

    return _call(
        body, name="adamw", grid=(r // tr,), sem=("parallel",),
        in_specs=[_rows(tr, c)] * 4, out_specs=[_rows(tr, c)] * 3,
        out_shape=[jax.ShapeDtypeStruct((r, c), F32)] * 3,
    )(w, g, m, v)


def _mesh_pos():
    return lax.axis_index("x"), lax.axis_index("y"), lax.axis_index("c")


def _two_level_gather(x_ref, out_ref, send_sems, recv_sems, local_sem):
    x, y, c = _mesh_pos()
    me, sibling = (x, y, c), (x, y, 1 - c)
    chips = [(1 - x, y), (x, 1 - y), (1 - x, 1 - y)]

    def slot(px, py, pc):
        return out_ref.at[4 * px + 2 * py + pc]

    def copy(k, block, to, src=None):
        return pltpu.make_async_remote_copy(
            src_ref=slot(*block) if src is None else src, dst_ref=slot(*block),
            send_sem=send_sems.at[k], recv_sem=recv_sems.at[k], device_id=to, device_id_type=MESH)

    mine = pltpu.make_async_copy(x_ref, slot(*me), local_sem)
    mine.start()
    first = [copy(1 + j, me, (*chip, c), src=x_ref) for j, chip in enumerate(chips)]
    first.append(copy(0, me, sibling, src=x_ref))
    for cp in first:
        cp.start()
    passed = [copy(4 + j, (*chip, c), sibling) for j, chip in enumerate(chips)]
    for j, chip in enumerate(chips):
        copy(1 + j, (*chip, c), me).wait_recv()
        passed[j].start()
    copy(0, sibling, me).wait_recv()
    for j, chip in enumerate(chips):
        copy(4 + j, (*chip, 1 - c), me).wait_recv()
    for cp in first + passed:
        cp.wait_send()
    mine.wait()


def gather_small(x):
    return _call(
        _two_level_gather_body(), name="gather_small",
        in_specs=[pl.BlockSpec(memory_space=pltpu.VMEM)], out_specs=pl.BlockSpec(memory_space=pltpu.VMEM),
        out_shape=jax.ShapeDtypeStruct((N_DEV,) + x.shape, x.dtype),
        scratch=[pltpu.SemaphoreType.DMA((7,)), pltpu.SemaphoreType.DMA((7,)), pltpu.SemaphoreType.DMA(())],
    )(x)


def _two_level_gather_body():
    def body(x_ref, out_ref, send_sems, recv_sems, local_sem):
        _two_level_gather(x_ref, out_ref, send_sems, recv_sems, local_sem)
    return body


_HBM = pl.BlockSpec(memory_space=pltpu.HBM)
_SEM = pl.BlockSpec(memory_space=pltpu.SEMAPHORE)
_EFFECT = pltpu.SideEffectType.DATAFLOW_SIDE_EFFECTING
_RELATIONS = (4, 2, 6, 5, 3, 7, 1)


def _split_copy(src_ref, land_ref, send_sems, recv_sems, k, gather):
    x, y, c = _mesh_pos()
    px, py, pc = (1 - x if k & 4 else x), (1 - y if k & 2 else y), (1 - c if k & 1 else c)
    if gather:
        src, dst = src_ref, land_ref.at[4 * x + 2 * y + c]
    else:
        src, dst = src_ref.at[4 * px + 2 * py + pc], land_ref.at[k - 1]
    return pltpu.make_async_remote_copy(src_ref=src, dst_ref=dst, send_sem=send_sems.at[k - 1],
                                        recv_sem=recv_sems.at[k - 1], device_id=(px, py, pc), device_id_type=MESH)


def copy_start(src, land_shape, *, gather, name):
    def body(src_ref, land_ref, send_sems, recv_sems, src_thru, land_thru, token):
        for k in _RELATIONS:
            _split_copy(src_ref, land_ref, send_sems, recv_sems, k, gather).start()
        token[...] = jnp.zeros_like(token)

    return pl.pallas_call(
        body, name=name,
        out_shape=(pltpu.SemaphoreType.DMA((7,)), pltpu.SemaphoreType.DMA((7,)), pltpu.HBM(src.shape, src.dtype),
                   pltpu.HBM(land_shape, src.dtype), jax.ShapeDtypeStruct((8, 128), F32)),
        in_specs=(_HBM, _HBM), out_specs=(_SEM, _SEM, _HBM, _HBM, pl.BlockSpec(memory_space=pltpu.VMEM)),
        input_output_aliases={0: 2, 1: 3},
        compiler_params=pltpu.CompilerParams(has_side_effects=_EFFECT),
    )(pltpu.with_memory_space_constraint(src, pltpu.HBM),
      pltpu.with_memory_space_constraint(lax.empty(land_shape, src.dtype), pltpu.HBM))


def copy_wait(started, after, *, gather, name):
    send_sems, recv_sems, src_thru, land_thru, _ = started

    def body(src_ref, land_ref, send_sems, recv_sems, after_ref, src_dead, got_ref):
        for k in _RELATIONS:
            cp = _split_copy(src_ref, land_ref, send_sems, recv_sems, k, gather)
            cp.wait_send()
            cp.wait_recv()

    return pl.pallas_call(
        body, name=name,
        out_shape=(pltpu.HBM(src_thru.shape, src_thru.dtype), pltpu.HBM(land_thru.shape, land_thru.dtype)),
        in_specs=(_HBM, _HBM, _SEM, _SEM, pl.BlockSpec(memory_space=pl.ANY)), out_specs=(_HBM, _HBM),
        input_output_aliases={0: 0, 1: 1},
        compiler_params=pltpu.CompilerParams(has_side_effects=_EFFECT),
    )(src_thru, land_thru, send_sems, recv_sems, after)


WEIGHTS = ['ln1_g', 'ln1_b', 'ffn1_w_gate', 'ffn1_w_up', 'ffn1_w_down', 'mix_w_in', 'pool_w', 'pool_scale',
           'sconv_w', 'cconv_w', 'cconv_b', 'cnorm_g', 'cnorm_b', 'mix_w_out', 'ln2_g', 'ln2_b',
           'ffn2_w_gate', 'ffn2_w_up', 'ffn2_w_down', 'ln3_g', 'ln3_b']


def _pack_small(pieces):
    flat = jnp.concatenate([p.reshape(-1) for p in pieces])
    pad = -flat.shape[0] % 1024
    return jnp.pad(flat, (0, pad)).reshape(-1, 128)


def _unpack_small(flat, shapes):
    out, off = [], 0
    for sh in shapes:
        n = 1
        for s in sh:
            n *= s
        out.append(flat[off:off + n].reshape(sh))
        off += n
    return out


def _pad_rows(a, rows):
    return jnp.pad(a, ((0, rows - a.shape[0]), (0, 0)))


def kernel(x, ln1_g, ln1_b, ffn1_w_gate, ffn1_w_up, ffn1_w_down, mix_w_in, pool_w, pool_scale, sconv_w, cconv_w, cconv_b, cnorm_g, cnorm_b, mix_w_out, ln2_g, ln2_b, ffn2_w_gate, ffn2_w_up, ffn2_w_down, ln3_g, ln3_b, loss_target, m_ln1_g, m_ln1_b, m_ffn1_w_gate, m_ffn1_w_up, m_ffn1_w_down, m_mix_w_in, m_pool_w, m_pool_scale, m_sconv_w, m_cconv_w, m_cconv_b, m_cnorm_g, m_cnorm_b, m_mix_w_out, m_ln2_g, m_ln2_b, m_ffn2_w_gate, m_ffn2_w_up, m_ffn2_w_down, m_ln3_g, m_ln3_b, v_ln1_g, v_ln1_b, v_ffn1_w_gate, v_ffn1_w_up, v_ffn1_w_down, v_mix_w_in, v_pool_w, v_pool_scale, v_sconv_w, v_cconv_w, v_cconv_b, v_cnorm_g, v_cnorm_b, v_mix_w_out, v_ln2_g, v_ln2_b, v_ffn2_w_gate, v_ffn2_w_up, v_ffn2_w_down, v_ln3_g, v_ln3_b):
    a = dict(locals())
    depth, d = ln1_g.shape
    t = x.shape[1]
    fs = ffn1_w_gate.shape[2]
    f = fs * N_DEV
    ins = mix_w_in.shape[2]
    ncols = ins * N_DEV
    outs = mix_w_out.shape[1]
    p, s = d // 4, 3 * d // 8
    pg = p // 4
    cs = sconv_w.shape[2]
    alpha = (2.0 * depth) ** 0.25
    me = 4 * lax.axis_index("x") + 2 * lax.axis_index("y") + lax.axis_index("c")
    tm = min(512, t)
    tm_bwd = min(256, t)
    tm_tn = min(1024, t)
    tt_fwd = min(512, t)
    tt_bwd = min(512, t)

    sizes = {"wg1": fs, "wu1": fs, "wd1": fs, "win": ins, "wout": outs, "wg2": fs, "wu2": fs, "wd2": fs,
             "conv": 16}
    per_layer = ["wg1", "wu1", "wd1", "win", "wout", "wg2", "wu2", "wd2"]
    wire = jnp.dtype(F32).itemsize // jnp.dtype(MM).itemsize
    n_conv = (SCONV_K + CCONV_K) * cs

    def conv_rows(l):
        flat = jnp.concatenate([sconv_w[l].reshape(-1), cconv_w[l].reshape(-1)])
        bits = lax.bitcast_convert_type(flat, MM).reshape(-1)
        return jnp.pad(bits, (0, sizes["conv"] * d - bits.shape[0])).reshape(sizes["conv"], d)

    def layer_block(l, key):
        if key == "conv":
            return conv_rows(l)
        return {"wg1": ffn1_w_gate[l].T, "wu1": ffn1_w_up[l].T, "wd1": ffn1_w_down[l], "win": mix_w_in[l].T,
                "wout": mix_w_out[l], "wg2": ffn2_w_gate[l].T, "wu2": ffn2_w_up[l].T,
                "wd2": ffn2_w_down[l]}[key].astype(MM)

    def gather_groups(l):
        if l == 0:
            return [["wg1", "wu1"], ["wd1"], ["win", "wout", "wg2", "wu2", "wd2", "conv"]]
        return [per_layer + ["conv"]]

    gathers = {}
    for l in range(depth):
        for gi, keys in enumerate(gather_groups(l)):
            src = jnp.concatenate([layer_block(l, k) for k in keys], axis=0)
            gathers[l, gi] = copy_start(src, (N_DEV,) + src.shape, gather=True, name=f"gather_start_{l}_{gi}")
    started = sum(st[4][0:1, 0:1] for st in gathers.values())

    def gathered(l, gi, after):
        mine, land = copy_wait(gathers[l, gi], after, gather=True, name=f"gather_wait_{l}_{gi}")
        wall = lax.dynamic_update_slice(land, mine[None], (me, 0, 0))
        out, off = {}, 0
        for k in gather_groups(l)[gi]:
            out[k] = wall[:, off:off + sizes[k], :].reshape(N_DEV * sizes[k], d)
            off += sizes[k]
        return out

    def conv_filters(rows):
        bits = rows.reshape(N_DEV, -1)[:, :n_conv * wire]
        vals = lax.bitcast_convert_type(bits.reshape(N_DEV, n_conv, wire) if wire > 1 else bits, F32)
        both = vals.reshape(N_DEV, SCONV_K + CCONV_K, cs).transpose(1, 0, 2).reshape(SCONV_K + CCONV_K, s)
        return _pad_rows(both[:SCONV_K], 8), _pad_rows(both[SCONV_K:], 32)

    eye = jnp.eye(4, dtype=F32)
    wbd_all = (pool_w[:, :, :, None, :] * eye[None, :, None, :, None]).reshape(depth, p, p)

    def row(v):
        return v.reshape(1, -1)

    saved = []
    cur, gam, bet = x[0], jnp.ones((1, d), F32), jnp.zeros((1, d), F32) + started
    for l in range(depth):
        w = gathered(l, 0, started if l == 0 else cur)
        sv = {"w": w}
        sv["xb1"], sv["g1"], sv["u1"], sv["h1"] = ffn_up(cur, gam, bet, w["wg1"].T, w["wu1"].T, tm=tm)
        if l == 0:
            w.update(gathered(l, 1, sv["xb1"]))
        sv["xh1"], sv["rs1"] = mm_res_ln(sv["h1"], w["wd1"], cur, gam, bet, alpha=alpha, scale=0.5, tm=tm)
        if l == 0:
            w.update(gathered(l, 2, sv["xh1"]))
        g1, b1 = row(ln1_g[l]), row(ln1_b[l])
        sv["proj"], sv["xb2"] = mm_in(sv["xh1"], g1, b1, w["win"].T, tm=tm)
        sv["wbd"] = wbd_all[l].astype(MM)
        sv["sw"], sv["cw"] = conv_filters(w["conv"])
        sv["cat"] = mixer_fwd(sv["proj"], sv["wbd"], row(pool_scale[l]), sv["sw"], sv["cw"], row(cconv_b[l]),
                              row(cnorm_g[l]), row(cnorm_b[l]), d=d, tt=tt_fwd)
        sv["xh2"], sv["rs2"] = mm_res_ln(sv["cat"], w["wout"], sv["xh1"], g1, b1, alpha=alpha, scale=1.0, tm=tm)
        g2, b2 = row(ln2_g[l]), row(ln2_b[l])
        sv["xb3"], sv["g3"], sv["u3"], sv["h3"] = ffn_up(sv["xh2"], g2, b2, w["wg2"].T, w["wu2"].T, tm=tm)
        sv["xh3"], sv["rs3"] = mm_res_ln(sv["h3"], w["wd2"], sv["xh2"], g2, b2, alpha=alpha, scale=0.5, tm=tm)
        saved.append(sv)
        cur, gam, bet = sv["xh3"], row(ln3_g[l]), row(ln3_b[l])

    dcur, lsum = loss_head(cur, gam, bet, loss_target[0], tm=tm)
    loss = lax.psum(lsum[0, 0] * (0.5 / d), MESH_AXES)

    exchanges = []
    small = [None] * depth

    def exchange(l, keys, gws):
        rows = sum(sizes[k] for k in keys)
        pad = -rows % 256 if _row_tile(rows, 256) < 64 else 0
        parts = [g.astype(MM).reshape(N_DEV, sizes[k], d) for k, g in zip(keys, gws)]
        mine = [lax.dynamic_slice_in_dim(g, me * sizes[k], sizes[k], axis=0) for k, g in zip(keys, gws)]
        if pad:
            parts.append(jnp.zeros((N_DEV, pad, d), MM))
            mine.append(jnp.zeros((pad, d), F32))
        st = copy_start(jnp.concatenate(parts, axis=1), (N_DEV - 1, rows + pad, d), gather=False,
                        name=f"exchange_start_{l}_{keys[0]}")
        exchanges.append((l, keys, st, jnp.concatenate(mine, axis=0)))
        return st[4][0:1, 0:1]

    sent = jnp.zeros((1, 1), F32)
    for l in reversed(range(depth)):
        sv = saved[l]
        w = sv["w"]
        dx, dyb, dg, du, dg3, db3 = ffn_bwd(dcur, sv["xh3"], sv["rs3"], row(ln3_g[l]) + sent, w["wd2"].T, w["wg2"],
                                             w["wu2"], sv["g3"], sv["u3"], alpha=alpha, tm=tm_bwd)
        gw_g2, gw_u2 = tn_matmul(dg, sv["xb3"], tm=tm_tn), tn_matmul(du, sv["xb3"], tm=tm_tn)
        gw_d2 = tn_matmul(sv["h3"], dyb, tm=tm_tn)
        dz, dzb, dcat, dg2, db2 = lnbwd_mm(dx, sv["xh2"], sv["rs2"], row(ln2_g[l]), w["wout"].T, tm=tm)
        dproj, dwbd, dps, dsw, dcw, dcb, dcg, dcbt = mixer_bwd(
            sv["proj"], dcat, sv["wbd"], sv["wbd"].T, row(pool_scale[l]), sv["sw"], sv["cw"], row(cconv_b[l]),
            row(cnorm_g[l]), row(cnorm_b[l]), d=d, tt=tt_bwd)
        gw_out = tn_matmul(sv["cat"], dzb, tm=tm_tn)
        gw_in = tn_matmul(dproj, sv["xb2"], tm=tm_tn)
        if l == 0:
            sent = exchange(l, ["wg2", "wu2", "wd2", "wout", "win"], [gw_g2, gw_u2, gw_d2, gw_out, gw_in])
        dx = mm_add(dproj, w["win"], dz, alpha=alpha, tm=tm)
        dx, dyb, dg, du, dg1, db1 = ffn_bwd(dx, sv["xh1"], sv["rs1"], row(ln1_g[l]) + sent, w["wd1"].T, w["wg1"],
                                             w["wu1"], sv["g1"], sv["u1"], alpha=alpha, tm=tm_bwd)
        dcur = dx
        if l == 0:
            sent = exchange(l, ["wg1"], [tn_matmul(dg, sv["xb1"], tm=tm_tn)])
            sent = exchange(l, ["wu1"], [tn_matmul(du, sv["xb1"], tm=tm_tn, after=sent)])
            sent = exchange(l, ["wd1"], [tn_matmul(sv["h1"], dyb, tm=tm_tn, after=sent)])
        else:
            gw_g1, gw_u1 = tn_matmul(dg, sv["xb1"], tm=tm_tn), tn_matmul(du, sv["xb1"], tm=tm_tn)
            gw_d1 = tn_matmul(sv["h1"], dyb, tm=tm_tn)
            sent = exchange(l, per_layer, [gw_g1, gw_u1, gw_d1, gw_in, gw_out, gw_g2, gw_u2, gw_d2])
        dpw = jnp.stack([dwbd[g * pg:(g + 1) * pg, g * pg:(g + 1) * pg] for g in range(4)])
        small[l] = [dg1, db1, dg2, db2, dg3, db3, dpw, dps, dsw[:SCONV_K], dcw[:CCONV_K], dcb, dcg, dcbt]
    grad_x = dcur[None]

    by_key = {k: [None] * depth for k in per_layer}
    for l, keys, st, mine in exchanges:
        recv = copy_wait(st, dcur, gather=False, name=f"exchange_wait_{l}_{keys[0]}")[1]
        gsum, off = sum_parts(mine, recv), 0
        for k in keys:
            by_key[k][l] = gsum[off:off + sizes[k]]
            off += sizes[k]

    small_shapes = [g.shape for g in small[0]]
    small_flat = _pack_small([g for l in range(depth) for g in small[l]])
    small_sum = sum_gathered(gather_small(small_flat)).reshape(-1)
    small_g = _unpack_small(small_sum, small_shapes * depth)
    n_small = len(small_shapes)

    grads = {}
    grads["ffn1_w_gate"] = jnp.stack([g.T for g in by_key["wg1"]])
    grads["ffn1_w_up"] = jnp.stack([g.T for g in by_key["wu1"]])
    grads["ffn1_w_down"] = jnp.stack(by_key["wd1"])
    grads["mix_w_in"] = jnp.stack([g.T for g in by_key["win"]])
    grads["mix_w_out"] = jnp.stack(by_key["wout"])
    grads["ffn2_w_gate"] = jnp.stack([g.T for g in by_key["wg2"]])
    grads["ffn2_w_up"] = jnp.stack([g.T for g in by_key["wu2"]])
    grads["ffn2_w_down"] = jnp.stack(by_key["wd2"])
    small_names = ["ln1_g", "ln1_b", "ln2_g", "ln2_b", "ln3_g", "ln3_b", "pool_w", "pool_scale", "sconv_w",
                   "cconv_w", "cconv_b", "cnorm_g", "cnorm_b"]
    for idx, name in enumerate(small_names):
        full = jnp.stack([small_g[l * n_small + idx] for l in range(depth)])
        if name in ("sconv_w", "cconv_w"):
            full = lax.dynamic_slice_in_dim(full, me * cs, cs, axis=2)
        grads[name] = full.reshape(a[name].shape)

    deltas, new_m, new_v = {}, {}, {}
    for name in WEIGHTS:
        w = a[name]
        c = w.shape[-1]
        dl, mn, vn = adamw(w.reshape(-1, c), grads[name].reshape(-1, c), a["m_" + name].reshape(-1, c),
                           a["v_" + name].reshape(-1, c))
        deltas[name], new_m[name], new_v[name] = dl.reshape(w.shape), mn.reshape(w.shape), vn.reshape(w.shape)

    return (loss, grad_x, *[grads[n] for n in WEIGHTS], *[deltas[n] for n in WEIGHTS],
            *[new_m[n] for n in WEIGHTS], *[new_v[n] for n in WEIGHTS])
```

```python
import functools

import jax
import jax.numpy as jnp
from jax import lax
from jax.experimental import pallas as pl
from jax.experimental.pallas import tpu as pltpu

F32 = jnp.float32
MM = jnp.bfloat16
LN_EPS = 1e-5
N_DEV = 8
MESH_AXES = ("x", "y", "c")
HALO = 32
CCONV_K = 31
SCONV_K = 3
CONV_ROWS = 32
FFN_CHUNK = 256
VMEM_LIMIT = 56 * 1024 * 1024
ADAM_LR, ADAM_B1, ADAM_B2, ADAM_EPS, ADAM_WD, ADAM_STEP = 0.001, 0.9, 0.999, 1e-08, 0.01, 10
MESH = pl.DeviceIdType.MESH


def _call(body, *, name, out_shape, in_specs, out_specs, grid=None, scratch=(), sem=None):
    kw = {}
    if grid is not None:
        kw["grid"] = grid
    params = dict(vmem_limit_bytes=VMEM_LIMIT)
    if sem is not None:
        params["dimension_semantics"] = sem
    return pl.pallas_call(body, name=name, out_shape=out_shape, in_specs=in_specs, out_specs=out_specs,
                          scratch_shapes=list(scratch), compiler_params=pltpu.CompilerParams(**params), **kw)


def _rows(tm, n):
    return pl.BlockSpec((tm, n), lambda i: (i, 0))


def _const(shape):
    nd = len(shape)
    return pl.BlockSpec(shape, lambda *_: (0,) * nd, pipeline_mode=pl.Buffered(1))


def _acc(shape):
    nd = len(shape)
    return pl.BlockSpec(shape, lambda *_: (0,) * nd)


def _row_tile(rows, cap):
    best = None
    for t in range(8, min(rows, cap) + 1, 8):
        if rows % t == 0:
            best = t
    return best if best is not None else rows


def _sig(x):
    return 1.0 / (1.0 + jnp.exp(-x))


def _ln_stats(z):
    mu = jnp.mean(z, axis=-1, keepdims=True)
    zc = z - mu
    var = jnp.mean(zc * zc, axis=-1, keepdims=True)
    rstd = lax.rsqrt(var + LN_EPS)
    return zc * rstd, rstd


def _ln_bwd(dout, xhat, rstd, gamma):
    dxh = dout * gamma
    m1 = jnp.mean(dxh, axis=-1, keepdims=True)
    m2 = jnp.mean(dxh * xhat, axis=-1, keepdims=True)
    return rstd * (dxh - m1 - xhat * m2)


def _colsum(v):
    return jnp.sum(v, axis=0, keepdims=True)


def _f_chunks(f, width=None):
    if width is not None and f % width == 0:
        return f // width, width
    n = 2 if f >= 2048 and f % 256 == 0 else 1
    return n, f // n


def _dot_nt(a, b):
    return lax.dot_general(a, b, (((1,), (1,)), ((), ())), preferred_element_type=F32)


def ffn_up(xin, gam, bet, wgt, wut, *, tm):
    t, d = xin.shape
    f = wgt.shape[0]
    nch, fc = _f_chunks(f, FFN_CHUNK)

    def body(x_ref, g_ref, b_ref, wg_ref, wu_ref, xb_ref, go_ref, uo_ref, h_ref):
        xb = (x_ref[...] * g_ref[...] + b_ref[...]).astype(MM)
        xb_ref[...] = xb
        for c in range(nch):
            sl = slice(c * fc, (c + 1) * fc)
            g = _dot_nt(xb, wg_ref[sl, :])
            u = _dot_nt(xb, wu_ref[sl, :])
            go_ref[:, sl] = g.astype(MM)
            uo_ref[:, sl] = u.astype(MM)
            h_ref[:, sl] = (g * _sig(g) * u).astype(MM)

    return _call(
        body, name="ffn_up", grid=(t // tm,), sem=("parallel",),
        in_specs=[_rows(tm, d), _const((1, d)), _const((1, d)), _const((f, d)), _const((f, d))],
        out_specs=[_rows(tm, d), _rows(tm, f), _rows(tm, f), _rows(tm, f)],
        out_shape=[jax.ShapeDtypeStruct((t, d), MM)] + [jax.ShapeDtypeStruct((t, f), MM)] * 3,
    )(xin, gam, bet, wgt, wut)


def mm_res_ln(a, w, xin, gam, bet, *, alpha, scale, tm):
    t, k = a.shape
    d = w.shape[1]

    def body(a_ref, w_ref, x_ref, g_ref, b_ref, xh_ref, rs_ref):
        y = jnp.dot(a_ref[...], w_ref[...], preferred_element_type=F32)
        x = x_ref[...] * g_ref[...] + b_ref[...]
        xh, rstd = _ln_stats(alpha * x + scale * y)
        xh_ref[...] = xh
        rs_ref[...] = rstd

    return _call(
        body, name="mm_res_ln", grid=(t // tm,), sem=("parallel",),
        in_specs=[_rows(tm, k), _const((k, d)), _rows(tm, d), _const((1, d)), _const((1, d))],
        out_specs=[_rows(tm, d), _rows(tm, 1)],
        out_shape=[jax.ShapeDtypeStruct((t, d), F32), jax.ShapeDtypeStruct((t, 1), F32)],
    )(a, w, xin, gam, bet)


def mm_in(xin, gam, bet, wt, *, tm):
    t, d = xin.shape
    n = wt.shape[0]

    def body(x_ref, g_ref, b_ref, w_ref, o_ref, xb_ref):
        xb = (x_ref[...] * g_ref[...] + b_ref[...]).astype(MM)
        xb_ref[...] = xb
        o_ref[...] = _dot_nt(xb, w_ref[...])

    return _call(
        body, name="mm_in", grid=(t // tm,), sem=("parallel",),
        in_specs=[_rows(tm, d), _const((1, d)), _const((1, d)), _const((n, d))],
        out_specs=[_rows(tm, n), _rows(tm, d)],
        out_shape=[jax.ShapeDtypeStruct((t, n), F32), jax.ShapeDtypeStruct((t, d), MM)],
    )(xin, gam, bet, wt)


def loss_head(xh, gam, bet, target, *, tm):
    t, d = xh.shape

    def body(x_ref, g_ref, b_ref, t_ref, dy_ref, l_ref):
        @pl.when(pl.program_id(0) == 0)
        def _():
            l_ref[...] = jnp.zeros_like(l_ref)

        e = x_ref[...] * g_ref[...] + b_ref[...] - t_ref[...]
        dy_ref[...] = e * (1.0 / d)
        l_ref[...] += jnp.sum(_colsum(e * e), axis=1, keepdims=True)

    return _call(
        body, name="loss_head", grid=(t // tm,), sem=("arbitrary",),
        in_specs=[_rows(tm, d), _const((1, d)), _const((1, d)), _rows(tm, d)],
        out_specs=[_rows(tm, d), _acc((1, 1))],
        out_shape=[jax.ShapeDtypeStruct((t, d), F32), jax.ShapeDtypeStruct((1, 1), F32)],
    )(xh, gam, bet, target)


def _halo_specs(tt, ncols, t):
    per, last = tt // HALO, t // HALO - 1
    return [pl.BlockSpec((HALO, ncols), lambda i: (jnp.maximum(i * per - 1, 0), 0)),
            pl.BlockSpec((tt, ncols), lambda i: (i, 0)),
            pl.BlockSpec((HALO, ncols), lambda i: (jnp.minimum((i + 1) * per, last), 0))]


def _fill_ext(ext, prev_ref, cur_ref, next_ref, i, nt, tt):
    ext[0:HALO, :] = jnp.where(i > 0, prev_ref[...], 0.0)
    ext[HALO:HALO + tt, :] = cur_ref[...]
    ext[HALO + tt:HALO + tt + HALO, :] = jnp.where(i < nt - 1, next_ref[...], 0.0)


def _make_shifts(sh, n):
    for r in range(1, 8):
        sh[r, 0:n, :] = sh[0, r:r + n, :]


def _shift_reader(sh):
    def read(o, rows):
        r = o % 8
        return sh[r, o - r:o - r + rows, :]
    return read


def _conv_chunks_shifted(sh, w_ref, ktaps, src0, nrows, flip=False):
    for r0 in range(0, nrows, CONV_ROWS):
        acc = None
        base = src0 + r0
        for r in range(8):
            ks = [k for k in range(ktaps) if (base + k) % 8 == r]
            if not ks:
                continue
            lo = base + ks[0] - r
            slab = sh[r, lo:lo + CONV_ROWS + 8 * (len(ks) - 1), :]
            for q, k in enumerate(ks):
                kk = ktaps - 1 - k if flip else k
                term = w_ref[kk:kk + 1, :] * slab[8 * q:8 * q + CONV_ROWS, :]
                acc = term if acc is None else acc + term
        yield r0, acc


def _ref_reader(ref):
    def read(o, rows):
        return ref[o:o + rows, :]
    return read


def _conv_chunks(read, w_ref, ktaps, src0, nrows, flip=False):
    for r0 in range(0, nrows, CONV_ROWS):
        acc = None
        for k in range(ktaps):
            kk = ktaps - 1 - k if flip else k
            term = w_ref[kk:kk + 1, :] * read(src0 + r0 + k, CONV_ROWS)
            acc = term if acc is None else acc + term
        yield r0, acc


def _pool_groups(nrows, p):
    lane = lax.broadcasted_iota(jnp.int32, (nrows, p), 1)
    g = p // 4
    return lane < g, lane < 2 * g, lane < 3 * g


def _pool_select(groups, v2, v4, v8, v16):
    g0, g1, g2 = groups
    return jnp.where(g0, v2, jnp.where(g1, v4, jnp.where(g2, v8, v16)))


def _pool_count(groups, i, tt, row0, nrows, p, t):
    pos = i * tt + (row0 - HALO) + lax.broadcasted_iota(jnp.int32, (nrows, p), 0)
    half = _pool_select(groups, 1, 2, 4, 8)
    lo = jnp.clip(pos - half, 0, t)
    hi = jnp.clip(pos + half, 0, t)
    return jnp.maximum(hi - lo, 1).astype(F32)


def _pool_forward(ext, s2, s4, s8, groups, cnt, tt, p):
    e = tt + 2 * HALO
    s2[8:e - 8, :] = ext[7:e - 9, 0:p] + ext[8:e - 8, 0:p]
    s4[16:e - 16, :] = s2[15:e - 17, :] + s2[17:e - 15, :]
    s8[24:e - 24, :] = s4[22:e - 26, :] + s4[26:e - 22, :]
    s16 = s8[28:e - 36, :] + s8[36:e - 28, :]
    c = slice(HALO, HALO + tt)
    tot = _pool_select(groups, s2[c, :], s4[c, :], s8[c, :], s16)
    return tot / cnt - ext[c, 0:p]


def mixer_fwd(proj, wbd, pscale, sw, cw, cb, cg, cbt, *, d, tt):
    t, ncols = proj.shape
    p, s = d // 4, 3 * d // 8
    o_gb, o_gc, o_v, o_cv, o_cg = p, p + s, p + 2 * s, p + 3 * s, p + 4 * s
    nt, e = t // tt, tt + 2 * HALO

    def body(prev_ref, cur_ref, next_ref, wbd_ref, ps_ref, sw_ref, cw_ref, cb_ref, cg_ref, cbt_ref,
             cat_ref, ext, a_sh, cv_s, s2, s4, s8):
        i = pl.program_id(0)
        _fill_ext(ext, prev_ref, cur_ref, next_ref, i, nt, tt)
        a_sh[0, 0:e, :] = ext[:, o_cv:o_cv + s] * _sig(ext[:, o_cg:o_cg + s])
        a_sh[0, e:e + 8, :] = jnp.zeros((8, s), F32)
        _make_shifts(a_sh, e)
        for r0, acc in _conv_chunks_shifted(a_sh, cw_ref, CCONV_K, HALO - CCONV_K // 2, tt):
            n, _ = _ln_stats(acc + cb_ref[...])
            yn = n * cg_ref[...] + cbt_ref[...]
            cat_ref[r0:r0 + CONV_ROWS, p + s:d] = (yn * _sig(yn)).astype(MM)
        cv_s[...] = ext[:, o_gc:o_gc + s] * ext[:, o_v:o_v + s]
        for r0, acc in _conv_chunks(_ref_reader(cv_s), sw_ref, SCONV_K, HALO - SCONV_K // 2, tt):
            gb = ext[HALO + r0:HALO + r0 + CONV_ROWS, o_gb:o_gb + s]
            cat_ref[r0:r0 + CONV_ROWS, p:p + s] = (gb * acc).astype(MM)
        groups = _pool_groups(tt, p)
        cnt = _pool_count(groups, i, tt, HALO, tt, p, t)
        pooled = _pool_forward(ext, s2, s4, s8, groups, cnt, tt, p)
        ya = jnp.dot(pooled.astype(MM), wbd_ref[...], preferred_element_type=F32) * ps_ref[...]
        cat_ref[:, 0:p] = ya.astype(MM)

    return _call(
        body, name="mixer_fwd", grid=(nt,), sem=("parallel",),
        in_specs=_halo_specs(tt, ncols, t) + [_const((p, p)), _const((1, p)), _const((8, s)), _const((32, s)),
                                               _const((1, s)), _const((1, s)), _const((1, s))],
        out_specs=_rows(tt, d),
        out_shape=jax.ShapeDtypeStruct((t, d), MM),
        scratch=[pltpu.VMEM((e, ncols), F32), pltpu.VMEM((8, e + 8, s), F32), pltpu.VMEM((e, s), F32),
                 pltpu.VMEM((e, p), F32), pltpu.VMEM((e, p), F32), pltpu.VMEM((e, p), F32)],
    )(proj, proj, proj, wbd, pscale, sw, cw, cb, cg, cbt)


def mixer_bwd(proj, dcat, wbd, wbdt, pscale, sw, cw, cb, cg, cbt, *, d, tt):
    t, ncols = proj.shape
    p, s = d // 4, 3 * d // 8
    o_gb, o_gc, o_v, o_cv, o_cg = p, p + s, p + 2 * s, p + 3 * s, p + 4 * s
    nt, e = t // tt, tt + 2 * HALO
    pad = 16

    def body(pp_ref, pc_ref, pn_ref, dp_ref, dc_ref, dn_ref, wbd_ref, wbdt_ref, ps_ref, sw_ref, cw_ref,
             cb_ref, cg_ref, cbt_ref,
             dproj_ref, dwbd_ref, dps_ref, dsw_ref, dcw_ref, dcb_ref, dcg_ref, dcbt_ref,
             ext, dext, a_sh, b_sh, sg_s, cv_s, ds_s, q_s, r2, r4, r8):
        i = pl.program_id(0)

        @pl.when(i == 0)
        def _():
            for ref in (dwbd_ref, dps_ref, dsw_ref, dcw_ref, dcb_ref, dcg_ref, dcbt_ref):
                ref[...] = jnp.zeros_like(ref)

        _fill_ext(ext, pp_ref, pc_ref, pn_ref, i, nt, tt)
        _fill_ext(dext, dp_ref, dc_ref, dn_ref, i, nt, tt)
        c = slice(HALO, HALO + tt)

        sg_s[...] = _sig(ext[:, o_cg:o_cg + s])
        a_sh[0, 0:pad, :] = jnp.zeros((pad, s), F32)
        a_sh[0, pad + e:pad + e + pad + 8, :] = jnp.zeros((pad + 8, s), F32)
        a_sh[0, pad:pad + e, :] = ext[:, o_cv:o_cv + s] * sg_s[...]
        _make_shifts(a_sh, e + 2 * pad)
        read_a = _shift_reader(a_sh)
        for r0, acc in _conv_chunks_shifted(a_sh, cw_ref, CCONV_K, pad - CCONV_K // 2, e):
            b = acc + cb_ref[...]
            n, rstd = _ln_stats(b)
            yn = n * cg_ref[...] + cbt_ref[...]
            sy = _sig(yn)
            rows = slice(r0, r0 + CONV_ROWS)
            dyn = dext[rows, p + s:d] * (sy * (1.0 + yn * (1.0 - sy)))
            db = _ln_bwd(dyn, n, rstd, cg_ref[...])
            b_sh[0, rows, :] = db
            if HALO <= r0 < HALO + tt:
                dcg_ref[...] += _colsum(dyn * n)
                dcbt_ref[...] += _colsum(dyn)
                dcb_ref[...] += _colsum(db)
        b_sh[0, e:e + 8, :] = jnp.zeros((8, s), F32)
        _make_shifts(b_sh, e)
        for r0, da in _conv_chunks_shifted(b_sh, cw_ref, CCONV_K, HALO - CCONV_K // 2, tt, flip=True):
            rows = slice(HALO + r0, HALO + r0 + CONV_ROWS)
            sg = sg_s[rows, :]
            dproj_ref[r0:r0 + CONV_ROWS, o_cv:o_cv + s] = (da * sg).astype(MM)
            dproj_ref[r0:r0 + CONV_ROWS, o_cg:o_cg + s] = (
                da * ext[rows, o_cv:o_cv + s] * sg * (1.0 - sg)).astype(MM)
        for k in range(CCONV_K):
            lo = pad + HALO + k - CCONV_K // 2
            dcw_ref[k:k + 1, :] += _colsum(b_sh[0, c, :] * read_a(lo, tt))

        cv_s[...] = ext[:, o_gc:o_gc + s] * ext[:, o_v:o_v + s]
        ds_s[...] = dext[:, p:p + s] * ext[:, o_gb:o_gb + s]
        for r0, acc in _conv_chunks(_ref_reader(cv_s), sw_ref, SCONV_K, HALO - SCONV_K // 2, tt):
            rows = slice(HALO + r0, HALO + r0 + CONV_ROWS)
            dproj_ref[r0:r0 + CONV_ROWS, o_gb:o_gb + s] = (dext[rows, p:p + s] * acc).astype(MM)
        for r0, dcv in _conv_chunks(_ref_reader(ds_s), sw_ref, SCONV_K, HALO - SCONV_K // 2, tt, flip=True):
            rows = slice(HALO + r0, HALO + r0 + CONV_ROWS)
            dproj_ref[r0:r0 + CONV_ROWS, o_gc:o_gc + s] = (dcv * ext[rows, o_v:o_v + s]).astype(MM)
            dproj_ref[r0:r0 + CONV_ROWS, o_v:o_v + s] = (dcv * ext[rows, o_gc:o_gc + s]).astype(MM)
        for k in range(SCONV_K):
            lo = HALO + k - SCONV_K // 2
            dsw_ref[k:k + 1, :] += _colsum(ds_s[c, :] * cv_s[lo:lo + tt, :])

        groups = _pool_groups(tt, p)
        cnt = _pool_count(groups, i, tt, HALO, tt, p, t)
        pooled = _pool_forward(ext, r2, r4, r8, groups, cnt, tt, p).astype(MM)
        ta = jnp.dot(pooled, wbd_ref[...], preferred_element_type=F32)
        dps_ref[...] += _colsum(dext[c, 0:p] * ta)
        dta = (dext[:, 0:p] * ps_ref[...]).astype(MM)
        dwbd_ref[...] += lax.dot_general(pooled, dta[HALO:HALO + tt, :], (((0,), (0,)), ((), ())),
                                         preferred_element_type=F32)
        dpool = jnp.dot(dta, wbdt_ref[...], preferred_element_type=F32)
        groups_e = _pool_groups(e, p)
        q_s[...] = dpool / _pool_count(groups_e, i, tt, 0, e, p, t)
        r2[8:e - 8, :] = q_s[8:e - 8, :] + q_s[9:e - 7, :]
        r4[16:e - 16, :] = r2[15:e - 17, :] + r2[17:e - 15, :]
        r8[24:e - 24, :] = r4[22:e - 26, :] + r4[26:e - 22, :]
        r16 = r8[28:e - 36, :] + r8[36:e - 28, :]
        du = _pool_select(groups, r2[c, :], r4[c, :], r8[c, :], r16) - dpool[HALO:HALO + tt, :]
        dproj_ref[:, 0:p] = du.astype(MM)

    small = [(p, p), (1, p), (8, s), (32, s), (1, s), (1, s), (1, s)]
    return _call(
        body, name="mixer_bwd", grid=(nt,), sem=("arbitrary",),
        in_specs=_halo_specs(tt, ncols, t) + _halo_specs(tt, d, t) + [
            _const((p, p)), _const((p, p)), _const((1, p)), _const((8, s)), _const((32, s)),
            _const((1, s)), _const((1, s)), _const((1, s))],
        out_specs=[_rows(tt, ncols)] + [_acc(sh) for sh in small],
        out_shape=[jax.ShapeDtypeStruct((t, ncols), MM)] + [jax.ShapeDtypeStruct(sh, F32) for sh in small],
        scratch=[pltpu.VMEM((e, ncols), F32), pltpu.VMEM((e, d), F32), pltpu.VMEM((8, e + 2 * pad + 8, s), F32),
                 pltpu.VMEM((8, e + 8, s), F32)] + [pltpu.VMEM((e, s), F32)] * 3 + [pltpu.VMEM((e, p), F32)] * 4,
    )(proj, proj, proj, dcat, dcat, dcat, wbd, wbdt, pscale, sw, cw, cb, cg, cbt)


def ffn_bwd(dout, xh, rs, gam, wd, wgt, wut, g, u, *, alpha, tm):
    t, d = dout.shape
    f = g.shape[1]
    nch, fc = _f_chunks(f, FFN_CHUNK)

    def body(do_ref, xh_ref, rs_ref, gm_ref, wd_ref, wgt_ref, wut_ref, g_ref, u_ref,
             dx_ref, dyb_ref, dg_ref, du_ref, dgam_ref, dbet_ref):
        @pl.when(pl.program_id(0) == 0)
        def _():
            dgam_ref[...] = jnp.zeros_like(dgam_ref)
            dbet_ref[...] = jnp.zeros_like(dbet_ref)

        dout_v, xhat = do_ref[...], xh_ref[...]
        dgam_ref[...] += _colsum(dout_v * xhat)
        dbet_ref[...] += _colsum(dout_v)
        dz = _ln_bwd(dout_v, xhat, rs_ref[...], gm_ref[...])
        dyb = (0.5 * dz).astype(MM)
        dyb_ref[...] = dyb
        for c in range(nch):
            sl = slice(c * fc, (c + 1) * fc)
            dh = _dot_nt(dyb, wd_ref[sl, :])
            gv, uv = g_ref[:, sl].astype(F32), u_ref[:, sl].astype(F32)
            sg = _sig(gv)
            dg_ref[:, sl] = (dh * uv * (sg * (1.0 + gv * (1.0 - sg)))).astype(MM)
            du_ref[:, sl] = (dh * (gv * sg)).astype(MM)
        dx_ref[...] = (alpha * dz + jnp.dot(dg_ref[...], wgt_ref[...], preferred_element_type=F32)
                       + jnp.dot(du_ref[...], wut_ref[...], preferred_element_type=F32))

    return _call(
        body, name="ffn_bwd", grid=(t // tm,), sem=("arbitrary",),
        in_specs=[_rows(tm, d), _rows(tm, d), _rows(tm, 1), _const((1, d)), _const((f, d)), _const((f, d)),
                  _const((f, d)), _rows(tm, f), _rows(tm, f)],
        out_specs=[_rows(tm, d), _rows(tm, d), _rows(tm, f), _rows(tm, f), _acc((1, d)), _acc((1, d))],
        out_shape=[jax.ShapeDtypeStruct((t, d), F32), jax.ShapeDtypeStruct((t, d), MM),
                   jax.ShapeDtypeStruct((t, f), MM), jax.ShapeDtypeStruct((t, f), MM),
                   jax.ShapeDtypeStruct((1, d), F32), jax.ShapeDtypeStruct((1, d), F32)],
    )(dout, xh, rs, gam, wd, wgt, wut, g, u)


def lnbwd_mm(dout, xh, rs, gam, w, *, tm):
    t, d = dout.shape
    n = w.shape[0]

    def body(do_ref, xh_ref, rs_ref, gm_ref, wt_ref, dz_ref, dzb_ref, da_ref, dgam_ref, dbet_ref):
        @pl.when(pl.program_id(0) == 0)
        def _():
            dgam_ref[...] = jnp.zeros_like(dgam_ref)
            dbet_ref[...] = jnp.zeros_like(dbet_ref)

        dout_v, xhat = do_ref[...], xh_ref[...]
        dgam_ref[...] += _colsum(dout_v * xhat)
        dbet_ref[...] += _colsum(dout_v)
        dz = _ln_bwd(dout_v, xhat, rs_ref[...], gm_ref[...])
        dz_ref[...] = dz
        dzb = dz.astype(MM)
        dzb_ref[...] = dzb
        da_ref[...] = _dot_nt(dzb, wt_ref[...])

    return _call(
        body, name="lnbwd_mm", grid=(t // tm,), sem=("arbitrary",),
        in_specs=[_rows(tm, d), _rows(tm, d), _rows(tm, 1), _const((1, d)), _const((n, d))],
        out_specs=[_rows(tm, d), _rows(tm, d), _rows(tm, n), _acc((1, d)), _acc((1, d))],
        out_shape=[jax.ShapeDtypeStruct((t, d), F32), jax.ShapeDtypeStruct((t, d), MM),
                   jax.ShapeDtypeStruct((t, n), F32), jax.ShapeDtypeStruct((1, d), F32),
                   jax.ShapeDtypeStruct((1, d), F32)],
    )(dout, xh, rs, gam, w)


def mm_add(a, w, r, *, alpha, tm):
    t, k = a.shape
    d = w.shape[1]

    def body(a_ref, w_ref, r_ref, o_ref):
        o_ref[...] = jnp.dot(a_ref[...], w_ref[...], preferred_element_type=F32) + alpha * r_ref[...]

    return _call(
        body, name="mm_add", grid=(t // tm,), sem=("parallel",),
        in_specs=[_rows(tm, k), _const((k, d)), _rows(tm, d)],
        out_specs=_rows(tm, d),
        out_shape=jax.ShapeDtypeStruct((t, d), F32),
    )(a, w, r)


def tn_matmul(a, b, *, tm, after=None):
    t, n = a.shape
    d = b.shape[1]
    nch, nc = _f_chunks(n)
    steps = t // tm

    def body(a_ref, b_ref, *rest):
        o_ref, ob_ref = rest[-2:]

        @pl.when(pl.program_id(1) == 0)
        def _():
            o_ref[...] = jnp.zeros_like(o_ref)

        o_ref[...] += lax.dot_general(a_ref[...], b_ref[...], (((0,), (0,)), ((), ())),
                                      preferred_element_type=F32)

        @pl.when(pl.program_id(1) == steps - 1)
        def _():
            ob_ref[...] = o_ref[...].astype(MM)

    extra = [] if after is None else [after]
    return _call(
        body, name="tn_matmul", grid=(nch, steps), sem=("parallel", "arbitrary"),
        in_specs=[pl.BlockSpec((tm, nc), lambda j, i: (i, j)), pl.BlockSpec((tm, d), lambda j, i: (i, 0))]
        + [pl.BlockSpec(memory_space=pl.ANY)] * len(extra),
        out_specs=[pl.BlockSpec((nc, d), lambda j, i: (j, 0))] * 2,
        out_shape=[jax.ShapeDtypeStruct((n, d), F32), jax.ShapeDtypeStruct((n, d), MM)],
    )(a, b, *extra)


def sum_parts(own, recv):
    r, d = own.shape
    n = recv.shape[0]
    tr = _row_tile(r, 256)

    def body(own_ref, recv_ref, o_ref):
        acc = own_ref[...]
        for k in range(n):
            acc = acc + recv_ref[k].astype(F32)
        o_ref[...] = acc

    return _call(
        body, name="sum_parts", grid=(r // tr,), sem=("parallel",),
        in_specs=[_rows(tr, d), pl.BlockSpec((n, tr, d), lambda i: (0, i, 0))],
        out_specs=_rows(tr, d),
        out_shape=jax.ShapeDtypeStruct((r, d), F32),
    )(own, recv)


def sum_gathered(parts):
    _, r, n = parts.shape

    def body(p_ref, o_ref):
        acc = p_ref[0]
        for j in range(1, N_DEV):
            acc = acc + p_ref[j]
        o_ref[...] = acc

    return _call(
        body, name="sum_gathered",
        in_specs=[pl.BlockSpec(memory_space=pltpu.VMEM)], out_specs=pl.BlockSpec(memory_space=pltpu.VMEM),
        out_shape=jax.ShapeDtypeStruct((r, n), F32),
    )(parts)


def adamw(w, g, m, v):
    r, c = w.shape
    tr = _row_tile(r, 512)

    def body(w_ref, g_ref, m_ref, v_ref, d_ref, mo_ref, vo_ref):
        gv = g_ref[...]
        mn = ADAM_B1 * m_ref[...] + (1.0 - ADAM_B1) * gv
        vn = ADAM_B2 * v_ref[...] + (1.0 - ADAM_B2) * (gv * gv)
        m_hat = mn / (1.0 - ADAM_B1 ** ADAM_STEP)
        v_hat = vn / (1.0 - ADAM_B2 ** ADAM_STEP)
        d_ref[...] = -ADAM_LR * (m_hat / (jnp.sqrt(v_hat) + ADAM_EPS) + ADAM_WD * w_ref[...])
        mo_ref[...] = mn
        vo_ref[...] = vn

    return _call(
        body, name="adamw", grid=(r // tr,), sem=("parallel",),
        in_specs=[_rows(tr, c)] * 4, out_specs=[_rows(tr, c)] * 3,
        out_shape=[jax.ShapeDtypeStruct((r, c), F32)] * 3,
    )(w, g, m, v)


def _mesh_pos():
    return lax.axis_index("x"), lax.axis_index("y"), lax.axis_index("c")


def _two_level_gather(x_ref, out_ref, send_sems, recv_sems, local_sem):
    x, y, c = _mesh_pos()
    me, sibling = (x, y, c), (x, y, 1 - c)
    chips = [(1 - x, y), (x, 1 - y), (1 - x, 1 - y)]

    def slot(px, py, pc):
        return out_ref.at[4 * px + 2 * py + pc]

    def copy(k, block, to, src=None):
        return pltpu.make_async_remote_copy(
            src_ref=slot(*block) if src is None else src, dst_ref=slot(*block),
            send_sem=send_sems.at[k], recv_sem=recv_sems.at[k], device_id=to, device_id_type=MESH)

    mine = pltpu.make_async_copy(x_ref, slot(*me), local_sem)
    mine.start()
    first = [copy(1 + j, me, (*chip, c), src=x_ref) for j, chip in enumerate(chips)]
    first.append(copy(0, me, sibling, src=x_ref))
    for cp in first:
        cp.start()
    passed = [copy(4 + j, (*chip, c), sibling) for j, chip in enumerate(chips)]
    for j, chip in enumerate(chips):
        copy(1 + j, (*chip, c), me).wait_recv()
        passed[j].start()
    copy(0, sibling, me).wait_recv()
    for j, chip in enumerate(chips):
        copy(4 + j, (*chip, 1 - c), me).wait_recv()
    for cp in first + passed:
        cp.wait_send()
    mine.wait()


def gather_small(x):
    return _call(
        _two_level_gather_body(), name="gather_small",
        in_specs=[pl.BlockSpec(memory_space=pltpu.VMEM)], out_specs=pl.BlockSpec(memory_space=pltpu.VMEM),
        out_shape=jax.ShapeDtypeStruct((N_DEV,) + x.shape, x.dtype),
        scratch=[pltpu.SemaphoreType.DMA((7,)), pltpu.SemaphoreType.DMA((7,)), pltpu.SemaphoreType.DMA(())],
    )(x)


def _two_level_gather_body():
    def body(x_ref, out_ref, send_sems, recv_sems, local_sem):
        _two_level_gather(x_ref, out_ref, send_sems, recv_sems, local_sem)
    return body


_HBM = pl.BlockSpec(memory_space=pltpu.HBM)
_SEM = pl.BlockSpec(memory_space=pltpu.SEMAPHORE)
_EFFECT = pltpu.SideEffectType.DATAFLOW_SIDE_EFFECTING
_RELATIONS = (4, 2, 6, 5, 3, 7, 1)


def _split_copy(src_ref, land_ref, send_sems, recv_sems, k, gather):
    x, y, c = _mesh_pos()
    px, py, pc = (1 - x if k & 4 else x), (1 - y if k & 2 else y), (1 - c if k & 1 else c)
    if gather:
        src, dst = src_ref, land_ref.at[4 * x + 2 * y + c]
    else:
        src, dst = src_ref.at[4 * px + 2 * py + pc], land_ref.at[k - 1]
    return pltpu.make_async_remote_copy(src_ref=src, dst_ref=dst, send_sem=send_sems.at[k - 1],
                                        recv_sem=recv_sems.at[k - 1], device_id=(px, py, pc), device_id_type=MESH)


def copy_start(src, land_shape, *, gather, name):
    def body(src_ref, land_ref, send_sems, recv_sems, src_thru, land_thru, token):
        for k in _RELATIONS:
            _split_copy(src_ref, land_ref, send_sems, recv_sems, k, gather).start()
        token[...] = jnp.zeros_like(token)

    return pl.pallas_call(
        body, name=name,
        out_shape=(pltpu.SemaphoreType.DMA((7,)), pltpu.SemaphoreType.DMA((7,)), pltpu.HBM(src.shape, src.dtype),
                   pltpu.HBM(land_shape, src.dtype), jax.ShapeDtypeStruct((8, 128), F32)),
        in_specs=(_HBM, _HBM), out_specs=(_SEM, _SEM, _HBM, _HBM, pl.BlockSpec(memory_space=pltpu.VMEM)),
        input_output_aliases={0: 2, 1: 3},
        compiler_params=pltpu.CompilerParams(has_side_effects=_EFFECT),
    )(pltpu.with_memory_space_constraint(src, pltpu.HBM),
      pltpu.with_memory_space_constraint(lax.empty(land_shape, src.dtype), pltpu.HBM))


def copy_wait(started, after, *, gather, name):
    send_sems, recv_sems, src_thru, land_thru, _ = started

    def body(src_ref, land_ref, send_sems, recv_sems, after_ref, src_dead, got_ref):
        for k in _RELATIONS:
            cp = _split_copy(src_ref, land_ref, send_sems, recv_sems, k, gather)
            cp.wait_send()
            cp.wait_recv()

    return pl.pallas_call(
        body, name=name,
        out_shape=(pltpu.HBM(src_thru.shape, src_thru.dtype), pltpu.HBM(land_thru.shape, land_thru.dtype)),
        in_specs=(_HBM, _HBM, _SEM, _SEM, pl.BlockSpec(memory_space=pl.ANY)), out_specs=(_HBM, _HBM),
        input_output_aliases={0: 0, 1: 1},
        compiler_params=pltpu.CompilerParams(has_side_effects=_EFFECT),
    )(src_thru, land_thru, send_sems, recv_sems, after)


WEIGHTS = ['ln1_g', 'ln1_b', 'ffn1_w_gate', 'ffn1_w_up', 'ffn1_w_down', 'mix_w_in', 'pool_w', 'pool_scale',
           'sconv_w', 'cconv_w', 'cconv_b', 'cnorm_g', 'cnorm_b', 'mix_w_out', 'ln2_g', 'ln2_b',
           'ffn2_w_gate', 'ffn2_w_up', 'ffn2_w_down', 'ln3_g', 'ln3_b']


def _pack_small(pieces):
    flat = jnp.concatenate([p.reshape(-1) for p in pieces])
    pad = -flat.shape[0] % 1024
    return jnp.pad(flat, (0, pad)).reshape(-1, 128)


def _unpack_small(flat, shapes):
    out, off = [], 0
    for sh in shapes:
        n = 1
        for s in sh:
            n *= s
        out.append(flat[off:off + n].reshape(sh))
        off += n
    return out


def _pad_rows(a, rows):
    return jnp.pad(a, ((0, rows - a.shape[0]), (0, 0)))


def kernel(x, ln1_g, ln1_b, ffn1_w_gate, ffn1_w_up, ffn1_w_down, mix_w_in, pool_w, pool_scale, sconv_w, cconv_w, cconv_b, cnorm_g, cnorm_b, mix_w_out, ln2_g, ln2_b, ffn2_w_gate, ffn2_w_up, ffn2_w_down, ln3_g, ln3_b, loss_target, m_ln1_g, m_ln1_b, m_ffn1_w_gate, m_ffn1_w_up, m_ffn1_w_down, m_mix_w_in, m_pool_w, m_pool_scale, m_sconv_w, m_cconv_w, m_cconv_b, m_cnorm_g, m_cnorm_b, m_mix_w_out, m_ln2_g, m_ln2_b, m_ffn2_w_gate, m_ffn2_w_up, m_ffn2_w_down, m_ln3_g, m_ln3_b, v_ln1_g, v_ln1_b, v_ffn1_w_gate, v_ffn1_w_up, v_ffn1_w_down, v_mix_w_in, v_pool_w, v_pool_scale, v_sconv_w, v_cconv_w, v_cconv_b, v_cnorm_g, v_cnorm_b, v_mix_w_out, v_ln2_g, v_ln2_b, v_ffn2_w_gate, v_ffn2_w_up, v_ffn2_w_down, v_ln3_g, v_ln3_b):
    a = dict(locals())
    depth, d = ln1_g.shape
    t = x.shape[1]
    fs = ffn1_w_gate.shape[2]
    f = fs * N_DEV
    ins = mix_w_in.shape[2]
    ncols = ins * N_DEV
    outs = mix_w_out.shape[1]
    p, s = d // 4, 3 * d // 8
    pg = p // 4
    cs = sconv_w.shape[2]
    alpha = (2.0 * depth) ** 0.25
    me = 4 * lax.axis_index("x") + 2 * lax.axis_index("y") + lax.axis_index("c")
    tm = min(512, t)
    tm_bwd = min(256, t)
    tm_tn = min(1024, t)
    tm_down = min(1024, t)
    tt_fwd = min(512, t)
    tt_bwd = min(512, t)

    sizes = {"wg1": fs, "wu1": fs, "wd1": fs, "win": ins, "wout": outs, "wg2": fs, "wu2": fs, "wd2": fs,
             "conv": 16}
    per_layer = ["wg1", "wu1", "wd1", "win", "wout", "wg2", "wu2", "wd2"]
    wire = jnp.dtype(F32).itemsize // jnp.dtype(MM).itemsize
    n_conv = (SCONV_K + CCONV_K) * cs

    def conv_rows(l):
        flat = jnp.concatenate([sconv_w[l].reshape(-1), cconv_w[l].reshape(-1)])
        bits = lax.bitcast_convert_type(flat, MM).reshape(-1)
        return jnp.pad(bits, (0, sizes["conv"] * d - bits.shape[0])).reshape(sizes["conv"], d)

    def layer_block(l, key):
        if key == "conv":
            return conv_rows(l)
        return {"wg1": ffn1_w_gate[l].T, "wu1": ffn1_w_up[l].T, "wd1": ffn1_w_down[l], "win": mix_w_in[l].T,
                "wout": mix_w_out[l], "wg2": ffn2_w_gate[l].T, "wu2": ffn2_w_up[l].T,
                "wd2": ffn2_w_down[l]}[key].astype(MM)

    def gather_groups(l):
        if l == 0:
            return [["wg1", "wu1"], ["wd1"], ["win", "wout", "wg2", "wu2", "wd2", "conv"]]
        return [per_layer + ["conv"]]

    gathers = {}
    for l in range(depth):
        for gi, keys in enumerate(gather_groups(l)):
            src = jnp.concatenate([layer_block(l, k) for k in keys], axis=0)
            gathers[l, gi] = copy_start(src, (N_DEV,) + src.shape, gather=True, name=f"gather_start_{l}_{gi}")
    started = sum(st[4][0:1, 0:1] for st in gathers.values())

    def gathered(l, gi, after):
        mine, land = copy_wait(gathers[l, gi], after, gather=True, name=f"gather_wait_{l}_{gi}")
        wall = lax.dynamic_update_slice(land, mine[None], (me, 0, 0))
        out, off = {}, 0
        for k in gather_groups(l)[gi]:
            out[k] = wall[:, off:off + sizes[k], :].reshape(N_DEV * sizes[k], d)
            off += sizes[k]
        return out

    def conv_filters(rows):
        bits = rows.reshape(N_DEV, -1)[:, :n_conv * wire]
        vals = lax.bitcast_convert_type(bits.reshape(N_DEV, n_conv, wire) if wire > 1 else bits, F32)
        both = vals.reshape(N_DEV, SCONV_K + CCONV_K, cs).transpose(1, 0, 2).reshape(SCONV_K + CCONV_K, s)
        return _pad_rows(both[:SCONV_K], 8), _pad_rows(both[SCONV_K:], 32)

    eye = jnp.eye(4, dtype=F32)
    wbd_all = (pool_w[:, :, :, None, :] * eye[None, :, None, :, None]).reshape(depth, p, p)

    def row(v):
        return v.reshape(1, -1)

    saved = []
    cur, gam, bet = x[0], jnp.ones((1, d), F32), jnp.zeros((1, d), F32) + started
    for l in range(depth):
        w = gathered(l, 0, started if l == 0 else cur)
        sv = {"w": w}
        sv["xb1"], sv["g1"], sv["u1"], sv["h1"] = ffn_up(cur, gam, bet, w["wg1"], w["wu1"], tm=tm)
        if l == 0:
            w.update(gathered(l, 1, sv["xb1"]))
        sv["xh1"], sv["rs1"] = mm_res_ln(sv["h1"], w["wd1"], cur, gam, bet, alpha=alpha, scale=0.5, tm=tm_down)
        if l == 0:
            w.update(gathered(l, 2, sv["xh1"]))
        g1, b1 = row(ln1_g[l]), row(ln1_b[l])
        sv["proj"], sv["xb2"] = mm_in(sv["xh1"], g1, b1, w["win"], tm=tm)
        sv["wbd"] = wbd_all[l].astype(MM)
        sv["sw"], sv["cw"] = conv_filters(w["conv"])
        sv["cat"] = mixer_fwd(sv["proj"], sv["wbd"], row(pool_scale[l]), sv["sw"], sv["cw"], row(cconv_b[l]),
                              row(cnorm_g[l]), row(cnorm_b[l]), d=d, tt=tt_fwd)
        sv["xh2"], sv["rs2"] = mm_res_ln(sv["cat"], w["wout"], sv["xh1"], g1, b1, alpha=alpha, scale=1.0, tm=tm_down)
        g2, b2 = row(ln2_g[l]), row(ln2_b[l])
        sv["xb3"], sv["g3"], sv["u3"], sv["h3"] = ffn_up(sv["xh2"], g2, b2, w["wg2"], w["wu2"], tm=tm)
        sv["xh3"], sv["rs3"] = mm_res_ln(sv["h3"], w["wd2"], sv["xh2"], g2, b2, alpha=alpha, scale=0.5, tm=tm_down)
        saved.append(sv)
        cur, gam, bet = sv["xh3"], row(ln3_g[l]), row(ln3_b[l])

    dcur, lsum = loss_head(cur, gam, bet, loss_target[0], tm=tm)
    loss = lax.psum(lsum[0, 0] * (0.5 / d), MESH_AXES)

    exchanges = []
    small = [None] * depth

    def exchange(l, keys, gws):
        rows = sum(sizes[k] for k in keys)
        pad = -rows % 256 if _row_tile(rows, 256) < 64 else 0
        parts = [gb.reshape(N_DEV, sizes[k], d) for k, (_, gb) in zip(keys, gws)]
        mine = [lax.dynamic_slice_in_dim(g, me * sizes[k], sizes[k], axis=0) for k, (g, _) in zip(keys, gws)]
        if pad:
            parts.append(jnp.zeros((N_DEV, pad, d), MM))
            mine.append(jnp.zeros((pad, d), F32))
        st = copy_start(jnp.concatenate(parts, axis=1), (N_DEV - 1, rows + pad, d), gather=False,
                        name=f"exchange_start_{l}_{keys[0]}")
        exchanges.append((l, keys, st, jnp.concatenate(mine, axis=0)))
        return st[4][0:1, 0:1]

    sent = jnp.zeros((1, 1), F32)
    for l in reversed(range(depth)):
        sv = saved[l]
        w = sv["w"]
        dx, dyb, dg, du, dg3, db3 = ffn_bwd(dcur, sv["xh3"], sv["rs3"], row(ln3_g[l]) + sent, w["wd2"], w["wg2"],
                                             w["wu2"], sv["g3"], sv["u3"], alpha=alpha, tm=tm_bwd)
        gw_g2, gw_u2 = tn_matmul(dg, sv["xb3"], tm=tm_tn), tn_matmul(du, sv["xb3"], tm=tm_tn)
        gw_d2 = tn_matmul(sv["h3"], dyb, tm=tm_tn)
        dz, dzb, dcat, dg2, db2 = lnbwd_mm(dx, sv["xh2"], sv["rs2"], row(ln2_g[l]), w["wout"], tm=tm)
        dproj, dwbd, dps, dsw, dcw, dcb, dcg, dcbt = mixer_bwd(
            sv["proj"], dcat, sv["wbd"], sv["wbd"].T, row(pool_scale[l]), sv["sw"], sv["cw"], row(cconv_b[l]),
            row(cnorm_g[l]), row(cnorm_b[l]), d=d, tt=tt_bwd)
        gw_out = tn_matmul(sv["cat"], dzb, tm=tm_tn)
        gw_in = tn_matmul(dproj, sv["xb2"], tm=tm_tn)
        if l == 0:
            sent = exchange(l, ["wg2", "wu2", "wd2", "wout", "win"], [gw_g2, gw_u2, gw_d2, gw_out, gw_in])
        dx = mm_add(dproj, w["win"], dz, alpha=alpha, tm=tm)
        dx, dyb, dg, du, dg1, db1 = ffn_bwd(dx, sv["xh1"], sv["rs1"], row(ln1_g[l]) + sent, w["wd1"], w["wg1"],
                                             w["wu1"], sv["g1"], sv["u1"], alpha=alpha, tm=tm_bwd)
        dcur = dx
        if l == 0:
            sent = exchange(l, ["wg1"], [tn_matmul(dg, sv["xb1"], tm=tm_tn)])
            sent = exchange(l, ["wu1"], [tn_matmul(du, sv["xb1"], tm=tm_tn, after=sent)])
            sent = exchange(l, ["wd1"], [tn_matmul(sv["h1"], dyb, tm=tm_tn, after=sent)])
        else:
            gw_g1, gw_u1 = tn_matmul(dg, sv["xb1"], tm=tm_tn), tn_matmul(du, sv["xb1"], tm=tm_tn)
            gw_d1 = tn_matmul(sv["h1"], dyb, tm=tm_tn)
            sent = exchange(l, per_layer, [gw_g1, gw_u1, gw_d1, gw_in, gw_out, gw_g2, gw_u2, gw_d2])
        dpw = jnp.stack([dwbd[g * pg:(g + 1) * pg, g * pg:(g + 1) * pg] for g in range(4)])
        small[l] = [dg1, db1, dg2, db2, dg3, db3, dpw, dps, dsw[:SCONV_K], dcw[:CCONV_K], dcb, dcg, dcbt]
    grad_x = dcur[None]

    by_key = {k: [None] * depth for k in per_layer}
    for l, keys, st, mine in exchanges:
        recv = copy_wait(st, dcur, gather=False, name=f"exchange_wait_{l}_{keys[0]}")[1]
        gsum, off = sum_parts(mine, recv), 0
        for k in keys:
            by_key[k][l] = gsum[off:off + sizes[k]]
            off += sizes[k]

    small_shapes = [g.shape for g in small[0]]
    small_flat = _pack_small([g for l in range(depth) for g in small[l]])
    small_sum = sum_gathered(gather_small(small_flat)).reshape(-1)
    small_g = _unpack_small(small_sum, small_shapes * depth)
    n_small = len(small_shapes)

    grads = {}
    grads["ffn1_w_gate"] = jnp.stack([g.T for g in by_key["wg1"]])
    grads["ffn1_w_up"] = jnp.stack([g.T for g in by_key["wu1"]])
    grads["ffn1_w_down"] = jnp.stack(by_key["wd1"])
    grads["mix_w_in"] = jnp.stack([g.T for g in by_key["win"]])
    grads["mix_w_out"] = jnp.stack(by_key["wout"])
    grads["ffn2_w_gate"] = jnp.stack([g.T for g in by_key["wg2"]])
    grads["ffn2_w_up"] = jnp.stack([g.T for g in by_key["wu2"]])
    grads["ffn2_w_down"] = jnp.stack(by_key["wd2"])
    small_names = ["ln1_g", "ln1_b", "ln2_g", "ln2_b", "ln3_g", "ln3_b", "pool_w", "pool_scale", "sconv_w",
                   "cconv_w", "cconv_b", "cnorm_g", "cnorm_b"]
    for idx, name in enumerate(small_names):
        full = jnp.stack([small_g[l * n_small + idx] for l in range(depth)])
        if name in ("sconv_w", "cconv_w"):
            full = lax.dynamic_slice_in_dim(full, me * cs, cs, axis=2)
        grads[name] = full.reshape(a[name].shape)

    deltas, new_m, new_v = {}, {}, {}
    for name in WEIGHTS:
        w = a[name]
        c = w.shape[-1]
        dl, mn, vn = adamw(w.reshape(-1, c), grads[name].reshape(-1, c), a["m_" + name].reshape(-1, c),
                           a["v_" + name].reshape(-1, c))
        deltas[name], new_m[name], new_v[name] = dl.reshape(w.shape), mn.reshape(w.shape), vn.reshape(w.shape)

    return (loss, grad_x, *[grads[n] for n in WEIGHTS], *[deltas[n] for n in WEIGHTS],
            *[new_m[n] for n in WEIGHTS], *[new_v[n] for n in WEIGHTS])
```

```python
import functools

import jax
import jax.numpy as jnp
from jax import lax
from jax.experimental import pallas as pl
from jax.experimental.pallas import tpu as pltpu

F32 = jnp.float32
MM = jnp.bfloat16
LN_EPS = 1e-5
N_DEV = 8
MESH_AXES = ("x", "y", "c")
HALO = 32
CCONV_K = 31
SCONV_K = 3
CONV_ROWS = 32
FFN_CHUNK = 256
VMEM_LIMIT = 56 * 1024 * 1024
ADAM_LR, ADAM_B1, ADAM_B2, ADAM_EPS, ADAM_WD, ADAM_STEP = 0.001, 0.9, 0.999, 1e-08, 0.01, 10
MESH = pl.DeviceIdType.MESH


def _call(body, *, name, out_shape, in_specs, out_specs, grid=None, scratch=(), sem=None):
    kw = {}
    if grid is not None:
        kw["grid"] = grid
    params = dict(vmem_limit_bytes=VMEM_LIMIT)
    if sem is not None:
        params["dimension_semantics"] = sem
    return pl.pallas_call(body, name=name, out_shape=out_shape, in_specs=in_specs, out_specs=out_specs,
                          scratch_shapes=list(scratch), compiler_params=pltpu.CompilerParams(**params), **kw)


def _rows(tm, n):
    return pl.BlockSpec((tm, n), lambda i: (i, 0))


def _const(shape):
    nd = len(shape)
    return pl.BlockSpec(shape, lambda *_: (0,) * nd, pipeline_mode=pl.Buffered(1))


def _acc(shape):
    nd = len(shape)
    return pl.BlockSpec(shape, lambda *_: (0,) * nd)


def _row_tile(rows, cap):
    best = None
    for t in range(8, min(rows, cap) + 1, 8):
        if rows % t == 0:
            best = t
    return best if best is not None else rows


def _sig(x):
    return 1.0 / (1.0 + jnp.exp(-x))


def _ln_stats(z):
    mu = jnp.mean(z, axis=-1, keepdims=True)
    zc = z - mu
    var = jnp.mean(zc * zc, axis=-1, keepdims=True)
    rstd = lax.rsqrt(var + LN_EPS)
    return zc * rstd, rstd


def _ln_bwd(dout, xhat, rstd, gamma):
    dxh = dout * gamma
    m1 = jnp.mean(dxh, axis=-1, keepdims=True)
    m2 = jnp.mean(dxh * xhat, axis=-1, keepdims=True)
    return rstd * (dxh - m1 - xhat * m2)


def _colsum(v):
    return jnp.sum(v, axis=0, keepdims=True)


def _f_chunks(f, width=None):
    if width is not None and f % width == 0:
        return f // width, width
    n = 2 if f >= 2048 and f % 256 == 0 else 1
    return n, f // n


def _dot_nt(a, b):
    return lax.dot_general(a, b, (((1,), (1,)), ((), ())), preferred_element_type=F32)


def ffn_up(xin, gam, bet, wgt, wut, *, tm):
    t, d = xin.shape
    f = wgt.shape[0]
    nch, fc = _f_chunks(f, FFN_CHUNK)

    def body(x_ref, g_ref, b_ref, wg_ref, wu_ref, xb_ref, go_ref, uo_ref, h_ref):
        xb = (x_ref[...] * g_ref[...] + b_ref[...]).astype(MM)
        xb_ref[...] = xb
        for c in range(nch):
            sl = slice(c * fc, (c + 1) * fc)
            g = _dot_nt(xb, wg_ref[sl, :])
            u = _dot_nt(xb, wu_ref[sl, :])
            go_ref[:, sl] = g.astype(MM)
            uo_ref[:, sl] = u.astype(MM)
            h_ref[:, sl] = (g * _sig(g) * u).astype(MM)

    return _call(
        body, name="ffn_up", grid=(t // tm,), sem=("parallel",),
        in_specs=[_rows(tm, d), _const((1, d)), _const((1, d)), _const((f, d)), _const((f, d))],
        out_specs=[_rows(tm, d), _rows(tm, f), _rows(tm, f), _rows(tm, f)],
        out_shape=[jax.ShapeDtypeStruct((t, d), MM)] + [jax.ShapeDtypeStruct((t, f), MM)] * 3,
    )(xin, gam, bet, wgt, wut)


def mm_res_ln(a, w, xin, gam, bet, *, alpha, scale, tm):
    t, k = a.shape
    d = w.shape[1]

    def body(a_ref, w_ref, x_ref, g_ref, b_ref, xh_ref, rs_ref):
        y = jnp.dot(a_ref[...], w_ref[...], preferred_element_type=F32)
        x = x_ref[...] * g_ref[...] + b_ref[...]
        xh, rstd = _ln_stats(alpha * x + scale * y)
        xh_ref[...] = xh
        rs_ref[...] = rstd

    return _call(
        body, name="mm_res_ln", grid=(t // tm,), sem=("parallel",),
        in_specs=[_rows(tm, k), _const((k, d)), _rows(tm, d), _const((1, d)), _const((1, d))],
        out_specs=[_rows(tm, d), _rows(tm, 1)],
        out_shape=[jax.ShapeDtypeStruct((t, d), F32), jax.ShapeDtypeStruct((t, 1), F32)],
    )(a, w, xin, gam, bet)


def mm_in(xin, gam, bet, wt, *, tm):
    t, d = xin.shape
    n = wt.shape[0]

    def body(x_ref, g_ref, b_ref, w_ref, o_ref, xb_ref):
        xb = (x_ref[...] * g_ref[...] + b_ref[...]).astype(MM)
        xb_ref[...] = xb
        o_ref[...] = _dot_nt(xb, w_ref[...])

    return _call(
        body, name="mm_in", grid=(t // tm,), sem=("parallel",),
        in_specs=[_rows(tm, d), _const((1, d)), _const((1, d)), _const((n, d))],
        out_specs=[_rows(tm, n), _rows(tm, d)],
        out_shape=[jax.ShapeDtypeStruct((t, n), F32), jax.ShapeDtypeStruct((t, d), MM)],
    )(xin, gam, bet, wt)


def loss_head(xh, gam, bet, target, *, tm):
    t, d = xh.shape

    def body(x_ref, g_ref, b_ref, t_ref, dy_ref, l_ref):
        @pl.when(pl.program_id(0) == 0)
        def _():
            l_ref[...] = jnp.zeros_like(l_ref)

        e = x_ref[...] * g_ref[...] + b_ref[...] - t_ref[...]
        dy_ref[...] = e * (1.0 / d)
        l_ref[...] += jnp.sum(_colsum(e * e), axis=1, keepdims=True)

    return _call(
        body, name="loss_head", grid=(t // tm,), sem=("arbitrary",),
        in_specs=[_rows(tm, d), _const((1, d)), _const((1, d)), _rows(tm, d)],
        out_specs=[_rows(tm, d), _acc((1, 1))],
        out_shape=[jax.ShapeDtypeStruct((t, d), F32), jax.ShapeDtypeStruct((1, 1), F32)],
    )(xh, gam, bet, target)


def _halo_specs(tt, ncols, t):
    per, last = tt // HALO, t // HALO - 1
    return [pl.BlockSpec((HALO, ncols), lambda i: (jnp.maximum(i * per - 1, 0), 0)),
            pl.BlockSpec((tt, ncols), lambda i: (i, 0)),
            pl.BlockSpec((HALO, ncols), lambda i: (jnp.minimum((i + 1) * per, last), 0))]


def _fill_ext(ext, prev_ref, cur_ref, next_ref, i, nt, tt):
    ext[0:HALO, :] = jnp.where(i > 0, prev_ref[...], 0.0)
    ext[HALO:HALO + tt, :] = cur_ref[...]
    ext[HALO + tt:HALO + tt + HALO, :] = jnp.where(i < nt - 1, next_ref[...], 0.0)


def _make_shifts(sh, n):
    for r in range(1, 8):
        sh[r, 0:n, :] = sh[0, r:r + n, :]


def _shift_reader(sh):
    def read(o, rows):
        r = o % 8
        return sh[r, o - r:o - r + rows, :]
    return read


def _conv_chunks_shifted(sh, w_ref, ktaps, src0, nrows, flip=False):
    for r0 in range(0, nrows, CONV_ROWS):
        acc = None
        base = src0 + r0
        for r in range(8):
            ks = [k for k in range(ktaps) if (base + k) % 8 == r]
            if not ks:
                continue
            lo = base + ks[0] - r
            slab = sh[r, lo:lo + CONV_ROWS + 8 * (len(ks) - 1), :]
            for q, k in enumerate(ks):
                kk = ktaps - 1 - k if flip else k
                term = w_ref[kk:kk + 1, :] * slab[8 * q:8 * q + CONV_ROWS, :]
                acc = term if acc is None else acc + term
        yield r0, acc


def _ref_reader(ref):
    def read(o, rows):
        return ref[o:o + rows, :]
    return read


def _conv_chunks(read, w_ref, ktaps, src0, nrows, flip=False):
    for r0 in range(0, nrows, CONV_ROWS):
        acc = None
        for k in range(ktaps):
            kk = ktaps - 1 - k if flip else k
            term = w_ref[kk:kk + 1, :] * read(src0 + r0 + k, CONV_ROWS)
            acc = term if acc is None else acc + term
        yield r0, acc


def _pool_groups(nrows, p):
    lane = lax.broadcasted_iota(jnp.int32, (nrows, p), 1)
    g = p // 4
    return lane < g, lane < 2 * g, lane < 3 * g


def _pool_select(groups, v2, v4, v8, v16):
    g0, g1, g2 = groups
    return jnp.where(g0, v2, jnp.where(g1, v4, jnp.where(g2, v8, v16)))


def _pool_count(groups, i, tt, row0, nrows, p, t):
    pos = i * tt + (row0 - HALO) + lax.broadcasted_iota(jnp.int32, (nrows, p), 0)
    half = _pool_select(groups, 1, 2, 4, 8)
    lo = jnp.clip(pos - half, 0, t)
    hi = jnp.clip(pos + half, 0, t)
    return jnp.maximum(hi - lo, 1).astype(F32)


def _pool_forward(ext, s2, s4, s8, groups, cnt, tt, p):
    e = tt + 2 * HALO
    s2[8:e - 8, :] = ext[7:e - 9, 0:p] + ext[8:e - 8, 0:p]
    s4[16:e - 16, :] = s2[15:e - 17, :] + s2[17:e - 15, :]
    s8[24:e - 24, :] = s4[22:e - 26, :] + s4[26:e - 22, :]
    s16 = s8[28:e - 36, :] + s8[36:e - 28, :]
    c = slice(HALO, HALO + tt)
    tot = _pool_select(groups, s2[c, :], s4[c, :], s8[c, :], s16)
    return tot / cnt - ext[c, 0:p]


def mixer_fwd(proj, wbd, pscale, sw, cw, cb, cg, cbt, *, d, tt):
    t, ncols = proj.shape
    p, s = d // 4, 3 * d // 8
    o_gb, o_gc, o_v, o_cv, o_cg = p, p + s, p + 2 * s, p + 3 * s, p + 4 * s
    nt, e = t // tt, tt + 2 * HALO

    def body(prev_ref, cur_ref, next_ref, wbd_ref, ps_ref, sw_ref, cw_ref, cb_ref, cg_ref, cbt_ref,
             cat_ref, conv_ref, ext, a_sh, cv_s, s2, s4, s8):
        i = pl.program_id(0)
        _fill_ext(ext, prev_ref, cur_ref, next_ref, i, nt, tt)
        a_sh[0, 0:e, :] = ext[:, o_cv:o_cv + s] * _sig(ext[:, o_cg:o_cg + s])
        a_sh[0, e:e + 8, :] = jnp.zeros((8, s), F32)
        _make_shifts(a_sh, e)
        for r0, acc in _conv_chunks_shifted(a_sh, cw_ref, CCONV_K, HALO - CCONV_K // 2, tt):
            b = acc + cb_ref[...]
            conv_ref[r0:r0 + CONV_ROWS, :] = b
            n, _ = _ln_stats(b)
            yn = n * cg_ref[...] + cbt_ref[...]
            cat_ref[r0:r0 + CONV_ROWS, p + s:d] = (yn * _sig(yn)).astype(MM)
        cv_s[...] = ext[:, o_gc:o_gc + s] * ext[:, o_v:o_v + s]
        for r0, acc in _conv_chunks(_ref_reader(cv_s), sw_ref, SCONV_K, HALO - SCONV_K // 2, tt):
            gb = ext[HALO + r0:HALO + r0 + CONV_ROWS, o_gb:o_gb + s]
            cat_ref[r0:r0 + CONV_ROWS, p:p + s] = (gb * acc).astype(MM)
        groups = _pool_groups(tt, p)
        cnt = _pool_count(groups, i, tt, HALO, tt, p, t)
        pooled = _pool_forward(ext, s2, s4, s8, groups, cnt, tt, p)
        ya = jnp.dot(pooled.astype(MM), wbd_ref[...], preferred_element_type=F32) * ps_ref[...]
        cat_ref[:, 0:p] = ya.astype(MM)

    return _call(
        body, name="mixer_fwd", grid=(nt,), sem=("parallel",),
        in_specs=_halo_specs(tt, ncols, t) + [_const((p, p)), _const((1, p)), _const((8, s)), _const((32, s)),
                                               _const((1, s)), _const((1, s)), _const((1, s))],
        out_specs=[_rows(tt, d), _rows(tt, s)],
        out_shape=[jax.ShapeDtypeStruct((t, d), MM), jax.ShapeDtypeStruct((t, s), F32)],
        scratch=[pltpu.VMEM((e, ncols), F32), pltpu.VMEM((8, e + 8, s), F32), pltpu.VMEM((e, s), F32),
                 pltpu.VMEM((e, p), F32), pltpu.VMEM((e, p), F32), pltpu.VMEM((e, p), F32)],
    )(proj, proj, proj, wbd, pscale, sw, cw, cb, cg, cbt)


def mixer_bwd(proj, dcat, conv, wbd, wbdt, pscale, sw, cw, cg, cbt, *, d, tt):
    t, ncols = proj.shape
    p, s = d // 4, 3 * d // 8
    o_gb, o_gc, o_v, o_cv, o_cg = p, p + s, p + 2 * s, p + 3 * s, p + 4 * s
    nt, e = t // tt, tt + 2 * HALO

    def body(pp_ref, pc_ref, pn_ref, dp_ref, dc_ref, dn_ref, bp_ref, bc_ref, bn_ref, wbd_ref, wbdt_ref, ps_ref,
             sw_ref, cw_ref, cg_ref, cbt_ref,
             dproj_ref, dwbd_ref, dps_ref, dsw_ref, dcw_ref, dcb_ref, dcg_ref, dcbt_ref,
             ext, dext, bext, a_sh, b_sh, sg_s, cv_s, ds_s, q_s, r2, r4, r8):
        i = pl.program_id(0)

        @pl.when(i == 0)
        def _():
            for ref in (dwbd_ref, dps_ref, dsw_ref, dcw_ref, dcb_ref, dcg_ref, dcbt_ref):
                ref[...] = jnp.zeros_like(ref)

        _fill_ext(ext, pp_ref, pc_ref, pn_ref, i, nt, tt)
        _fill_ext(dext, dp_ref, dc_ref, dn_ref, i, nt, tt)
        _fill_ext(bext, bp_ref, bc_ref, bn_ref, i, nt, tt)
        c = slice(HALO, HALO + tt)

        sg_s[...] = _sig(ext[:, o_cg:o_cg + s])
        a_sh[0, 0:e, :] = ext[:, o_cv:o_cv + s] * sg_s[...]
        a_sh[0, e:e + 8, :] = jnp.zeros((8, s), F32)
        _make_shifts(a_sh, e)
        read_a = _shift_reader(a_sh)
        for r0 in range(0, e, CONV_ROWS):
            b = bext[r0:r0 + CONV_ROWS, :]
            n, rstd = _ln_stats(b)
            yn = n * cg_ref[...] + cbt_ref[...]
            sy = _sig(yn)
            rows = slice(r0, r0 + CONV_ROWS)
            dyn = dext[rows, p + s:d] * (sy * (1.0 + yn * (1.0 - sy)))
            db = _ln_bwd(dyn, n, rstd, cg_ref[...])
            b_sh[0, rows, :] = db
            if HALO <= r0 < HALO + tt:
                dcg_ref[...] += _colsum(dyn * n)
                dcbt_ref[...] += _colsum(dyn)
                dcb_ref[...] += _colsum(db)
        b_sh[0, e:e + 8, :] = jnp.zeros((8, s), F32)
        _make_shifts(b_sh, e)
        for r0, da in _conv_chunks_shifted(b_sh, cw_ref, CCONV_K, HALO - CCONV_K // 2, tt, flip=True):
            rows = slice(HALO + r0, HALO + r0 + CONV_ROWS)
            sg = sg_s[rows, :]
            dproj_ref[r0:r0 + CONV_ROWS, o_cv:o_cv + s] = (da * sg).astype(MM)
            dproj_ref[r0:r0 + CONV_ROWS, o_cg:o_cg + s] = (
                da * ext[rows, o_cv:o_cv + s] * sg * (1.0 - sg)).astype(MM)
        for k in range(CCONV_K):
            lo = HALO + k - CCONV_K // 2
            dcw_ref[k:k + 1, :] += _colsum(b_sh[0, c, :] * read_a(lo, tt))

        cv_s[...] = ext[:, o_gc:o_gc + s] * ext[:, o_v:o_v + s]
        ds_s[...] = dext[:, p:p + s] * ext[:, o_gb:o_gb + s]
        for r0, acc in _conv_chunks(_ref_reader(cv_s), sw_ref, SCONV_K, HALO - SCONV_K // 2, tt):
            rows = slice(HALO + r0, HALO + r0 + CONV_ROWS)
            dproj_ref[r0:r0 + CONV_ROWS, o_gb:o_gb + s] = (dext[rows, p:p + s] * acc).astype(MM)
        for r0, dcv in _conv_chunks(_ref_reader(ds_s), sw_ref, SCONV_K, HALO - SCONV_K // 2, tt, flip=True):
            rows = slice(HALO + r0, HALO + r0 + CONV_ROWS)
            dproj_ref[r0:r0 + CONV_ROWS, o_gc:o_gc + s] = (dcv * ext[rows, o_v:o_v + s]).astype(MM)
            dproj_ref[r0:r0 + CONV_ROWS, o_v:o_v + s] = (dcv * ext[rows, o_gc:o_gc + s]).astype(MM)
        for k in range(SCONV_K):
            lo = HALO + k - SCONV_K // 2
            dsw_ref[k:k + 1, :] += _colsum(ds_s[c, :] * cv_s[lo:lo + tt, :])

        groups = _pool_groups(tt, p)
        cnt = _pool_count(groups, i, tt, HALO, tt, p, t)
        pooled = _pool_forward(ext, r2, r4, r8, groups, cnt, tt, p).astype(MM)
        ta = jnp.dot(pooled, wbd_ref[...], preferred_element_type=F32)
        dps_ref[...] += _colsum(dext[c, 0:p] * ta)
        dta = (dext[:, 0:p] * ps_ref[...]).astype(MM)
        dwbd_ref[...] += lax.dot_general(pooled, dta[HALO:HALO + tt, :], (((0,), (0,)), ((), ())),
                                         preferred_element_type=F32)
        dpool = jnp.dot(dta, wbdt_ref[...], preferred_element_type=F32)
        groups_e = _pool_groups(e, p)
        q_s[...] = dpool / _pool_count(groups_e, i, tt, 0, e, p, t)
        r2[8:e - 8, :] = q_s[8:e - 8, :] + q_s[9:e - 7, :]
        r4[16:e - 16, :] = r2[15:e - 17, :] + r2[17:e - 15, :]
        r8[24:e - 24, :] = r4[22:e - 26, :] + r4[26:e - 22, :]
        r16 = r8[28:e - 36, :] + r8[36:e - 28, :]
        du = _pool_select(groups, r2[c, :], r4[c, :], r8[c, :], r16) - dpool[HALO:HALO + tt, :]
        dproj_ref[:, 0:p] = du.astype(MM)

    small = [(p, p), (1, p), (8, s), (32, s), (1, s), (1, s), (1, s)]
    return _call(
        body, name="mixer_bwd", grid=(nt,), sem=("arbitrary",),
        in_specs=_halo_specs(tt, ncols, t) + _halo_specs(tt, d, t) + _halo_specs(tt, s, t) + [
            _const((p, p)), _const((p, p)), _const((1, p)), _const((8, s)), _const((32, s)),
            _const((1, s)), _const((1, s))],
        out_specs=[_rows(tt, ncols)] + [_acc(sh) for sh in small],
        out_shape=[jax.ShapeDtypeStruct((t, ncols), MM)] + [jax.ShapeDtypeStruct(sh, F32) for sh in small],
        scratch=[pltpu.VMEM((e, ncols), F32), pltpu.VMEM((e, d), F32), pltpu.VMEM((e, s), F32),
                 pltpu.VMEM((8, e + 8, s), F32), pltpu.VMEM((8, e + 8, s), F32)]
        + [pltpu.VMEM((e, s), F32)] * 3 + [pltpu.VMEM((e, p), F32)] * 4,
    )(proj, proj, proj, dcat, dcat, dcat, conv, conv, conv, wbd, wbdt, pscale, sw, cw, cg, cbt)


def ffn_bwd(dout, xh, rs, gam, wd, wgt, wut, g, u, *, alpha, tm):
    t, d = dout.shape
    f = g.shape[1]
    nch, fc = _f_chunks(f, FFN_CHUNK)

    def body(do_ref, xh_ref, rs_ref, gm_ref, wd_ref, wgt_ref, wut_ref, g_ref, u_ref,
             dx_ref, dyb_ref, dg_ref, du_ref, dgam_ref, dbet_ref):
        @pl.when(pl.program_id(0) == 0)
        def _():
            dgam_ref[...] = jnp.zeros_like(dgam_ref)
            dbet_ref[...] = jnp.zeros_like(dbet_ref)

        dout_v, xhat = do_ref[...], xh_ref[...]
        dgam_ref[...] += _colsum(dout_v * xhat)
        dbet_ref[...] += _colsum(dout_v)
        dz = _ln_bwd(dout_v, xhat, rs_ref[...], gm_ref[...])
        dyb = (0.5 * dz).astype(MM)
        dyb_ref[...] = dyb
        for c in range(nch):
            sl = slice(c * fc, (c + 1) * fc)
            dh = _dot_nt(dyb, wd_ref[sl, :])
            gv, uv = g_ref[:, sl].astype(F32), u_ref[:, sl].astype(F32)
            sg = _sig(gv)
            dg_ref[:, sl] = (dh * uv * (sg * (1.0 + gv * (1.0 - sg)))).astype(MM)
            du_ref[:, sl] = (dh * (gv * sg)).astype(MM)
        dx_ref[...] = (alpha * dz + jnp.dot(dg_ref[...], wgt_ref[...], preferred_element_type=F32)
                       + jnp.dot(du_ref[...], wut_ref[...], preferred_element_type=F32))

    return _call(
        body, name="ffn_bwd", grid=(t // tm,), sem=("arbitrary",),
        in_specs=[_rows(tm, d), _rows(tm, d), _rows(tm, 1), _const((1, d)), _const((f, d)), _const((f, d)),
                  _const((f, d)), _rows(tm, f), _rows(tm, f)],
        out_specs=[_rows(tm, d), _rows(tm, d), _rows(tm, f), _rows(tm, f), _acc((1, d)), _acc((1, d))],
        out_shape=[jax.ShapeDtypeStruct((t, d), F32), jax.ShapeDtypeStruct((t, d), MM),
                   jax.ShapeDtypeStruct((t, f), MM), jax.ShapeDtypeStruct((t, f), MM),
                   jax.ShapeDtypeStruct((1, d), F32), jax.ShapeDtypeStruct((1, d), F32)],
    )(dout, xh, rs, gam, wd, wgt, wut, g, u)


def lnbwd_mm(dout, xh, rs, gam, w, *, tm):
    t, d = dout.shape
    n = w.shape[0]

    def body(do_ref, xh_ref, rs_ref, gm_ref, wt_ref, dz_ref, dzb_ref, da_ref, dgam_ref, dbet_ref):
        @pl.when(pl.program_id(0) == 0)
        def _():
            dgam_ref[...] = jnp.zeros_like(dgam_ref)
            dbet_ref[...] = jnp.zeros_like(dbet_ref)

        dout_v, xhat = do_ref[...], xh_ref[...]
        dgam_ref[...] += _colsum(dout_v * xhat)
        dbet_ref[...] += _colsum(dout_v)
        dz = _ln_bwd(dout_v, xhat, rs_ref[...], gm_ref[...])
        dz_ref[...] = dz
        dzb = dz.astype(MM)
        dzb_ref[...] = dzb
        da_ref[...] = _dot_nt(dzb, wt_ref[...])

    return _call(
        body, name="lnbwd_mm", grid=(t // tm,), sem=("arbitrary",),
        in_specs=[_rows(tm, d), _rows(tm, d), _rows(tm, 1), _const((1, d)), _const((n, d))],
        out_specs=[_rows(tm, d), _rows(tm, d), _rows(tm, n), _acc((1, d)), _acc((1, d))],
        out_shape=[jax.ShapeDtypeStruct((t, d), F32), jax.ShapeDtypeStruct((t, d), MM),
                   jax.ShapeDtypeStruct((t, n), F32), jax.ShapeDtypeStruct((1, d), F32),
                   jax.ShapeDtypeStruct((1, d), F32)],
    )(dout, xh, rs, gam, w)


def mm_add(a, w, r, *, alpha, tm):
    t, k = a.shape
    d = w.shape[1]

    def body(a_ref, w_ref, r_ref, o_ref):
        o_ref[...] = jnp.dot(a_ref[...], w_ref[...], preferred_element_type=F32) + alpha * r_ref[...]

    return _call(
        body, name="mm_add", grid=(t // tm,), sem=("parallel",),
        in_specs=[_rows(tm, k), _const((k, d)), _rows(tm, d)],
        out_specs=_rows(tm, d),
        out_shape=jax.ShapeDtypeStruct((t, d), F32),
    )(a, w, r)


def tn_matmul(a, b, *, tm, after=None):
    t, n = a.shape
    d = b.shape[1]
    nch, nc = _f_chunks(n)
    steps = t // tm

    def body(a_ref, b_ref, *rest):
        o_ref, ob_ref = rest[-2:]

        @pl.when(pl.program_id(1) == 0)
        def _():
            o_ref[...] = jnp.zeros_like(o_ref)

        o_ref[...] += lax.dot_general(a_ref[...], b_ref[...], (((0,), (0,)), ((), ())),
                                      preferred_element_type=F32)

        @pl.when(pl.program_id(1) == steps - 1)
        def _():
            ob_ref[...] = o_ref[...].astype(MM)

    extra = [] if after is None else [after]
    return _call(
        body, name="tn_matmul", grid=(nch, steps), sem=("parallel", "arbitrary"),
        in_specs=[pl.BlockSpec((tm, nc), lambda j, i: (i, j)), pl.BlockSpec((tm, d), lambda j, i: (i, 0))]
        + [pl.BlockSpec(memory_space=pl.ANY)] * len(extra),
        out_specs=[pl.BlockSpec((nc, d), lambda j, i: (j, 0))] * 2,
        out_shape=[jax.ShapeDtypeStruct((n, d), F32), jax.ShapeDtypeStruct((n, d), MM)],
    )(a, b, *extra)


def sum_parts(own, recv):
    r, d = own.shape
    n = recv.shape[0]
    tr = _row_tile(r, 256)

    def body(own_ref, recv_ref, o_ref):
        acc = own_ref[...]
        for k in range(n):
            acc = acc + recv_ref[k].astype(F32)
        o_ref[...] = acc

    return _call(
        body, name="sum_parts", grid=(r // tr,), sem=("parallel",),
        in_specs=[_rows(tr, d), pl.BlockSpec((n, tr, d), lambda i: (0, i, 0))],
        out_specs=_rows(tr, d),
        out_shape=jax.ShapeDtypeStruct((r, d), F32),
    )(own, recv)


def sum_gathered(parts):
    _, r, n = parts.shape

    def body(p_ref, o_ref):
        acc = p_ref[0]
        for j in range(1, N_DEV):
            acc = acc + p_ref[j]
        o_ref[...] = acc

    return _call(
        body, name="sum_gathered",
        in_specs=[pl.BlockSpec(memory_space=pltpu.VMEM)], out_specs=pl.BlockSpec(memory_space=pltpu.VMEM),
        out_shape=jax.ShapeDtypeStruct((r, n), F32),
    )(parts)


def adamw(w, g, m, v):
    r, c = w.shape
    tr = _row_tile(r, 512)

    def body(w_ref, g_ref, m_ref, v_ref, d_ref, mo_ref, vo_ref):
        gv = g_ref[...]
        mn = ADAM_B1 * m_ref[...] + (1.0 - ADAM_B1) * gv
        vn = ADAM_B2 * v_ref[...] + (1.0 - ADAM_B2) * (gv * gv)
        m_hat = mn / (1.0 - ADAM_B1 ** ADAM_STEP)
        v_hat = vn / (1.0 - ADAM_B2 ** ADAM_STEP)
        d_ref[...] = -ADAM_LR * (m_hat / (jnp.sqrt(v_hat) + ADAM_EPS) + ADAM_WD * w_ref[...])
        mo_ref[...] = mn
        vo_ref[...] = vn

    return _call(
        body, name="adamw", grid=(r // tr,), sem=("parallel",),
        in_specs=[_rows(tr, c)] * 4, out_specs=[_rows(tr, c)] * 3,
        out_shape=[jax.ShapeDtypeStruct((r, c), F32)] * 3,
    )(w, g, m, v)


def _mesh_pos():
    return lax.axis_index("x"), lax.axis_index("y"), lax.axis_index("c")


def _two_level_gather(x_ref, out_ref, send_sems, recv_sems, local_sem):
    x, y, c = _mesh_pos()
    me, sibling = (x, y, c), (x, y, 1 - c)
    chips = [(1 - x, y), (x, 1 - y), (1 - x, 1 - y)]

    def slot(px, py, pc):
        return out_ref.at[4 * px + 2 * py + pc]

    def copy(k, block, to, src=None):
        return pltpu.make_async_remote_copy(
            src_ref=slot(*block) if src is None else src, dst_ref=slot(*block),
            send_sem=send_sems.at[k], recv_sem=recv_sems.at[k], device_id=to, device_id_type=MESH)

    mine = pltpu.make_async_copy(x_ref, slot(*me), local_sem)
    mine.start()
    first = [copy(1 + j, me, (*chip, c), src=x_ref) for j, chip in enumerate(chips)]
    first.append(copy(0, me, sibling, src=x_ref))
    for cp in first:
        cp.start()
    passed = [copy(4 + j, (*chip, c), sibling) for j, chip in enumerate(chips)]
    for j, chip in enumerate(chips):
        copy(1 + j, (*chip, c), me).wait_recv()
        passed[j].start()
    copy(0, sibling, me).wait_recv()
    for j, chip in enumerate(chips):
        copy(4 + j, (*chip, 1 - c), me).wait_recv()
    for cp in first + passed:
        cp.wait_send()
    mine.wait()


def gather_small(x):
    return _call(
        _two_level_gather_body(), name="gather_small",
        in_specs=[pl.BlockSpec(memory_space=pltpu.VMEM)], out_specs=pl.BlockSpec(memory_space=pltpu.VMEM),
        out_shape=jax.ShapeDtypeStruct((N_DEV,) + x.shape, x.dtype),
        scratch=[pltpu.SemaphoreType.DMA((7,)), pltpu.SemaphoreType.DMA((7,)), pltpu.SemaphoreType.DMA(())],
    )(x)


def _two_level_gather_body():
    def body(x_ref, out_ref, send_sems, recv_sems, local_sem):
        _two_level_gather(x_ref, out_ref, send_sems, recv_sems, local_sem)
    return body


_HBM = pl.BlockSpec(memory_space=pltpu.HBM)
_SEM = pl.BlockSpec(memory_space=pltpu.SEMAPHORE)
_EFFECT = pltpu.SideEffectType.DATAFLOW_SIDE_EFFECTING
_RELATIONS = (4, 2, 6, 5, 3, 7, 1)


def _split_copy(src_ref, land_ref, send_sems, recv_sems, k, gather):
    x, y, c = _mesh_pos()
    px, py, pc = (1 - x if k & 4 else x), (1 - y if k & 2 else y), (1 - c if k & 1 else c)
    if gather:
        src, dst = src_ref, land_ref.at[4 * x + 2 * y + c]
    else:
        src, dst = src_ref.at[4 * px + 2 * py + pc], land_ref.at[k - 1]
    return pltpu.make_async_remote_copy(src_ref=src, dst_ref=dst, send_sem=send_sems.at[k - 1],
                                        recv_sem=recv_sems.at[k - 1], device_id=(px, py, pc), device_id_type=MESH)


def copy_start(src, land_shape, *, gather, name):
    def body(src_ref, land_ref, send_sems, recv_sems, src_thru, land_thru, token):
        for k in _RELATIONS:
            _split_copy(src_ref, land_ref, send_sems, recv_sems, k, gather).start()
        token[...] = jnp.zeros_like(token)

    return pl.pallas_call(
        body, name=name,
        out_shape=(pltpu.SemaphoreType.DMA((7,)), pltpu.SemaphoreType.DMA((7,)), pltpu.HBM(src.shape, src.dtype),
                   pltpu.HBM(land_shape, src.dtype), jax.ShapeDtypeStruct((8, 128), F32)),
        in_specs=(_HBM, _HBM), out_specs=(_SEM, _SEM, _HBM, _HBM, pl.BlockSpec(memory_space=pltpu.VMEM)),
        input_output_aliases={0: 2, 1: 3},
        compiler_params=pltpu.CompilerParams(has_side_effects=_EFFECT),
    )(pltpu.with_memory_space_constraint(src, pltpu.HBM),
      pltpu.with_memory_space_constraint(lax.empty(land_shape, src.dtype), pltpu.HBM))


def copy_wait(started, after, *, gather, name):
    send_sems, recv_sems, src_thru, land_thru, _ = started

    def body(src_ref, land_ref, send_sems, recv_sems, after_ref, src_dead, got_ref):
        for k in _RELATIONS:
            cp = _split_copy(src_ref, land_ref, send_sems, recv_sems, k, gather)
            cp.wait_send()
            cp.wait_recv()

    return pl.pallas_call(
        body, name=name,
        out_shape=(pltpu.HBM(src_thru.shape, src_thru.dtype), pltpu.HBM(land_thru.shape, land_thru.dtype)),
        in_specs=(_HBM, _HBM, _SEM, _SEM, pl.BlockSpec(memory_space=pl.ANY)), out_specs=(_HBM, _HBM),
        input_output_aliases={0: 0, 1: 1},
        compiler_params=pltpu.CompilerParams(has_side_effects=_EFFECT),
    )(src_thru, land_thru, send_sems, recv_sems, after)


WEIGHTS = ['ln1_g', 'ln1_b', 'ffn1_w_gate', 'ffn1_w_up', 'ffn1_w_down', 'mix_w_in', 'pool_w', 'pool_scale',
           'sconv_w', 'cconv_w', 'cconv_b', 'cnorm_g', 'cnorm_b', 'mix_w_out', 'ln2_g', 'ln2_b',
           'ffn2_w_gate', 'ffn2_w_up', 'ffn2_w_down', 'ln3_g', 'ln3_b']


def _pack_small(pieces):
    flat = jnp.concatenate([p.reshape(-1) for p in pieces])
    pad = -flat.shape[0] % 1024
    return jnp.pad(flat, (0, pad)).reshape(-1, 128)


def _unpack_small(flat, shapes):
    out, off = [], 0
    for sh in shapes:
        n = 1
        for s in sh:
            n *= s
        out.append(flat[off:off + n].reshape(sh))
        off += n
    return out


def _pad_rows(a, rows):
    return jnp.pad(a, ((0, rows - a.shape[0]), (0, 0)))


def kernel(x, ln1_g, ln1_b, ffn1_w_gate, ffn1_w_up, ffn1_w_down, mix_w_in, pool_w, pool_scale, sconv_w, cconv_w, cconv_b, cnorm_g, cnorm_b, mix_w_out, ln2_g, ln2_b, ffn2_w_gate, ffn2_w_up, ffn2_w_down, ln3_g, ln3_b, loss_target, m_ln1_g, m_ln1_b, m_ffn1_w_gate, m_ffn1_w_up, m_ffn1_w_down, m_mix_w_in, m_pool_w, m_pool_scale, m_sconv_w, m_cconv_w, m_cconv_b, m_cnorm_g, m_cnorm_b, m_mix_w_out, m_ln2_g, m_ln2_b, m_ffn2_w_gate, m_ffn2_w_up, m_ffn2_w_down, m_ln3_g, m_ln3_b, v_ln1_g, v_ln1_b, v_ffn1_w_gate, v_ffn1_w_up, v_ffn1_w_down, v_mix_w_in, v_pool_w, v_pool_scale, v_sconv_w, v_cconv_w, v_cconv_b, v_cnorm_g, v_cnorm_b, v_mix_w_out, v_ln2_g, v_ln2_b, v_ffn2_w_gate, v_ffn2_w_up, v_ffn2_w_down, v_ln3_g, v_ln3_b):
    a = dict(locals())
    depth, d = ln1_g.shape
    t = x.shape[1]
    fs = ffn1_w_gate.shape[2]
    f = fs * N_DEV
    ins = mix_w_in.shape[2]
    ncols = ins * N_DEV
    outs = mix_w_out.shape[1]
    p, s = d // 4, 3 * d // 8
    pg = p // 4
    cs = sconv_w.shape[2]
    alpha = (2.0 * depth) ** 0.25
    me = 4 * lax.axis_index("x") + 2 * lax.axis_index("y") + lax.axis_index("c")
    tm = min(512, t)
    tm_bwd = min(256, t)
    tm_tn = min(1024, t)
    tm_down = min(1024, t)
    tt_fwd = min(512, t)
    tt_bwd = min(512, t)

    sizes = {"wg1": fs, "wu1": fs, "wd1": fs, "win": ins, "wout": outs, "wg2": fs, "wu2": fs, "wd2": fs,
             "conv": 16}
    per_layer = ["wg1", "wu1", "wd1", "win", "wout", "wg2", "wu2", "wd2"]
    wire = jnp.dtype(F32).itemsize // jnp.dtype(MM).itemsize
    n_conv = (SCONV_K + CCONV_K) * cs

    def conv_rows(l):
        flat = jnp.concatenate([sconv_w[l].reshape(-1), cconv_w[l].reshape(-1)])
        bits = lax.bitcast_convert_type(flat, MM).reshape(-1)
        return jnp.pad(bits, (0, sizes["conv"] * d - bits.shape[0])).reshape(sizes["conv"], d)

    def layer_block(l, key):
        if key == "conv":
            return conv_rows(l)
        return {"wg1": ffn1_w_gate[l].T, "wu1": ffn1_w_up[l].T, "wd1": ffn1_w_down[l], "win": mix_w_in[l].T,
                "wout": mix_w_out[l], "wg2": ffn2_w_gate[l].T, "wu2": ffn2_w_up[l].T,
                "wd2": ffn2_w_down[l]}[key].astype(MM)

    def gather_groups(l):
        if l == 0:
            return [["wg1", "wu1"], ["wd1"], ["win", "wout", "wg2", "wu2", "wd2", "conv"]]
        return [per_layer + ["conv"]]

    gathers = {}
    for l in range(depth):
        for gi, keys in enumerate(gather_groups(l)):
            src = jnp.concatenate([layer_block(l, k) for k in keys], axis=0)
            gathers[l, gi] = copy_start(src, (N_DEV,) + src.shape, gather=True, name=f"gather_start_{l}_{gi}")
    started = sum(st[4][0:1, 0:1] for st in gathers.values())

    def gathered(l, gi, after):
        mine, land = copy_wait(gathers[l, gi], after, gather=True, name=f"gather_wait_{l}_{gi}")
        wall = lax.dynamic_update_slice(land, mine[None], (me, 0, 0))
        out, off = {}, 0
        for k in gather_groups(l)[gi]:
            out[k] = wall[:, off:off + sizes[k], :].reshape(N_DEV * sizes[k], d)
            off += sizes[k]
        return out

    def conv_filters(rows):
        bits = rows.reshape(N_DEV, -1)[:, :n_conv * wire]
        vals = lax.bitcast_convert_type(bits.reshape(N_DEV, n_conv, wire) if wire > 1 else bits, F32)
        both = vals.reshape(N_DEV, SCONV_K + CCONV_K, cs).transpose(1, 0, 2).reshape(SCONV_K + CCONV_K, s)
        return _pad_rows(both[:SCONV_K], 8), _pad_rows(both[SCONV_K:], 32)

    eye = jnp.eye(4, dtype=F32)
    wbd_all = (pool_w[:, :, :, None, :] * eye[None, :, None, :, None]).reshape(depth, p, p)

    def row(v):
        return v.reshape(1, -1)

    saved = []
    cur, gam, bet = x[0], jnp.ones((1, d), F32), jnp.zeros((1, d), F32) + started
    for l in range(depth):
        w = gathered(l, 0, started if l == 0 else cur)
        sv = {"w": w}
        sv["xb1"], sv["g1"], sv["u1"], sv["h1"] = ffn_up(cur, gam, bet, w["wg1"], w["wu1"], tm=tm)
        if l == 0:
            w.update(gathered(l, 1, sv["xb1"]))
        sv["xh1"], sv["rs1"] = mm_res_ln(sv["h1"], w["wd1"], cur, gam, bet, alpha=alpha, scale=0.5, tm=tm_down)
        if l == 0:
            w.update(gathered(l, 2, sv["xh1"]))
        g1, b1 = row(ln1_g[l]), row(ln1_b[l])
        sv["proj"], sv["xb2"] = mm_in(sv["xh1"], g1, b1, w["win"], tm=tm)
        sv["wbd"] = wbd_all[l].astype(MM)
        sv["sw"], sv["cw"] = conv_filters(w["conv"])
        sv["cat"], sv["conv"] = mixer_fwd(sv["proj"], sv["wbd"], row(pool_scale[l]), sv["sw"], sv["cw"],
                                          row(cconv_b[l]), row(cnorm_g[l]), row(cnorm_b[l]), d=d, tt=tt_fwd)
        sv["xh2"], sv["rs2"] = mm_res_ln(sv["cat"], w["wout"], sv["xh1"], g1, b1, alpha=alpha, scale=1.0, tm=tm_down)
        g2, b2 = row(ln2_g[l]), row(ln2_b[l])
        sv["xb3"], sv["g3"], sv["u3"], sv["h3"] = ffn_up(sv["xh2"], g2, b2, w["wg2"], w["wu2"], tm=tm)
        sv["xh3"], sv["rs3"] = mm_res_ln(sv["h3"], w["wd2"], sv["xh2"], g2, b2, alpha=alpha, scale=0.5, tm=tm_down)
        saved.append(sv)
        cur, gam, bet = sv["xh3"], row(ln3_g[l]), row(ln3_b[l])

    dcur, lsum = loss_head(cur, gam, bet, loss_target[0], tm=tm)
    loss = lax.psum(lsum[0, 0] * (0.5 / d), MESH_AXES)

    exchanges = []
    small = [None] * depth

    def exchange(l, keys, gws):
        rows = sum(sizes[k] for k in keys)
        pad = -rows % 256 if _row_tile(rows, 256) < 64 else 0
        parts = [gb.reshape(N_DEV, sizes[k], d) for k, (_, gb) in zip(keys, gws)]
        mine = [lax.dynamic_slice_in_dim(g, me * sizes[k], sizes[k], axis=0) for k, (g, _) in zip(keys, gws)]
        if pad:
            parts.append(jnp.zeros((N_DEV, pad, d), MM))
            mine.append(jnp.zeros((pad, d), F32))
        st = copy_start(jnp.concatenate(parts, axis=1), (N_DEV - 1, rows + pad, d), gather=False,
                        name=f"exchange_start_{l}_{keys[0]}")
        exchanges.append((l, keys, st, jnp.concatenate(mine, axis=0)))
        return st[4][0:1, 0:1]

    sent = jnp.zeros((1, 1), F32)
    for l in reversed(range(depth)):
        sv = saved[l]
        w = sv["w"]
        dx, dyb, dg, du, dg3, db3 = ffn_bwd(dcur, sv["xh3"], sv["rs3"], row(ln3_g[l]) + sent, w["wd2"], w["wg2"],
                                             w["wu2"], sv["g3"], sv["u3"], alpha=alpha, tm=tm_bwd)
        gw_g2, gw_u2 = tn_matmul(dg, sv["xb3"], tm=tm_tn), tn_matmul(du, sv["xb3"], tm=tm_tn)
        gw_d2 = tn_matmul(sv["h3"], dyb, tm=tm_tn)
        dz, dzb, dcat, dg2, db2 = lnbwd_mm(dx, sv["xh2"], sv["rs2"], row(ln2_g[l]), w["wout"], tm=tm)
        dproj, dwbd, dps, dsw, dcw, dcb, dcg, dcbt = mixer_bwd(
            sv["proj"], dcat, sv["conv"], sv["wbd"], sv["wbd"].T, row(pool_scale[l]), sv["sw"], sv["cw"],
            row(cnorm_g[l]), row(cnorm_b[l]), d=d, tt=tt_bwd)
        gw_out = tn_matmul(sv["cat"], dzb, tm=tm_tn)
        gw_in = tn_matmul(dproj, sv["xb2"], tm=tm_tn)
        if l == 0:
            sent = exchange(l, ["wg2", "wu2", "wd2", "wout", "win"], [gw_g2, gw_u2, gw_d2, gw_out, gw_in])
        dx = mm_add(dproj, w["win"], dz, alpha=alpha, tm=tm)
        dx, dyb, dg, du, dg1, db1 = ffn_bwd(dx, sv["xh1"], sv["rs1"], row(ln1_g[l]) + sent, w["wd1"], w["wg1"],
                                             w["wu1"], sv["g1"], sv["u1"], alpha=alpha, tm=tm_bwd)
        dcur = dx
        if l == 0:
            sent = exchange(l, ["wg1"], [tn_matmul(dg, sv["xb1"], tm=tm_tn)])
            sent = exchange(l, ["wu1"], [tn_matmul(du, sv["xb1"], tm=tm_tn, after=sent)])
            sent = exchange(l, ["wd1"], [tn_matmul(sv["h1"], dyb, tm=tm_tn, after=sent)])
        else:
            gw_g1, gw_u1 = tn_matmul(dg, sv["xb1"], tm=tm_tn), tn_matmul(du, sv["xb1"], tm=tm_tn)
            gw_d1 = tn_matmul(sv["h1"], dyb, tm=tm_tn)
            sent = exchange(l, per_layer, [gw_g1, gw_u1, gw_d1, gw_in, gw_out, gw_g2, gw_u2, gw_d2])
        dpw = jnp.stack([dwbd[g * pg:(g + 1) * pg, g * pg:(g + 1) * pg] for g in range(4)])
        small[l] = [dg1, db1, dg2, db2, dg3, db3, dpw, dps, dsw[:SCONV_K], dcw[:CCONV_K], dcb, dcg, dcbt]
    grad_x = dcur[None]

    by_key = {k: [None] * depth for k in per_layer}

    def finish(ex, after):
        l, keys, st, mine = ex
        recv = copy_wait(st, after, gather=False, name=f"exchange_wait_{l}_{keys[0]}")[1]
        gsum, off = sum_parts(mine, recv), 0
        for k in keys:
            by_key[k][l] = gsum[off:off + sizes[k]]
            off += sizes[k]
        return gsum

    chain = exchanges[-1][3]
    for ex in exchanges[:-1]:
        chain = finish(ex, chain)

    small_shapes = [g.shape for g in small[0]]
    small_flat = _pack_small([g for l in range(depth) for g in small[l]])
    small_sum = sum_gathered(gather_small(small_flat)).reshape(-1)
    small_g = _unpack_small(small_sum, small_shapes * depth)
    n_small = len(small_shapes)

    grads, deltas, new_m, new_v = {}, {}, {}, {}

    def update(name, grad):
        w = a[name]
        c = w.shape[-1]
        dl, mn, vn = adamw(w.reshape(-1, c), grad.reshape(-1, c), a["m_" + name].reshape(-1, c),
                           a["v_" + name].reshape(-1, c))
        grads[name] = grad
        deltas[name], new_m[name], new_v[name] = dl.reshape(w.shape), mn.reshape(w.shape), vn.reshape(w.shape)
        return dl

    small_names = ["ln1_g", "ln1_b", "ln2_g", "ln2_b", "ln3_g", "ln3_b", "pool_w", "pool_scale", "sconv_w",
                   "cconv_w", "cconv_b", "cnorm_g", "cnorm_b"]
    for idx, name in enumerate(small_names):
        full = jnp.stack([small_g[l * n_small + idx] for l in range(depth)])
        if name in ("sconv_w", "cconv_w"):
            full = lax.dynamic_slice_in_dim(full, me * cs, cs, axis=2)
        chain = update(name, full.reshape(a[name].shape))

    big_names = {"wg1": "ffn1_w_gate", "wu1": "ffn1_w_up", "wd1": "ffn1_w_down", "win": "mix_w_in",
                 "wout": "mix_w_out", "wg2": "ffn2_w_gate", "wu2": "ffn2_w_up", "wd2": "ffn2_w_down"}
    sent_transposed = ("wg1", "wu1", "win", "wg2", "wu2")

    def update_big(k):
        return update(big_names[k], jnp.stack([g.T if k in sent_transposed else g for g in by_key[k]]))

    last_keys = exchanges[-1][1]
    for k in per_layer:
        if k not in last_keys:
            chain = update_big(k)
    finish(exchanges[-1], chain)
    for k in last_keys:
        update_big(k)

    return (loss, grad_x, *[grads[n] for n in WEIGHTS], *[deltas[n] for n in WEIGHTS],
            *[new_m[n] for n in WEIGHTS], *[new_v[n] for n in WEIGHTS])
```

```python
import functools

import jax
import jax.numpy as jnp
from jax import lax
from jax.experimental import pallas as pl
from jax.experimental.pallas import tpu as pltpu

F32 = jnp.float32
MM = jnp.bfloat16
LN_EPS = 1e-5
N_DEV = 8
MESH_AXES = ("x", "y", "c")
HALO = 32
CCONV_K = 31
SCONV_K = 3
CONV_ROWS = 32
FFN_CHUNK = 256
VMEM_LIMIT = 56 * 1024 * 1024
ADAM_LR, ADAM_B1, ADAM_B2, ADAM_EPS, ADAM_WD, ADAM_STEP = 0.001, 0.9, 0.999, 1e-08, 0.01, 10
MESH = pl.DeviceIdType.MESH


def _call(body, *, name, out_shape, in_specs, out_specs, grid=None, scratch=(), sem=None):
    kw = {}
    if grid is not None:
        kw["grid"] = grid
    params = dict(vmem_limit_bytes=VMEM_LIMIT)
    if sem is not None:
        params["dimension_semantics"] = sem
    return pl.pallas_call(body, name=name, out_shape=out_shape, in_specs=in_specs, out_specs=out_specs,
                          scratch_shapes=list(scratch), compiler_params=pltpu.CompilerParams(**params), **kw)


def _rows(tm, n):
    return pl.BlockSpec((tm, n), lambda i: (i, 0))


def _const(shape):
    nd = len(shape)
    return pl.BlockSpec(shape, lambda *_: (0,) * nd, pipeline_mode=pl.Buffered(1))


def _acc(shape):
    nd = len(shape)
    return pl.BlockSpec(shape, lambda *_: (0,) * nd)


def _row_tile(rows, cap):
    best = None
    for t in range(8, min(rows, cap) + 1, 8):
        if rows % t == 0:
            best = t
    return best if best is not None else rows


def _sig(x):
    return 1.0 / (1.0 + jnp.exp(-x))


def _ln_stats(z):
    mu = jnp.mean(z, axis=-1, keepdims=True)
    zc = z - mu
    var = jnp.mean(zc * zc, axis=-1, keepdims=True)
    rstd = lax.rsqrt(var + LN_EPS)
    return zc * rstd, rstd


def _ln_bwd(dout, xhat, rstd, gamma):
    dxh = dout * gamma
    m1 = jnp.mean(dxh, axis=-1, keepdims=True)
    m2 = jnp.mean(dxh * xhat, axis=-1, keepdims=True)
    return rstd * (dxh - m1 - xhat * m2)


def _colsum(v):
    return jnp.sum(v, axis=0, keepdims=True)


def _f_chunks(f, width=None):
    if width is not None and f % width == 0:
        return f // width, width
    n = 2 if f >= 2048 and f % 256 == 0 else 1
    return n, f // n


def _dot_nt(a, b):
    return lax.dot_general(a, b, (((1,), (1,)), ((), ())), preferred_element_type=F32)


def ffn_up(xin, gam, bet, wgt, wut, *, tm):
    t, d = xin.shape
    f = wgt.shape[0]
    nch, fc = _f_chunks(f, FFN_CHUNK)

    def body(x_ref, g_ref, b_ref, wg_ref, wu_ref, xb_ref, go_ref, uo_ref, h_ref):
        xb = (x_ref[...] * g_ref[...] + b_ref[...]).astype(MM)
        xb_ref[...] = xb
        for c in range(nch):
            sl = slice(c * fc, (c + 1) * fc)
            g = _dot_nt(xb, wg_ref[sl, :])
            u = _dot_nt(xb, wu_ref[sl, :])
            go_ref[:, sl] = g.astype(MM)
            uo_ref[:, sl] = u.astype(MM)
            h_ref[:, sl] = (g * _sig(g) * u).astype(MM)

    return _call(
        body, name="ffn_up", grid=(t // tm,), sem=("parallel",),
        in_specs=[_rows(tm, d), _const((1, d)), _const((1, d)), _const((f, d)), _const((f, d))],
        out_specs=[_rows(tm, d), _rows(tm, f), _rows(tm, f), _rows(tm, f)],
        out_shape=[jax.ShapeDtypeStruct((t, d), MM)] + [jax.ShapeDtypeStruct((t, f), MM)] * 3,
    )(xin, gam, bet, wgt, wut)


def mm_res_ln(a, w, xin, gam, bet, *, alpha, scale, tm):
    t, k = a.shape
    d = w.shape[1]

    def body(a_ref, w_ref, x_ref, g_ref, b_ref, xh_ref, rs_ref):
        y = jnp.dot(a_ref[...], w_ref[...], preferred_element_type=F32)
        x = x_ref[...] * g_ref[...] + b_ref[...]
        xh, rstd = _ln_stats(alpha * x + scale * y)
        xh_ref[...] = xh
        rs_ref[...] = rstd

    return _call(
        body, name="mm_res_ln", grid=(t // tm,), sem=("parallel",),
        in_specs=[_rows(tm, k), _const((k, d)), _rows(tm, d), _const((1, d)), _const((1, d))],
        out_specs=[_rows(tm, d), _rows(tm, 1)],
        out_shape=[jax.ShapeDtypeStruct((t, d), F32), jax.ShapeDtypeStruct((t, 1), F32)],
    )(a, w, xin, gam, bet)


def mm_in(xin, gam, bet, wt, *, tm):
    t, d = xin.shape
    n = wt.shape[0]

    def body(x_ref, g_ref, b_ref, w_ref, o_ref, xb_ref):
        xb = (x_ref[...] * g_ref[...] + b_ref[...]).astype(MM)
        xb_ref[...] = xb
        o_ref[...] = _dot_nt(xb, w_ref[...]).astype(MM)

    return _call(
        body, name="mm_in", grid=(t // tm,), sem=("parallel",),
        in_specs=[_rows(tm, d), _const((1, d)), _const((1, d)), _const((n, d))],
        out_specs=[_rows(tm, n), _rows(tm, d)],
        out_shape=[jax.ShapeDtypeStruct((t, n), MM), jax.ShapeDtypeStruct((t, d), MM)],
    )(xin, gam, bet, wt)


def loss_head(xh, gam, bet, target, *, tm):
    t, d = xh.shape

    def body(x_ref, g_ref, b_ref, t_ref, dy_ref, l_ref):
        @pl.when(pl.program_id(0) == 0)
        def _():
            l_ref[...] = jnp.zeros_like(l_ref)

        e = x_ref[...] * g_ref[...] + b_ref[...] - t_ref[...]
        dy_ref[...] = e * (1.0 / d)
        l_ref[...] += jnp.sum(_colsum(e * e), axis=1, keepdims=True)

    return _call(
        body, name="loss_head", grid=(t // tm,), sem=("arbitrary",),
        in_specs=[_rows(tm, d), _const((1, d)), _const((1, d)), _rows(tm, d)],
        out_specs=[_rows(tm, d), _acc((1, 1))],
        out_shape=[jax.ShapeDtypeStruct((t, d), F32), jax.ShapeDtypeStruct((1, 1), F32)],
    )(xh, gam, bet, target)


def _halo_specs(tt, ncols, t):
    per, last = tt // HALO, t // HALO - 1
    return [pl.BlockSpec((HALO, ncols), lambda i: (jnp.maximum(i * per - 1, 0), 0)),
            pl.BlockSpec((tt, ncols), lambda i: (i, 0)),
            pl.BlockSpec((HALO, ncols), lambda i: (jnp.minimum((i + 1) * per, last), 0))]


def _fill_ext(ext, prev_ref, cur_ref, next_ref, i, nt, tt):
    ext[0:HALO, :] = jnp.where(i > 0, prev_ref[...].astype(F32), 0.0)
    ext[HALO:HALO + tt, :] = cur_ref[...].astype(F32)
    ext[HALO + tt:HALO + tt + HALO, :] = jnp.where(i < nt - 1, next_ref[...].astype(F32), 0.0)


def _make_shifts(sh, n):
    for r in range(1, 8):
        sh[r, 0:n, :] = sh[0, r:r + n, :]


def _shift_reader(sh):
    def read(o, rows):
        r = o % 8
        return sh[r, o - r:o - r + rows, :]
    return read


def _conv_chunks_shifted(sh, w_ref, ktaps, src0, nrows, flip=False):
    for r0 in range(0, nrows, CONV_ROWS):
        acc = None
        base = src0 + r0
        for r in range(8):
            ks = [k for k in range(ktaps) if (base + k) % 8 == r]
            if not ks:
                continue
            lo = base + ks[0] - r
            slab = sh[r, lo:lo + CONV_ROWS + 8 * (len(ks) - 1), :]
            for q, k in enumerate(ks):
                kk = ktaps - 1 - k if flip else k
                term = w_ref[kk:kk + 1, :] * slab[8 * q:8 * q + CONV_ROWS, :]
                acc = term if acc is None else acc + term
        yield r0, acc


def _ref_reader(ref):
    def read(o, rows):
        return ref[o:o + rows, :]
    return read


def _conv_chunks(read, w_ref, ktaps, src0, nrows, flip=False):
    for r0 in range(0, nrows, CONV_ROWS):
        acc = None
        for k in range(ktaps):
            kk = ktaps - 1 - k if flip else k
            term = w_ref[kk:kk + 1, :] * read(src0 + r0 + k, CONV_ROWS)
            acc = term if acc is None else acc + term
        yield r0, acc


def _pool_groups(nrows, p):
    lane = lax.broadcasted_iota(jnp.int32, (nrows, p), 1)
    g = p // 4
    return lane < g, lane < 2 * g, lane < 3 * g


def _pool_select(groups, v2, v4, v8, v16):
    g0, g1, g2 = groups
    return jnp.where(g0, v2, jnp.where(g1, v4, jnp.where(g2, v8, v16)))


def _pool_count(groups, i, tt, row0, nrows, p, t):
    pos = i * tt + (row0 - HALO) + lax.broadcasted_iota(jnp.int32, (nrows, p), 0)
    half = _pool_select(groups, 1, 2, 4, 8)
    lo = jnp.clip(pos - half, 0, t)
    hi = jnp.clip(pos + half, 0, t)
    return jnp.maximum(hi - lo, 1).astype(F32)


def _pool_forward(ext, s2, s4, s8, groups, cnt, tt, p):
    e = tt + 2 * HALO
    s2[8:e - 8, :] = ext[7:e - 9, 0:p] + ext[8:e - 8, 0:p]
    s4[16:e - 16, :] = s2[15:e - 17, :] + s2[17:e - 15, :]
    s8[24:e - 24, :] = s4[22:e - 26, :] + s4[26:e - 22, :]
    s16 = s8[28:e - 36, :] + s8[36:e - 28, :]
    c = slice(HALO, HALO + tt)
    tot = _pool_select(groups, s2[c, :], s4[c, :], s8[c, :], s16)
    return tot / cnt - ext[c, 0:p]


def mixer_fwd(proj, wbd, pscale, sw, cw, cb, cg, cbt, *, d, tt):
    t, ncols = proj.shape
    p, s = d // 4, 3 * d // 8
    o_gb, o_gc, o_v, o_cv, o_cg = p, p + s, p + 2 * s, p + 3 * s, p + 4 * s
    nt, e = t // tt, tt + 2 * HALO

    def body(prev_ref, cur_ref, next_ref, wbd_ref, ps_ref, sw_ref, cw_ref, cb_ref, cg_ref, cbt_ref,
             cat_ref, conv_ref, ext, a_sh, cv_s, s2, s4, s8):
        i = pl.program_id(0)
        _fill_ext(ext, prev_ref, cur_ref, next_ref, i, nt, tt)
        a_sh[0, 0:e, :] = ext[:, o_cv:o_cv + s] * _sig(ext[:, o_cg:o_cg + s])
        a_sh[0, e:e + 8, :] = jnp.zeros((8, s), F32)
        _make_shifts(a_sh, e)
        for r0, acc in _conv_chunks_shifted(a_sh, cw_ref, CCONV_K, HALO - CCONV_K // 2, tt):
            b = acc + cb_ref[...]
            conv_ref[r0:r0 + CONV_ROWS, :] = b
            n, _ = _ln_stats(b)
            yn = n * cg_ref[...] + cbt_ref[...]
            cat_ref[r0:r0 + CONV_ROWS, p + s:d] = (yn * _sig(yn)).astype(MM)
        cv_s[...] = ext[:, o_gc:o_gc + s] * ext[:, o_v:o_v + s]
        for r0, acc in _conv_chunks(_ref_reader(cv_s), sw_ref, SCONV_K, HALO - SCONV_K // 2, tt):
            gb = ext[HALO + r0:HALO + r0 + CONV_ROWS, o_gb:o_gb + s]
            cat_ref[r0:r0 + CONV_ROWS, p:p + s] = (gb * acc).astype(MM)
        groups = _pool_groups(tt, p)
        cnt = _pool_count(groups, i, tt, HALO, tt, p, t)
        pooled = _pool_forward(ext, s2, s4, s8, groups, cnt, tt, p)
        ya = jnp.dot(pooled.astype(MM), wbd_ref[...], preferred_element_type=F32) * ps_ref[...]
        cat_ref[:, 0:p] = ya.astype(MM)

    return _call(
        body, name="mixer_fwd", grid=(nt,), sem=("parallel",),
        in_specs=_halo_specs(tt, ncols, t) + [_const((p, p)), _const((1, p)), _const((8, s)), _const((32, s)),
                                               _const((1, s)), _const((1, s)), _const((1, s))],
        out_specs=[_rows(tt, d), _rows(tt, s)],
        out_shape=[jax.ShapeDtypeStruct((t, d), MM), jax.ShapeDtypeStruct((t, s), F32)],
        scratch=[pltpu.VMEM((e, ncols), F32), pltpu.VMEM((8, e + 8, s), F32), pltpu.VMEM((e, s), F32),
                 pltpu.VMEM((e, p), F32), pltpu.VMEM((e, p), F32), pltpu.VMEM((e, p), F32)],
    )(proj, proj, proj, wbd, pscale, sw, cw, cb, cg, cbt)


def mixer_bwd(proj, dcat, conv, wbd, wbdt, pscale, sw, cw, cg, cbt, *, d, tt):
    t, ncols = proj.shape
    p, s = d // 4, 3 * d // 8
    o_gb, o_gc, o_v, o_cv, o_cg = p, p + s, p + 2 * s, p + 3 * s, p + 4 * s
    nt, e = t // tt, tt + 2 * HALO

    def body(pp_ref, pc_ref, pn_ref, dp_ref, dc_ref, dn_ref, bp_ref, bc_ref, bn_ref, wbd_ref, wbdt_ref, ps_ref,
             sw_ref, cw_ref, cg_ref, cbt_ref,
             dproj_ref, dwbd_ref, dps_ref, dsw_ref, dcw_ref, dcb_ref, dcg_ref, dcbt_ref,
             ext, dext, bext, a_sh, b_sh, sg_s, cv_s, ds_s, q_s, r2, r4, r8):
        i = pl.program_id(0)

        @pl.when(i == 0)
        def _():
            for ref in (dwbd_ref, dps_ref, dsw_ref, dcw_ref, dcb_ref, dcg_ref, dcbt_ref):
                ref[...] = jnp.zeros_like(ref)

        _fill_ext(ext, pp_ref, pc_ref, pn_ref, i, nt, tt)
        _fill_ext(dext, dp_ref, dc_ref, dn_ref, i, nt, tt)
        _fill_ext(bext, bp_ref, bc_ref, bn_ref, i, nt, tt)
        c = slice(HALO, HALO + tt)

        sg_s[...] = _sig(ext[:, o_cg:o_cg + s])
        a_sh[0, 0:e, :] = ext[:, o_cv:o_cv + s] * sg_s[...]
        a_sh[0, e:e + 8, :] = jnp.zeros((8, s), F32)
        _make_shifts(a_sh, e)
        read_a = _shift_reader(a_sh)
        for r0 in range(0, e, CONV_ROWS):
            b = bext[r0:r0 + CONV_ROWS, :]
            n, rstd = _ln_stats(b)
            yn = n * cg_ref[...] + cbt_ref[...]
            sy = _sig(yn)
            rows = slice(r0, r0 + CONV_ROWS)
            dyn = dext[rows, p + s:d] * (sy * (1.0 + yn * (1.0 - sy)))
            db = _ln_bwd(dyn, n, rstd, cg_ref[...])
            b_sh[0, rows, :] = db
            if HALO <= r0 < HALO + tt:
                dcg_ref[...] += _colsum(dyn * n)
                dcbt_ref[...] += _colsum(dyn)
                dcb_ref[...] += _colsum(db)
        b_sh[0, e:e + 8, :] = jnp.zeros((8, s), F32)
        _make_shifts(b_sh, e)
        for r0, da in _conv_chunks_shifted(b_sh, cw_ref, CCONV_K, HALO - CCONV_K // 2, tt, flip=True):
            rows = slice(HALO + r0, HALO + r0 + CONV_ROWS)
            sg = sg_s[rows, :]
            dproj_ref[r0:r0 + CONV_ROWS, o_cv:o_cv + s] = (da * sg).astype(MM)
            dproj_ref[r0:r0 + CONV_ROWS, o_cg:o_cg + s] = (
                da * ext[rows, o_cv:o_cv + s] * sg * (1.0 - sg)).astype(MM)
        for k in range(CCONV_K):
            lo = HALO + k - CCONV_K // 2
            dcw_ref[k:k + 1, :] += _colsum(b_sh[0, c, :] * read_a(lo, tt))

        cv_s[...] = ext[:, o_gc:o_gc + s] * ext[:, o_v:o_v + s]
        ds_s[...] = dext[:, p:p + s] * ext[:, o_gb:o_gb + s]
        for r0, acc in _conv_chunks(_ref_reader(cv_s), sw_ref, SCONV_K, HALO - SCONV_K // 2, tt):
            rows = slice(HALO + r0, HALO + r0 + CONV_ROWS)
            dproj_ref[r0:r0 + CONV_ROWS, o_gb:o_gb + s] = (dext[rows, p:p + s] * acc).astype(MM)
        for r0, dcv in _conv_chunks(_ref_reader(ds_s), sw_ref, SCONV_K, HALO - SCONV_K // 2, tt, flip=True):
            rows = slice(HALO + r0, HALO + r0 + CONV_ROWS)
            dproj_ref[r0:r0 + CONV_ROWS, o_gc:o_gc + s] = (dcv * ext[rows, o_v:o_v + s]).astype(MM)
            dproj_ref[r0:r0 + CONV_ROWS, o_v:o_v + s] = (dcv * ext[rows, o_gc:o_gc + s]).astype(MM)
        for k in range(SCONV_K):
            lo = HALO + k - SCONV_K // 2
            dsw_ref[k:k + 1, :] += _colsum(ds_s[c, :] * cv_s[lo:lo + tt, :])

        groups = _pool_groups(tt, p)
        cnt = _pool_count(groups, i, tt, HALO, tt, p, t)
        pooled = _pool_forward(ext, r2, r4, r8, groups, cnt, tt, p).astype(MM)
        ta = jnp.dot(pooled, wbd_ref[...], preferred_element_type=F32)
        dps_ref[...] += _colsum(dext[c, 0:p] * ta)
        dta = (dext[:, 0:p] * ps_ref[...]).astype(MM)
        dwbd_ref[...] += lax.dot_general(pooled, dta[HALO:HALO + tt, :], (((0,), (0,)), ((), ())),
                                         preferred_element_type=F32)
        dpool = jnp.dot(dta, wbdt_ref[...], preferred_element_type=F32)
        groups_e = _pool_groups(e, p)
        q_s[...] = dpool / _pool_count(groups_e, i, tt, 0, e, p, t)
        r2[8:e - 8, :] = q_s[8:e - 8, :] + q_s[9:e - 7, :]
        r4[16:e - 16, :] = r2[15:e - 17, :] + r2[17:e - 15, :]
        r8[24:e - 24, :] = r4[22:e - 26, :] + r4[26:e - 22, :]
        r16 = r8[28:e - 36, :] + r8[36:e - 28, :]
        du = _pool_select(groups, r2[c, :], r4[c, :], r8[c, :], r16) - dpool[HALO:HALO + tt, :]
        dproj_ref[:, 0:p] = du.astype(MM)

    small = [(p, p), (1, p), (8, s), (32, s), (1, s), (1, s), (1, s)]
    return _call(
        body, name="mixer_bwd", grid=(nt,), sem=("arbitrary",),
        in_specs=_halo_specs(tt, ncols, t) + _halo_specs(tt, d, t) + _halo_specs(tt, s, t) + [
            _const((p, p)), _const((p, p)), _const((1, p)), _const((8, s)), _const((32, s)),
            _const((1, s)), _const((1, s))],
        out_specs=[_rows(tt, ncols)] + [_acc(sh) for sh in small],
        out_shape=[jax.ShapeDtypeStruct((t, ncols), MM)] + [jax.ShapeDtypeStruct(sh, F32) for sh in small],
        scratch=[pltpu.VMEM((e, ncols), F32), pltpu.VMEM((e, d), F32), pltpu.VMEM((e, s), F32),
                 pltpu.VMEM((8, e + 8, s), F32), pltpu.VMEM((8, e + 8, s), F32)]
        + [pltpu.VMEM((e, s), F32)] * 3 + [pltpu.VMEM((e, p), F32)] * 4,
    )(proj, proj, proj, dcat, dcat, dcat, conv, conv, conv, wbd, wbdt, pscale, sw, cw, cg, cbt)


def ffn_bwd(dout, xh, rs, gam, wd, wgt, wut, g, u, *, alpha, tm):
    t, d = dout.shape
    f = g.shape[1]
    nch, fc = _f_chunks(f, FFN_CHUNK)

    def body(do_ref, xh_ref, rs_ref, gm_ref, wd_ref, wgt_ref, wut_ref, g_ref, u_ref,
             dx_ref, dyb_ref, dg_ref, du_ref, dgam_ref, dbet_ref):
        @pl.when(pl.program_id(0) == 0)
        def _():
            dgam_ref[...] = jnp.zeros_like(dgam_ref)
            dbet_ref[...] = jnp.zeros_like(dbet_ref)

        dout_v, xhat = do_ref[...], xh_ref[...]
        dgam_ref[...] += _colsum(dout_v * xhat)
        dbet_ref[...] += _colsum(dout_v)
        dz = _ln_bwd(dout_v, xhat, rs_ref[...], gm_ref[...])
        dyb = (0.5 * dz).astype(MM)
        dyb_ref[...] = dyb
        for c in range(nch):
            sl = slice(c * fc, (c + 1) * fc)
            dh = _dot_nt(dyb, wd_ref[sl, :])
            gv, uv = g_ref[:, sl].astype(F32), u_ref[:, sl].astype(F32)
            sg = _sig(gv)
            dg_ref[:, sl] = (dh * uv * (sg * (1.0 + gv * (1.0 - sg)))).astype(MM)
            du_ref[:, sl] = (dh * (gv * sg)).astype(MM)
        dx_ref[...] = (alpha * dz + jnp.dot(dg_ref[...], wgt_ref[...], preferred_element_type=F32)
                       + jnp.dot(du_ref[...], wut_ref[...], preferred_element_type=F32))

    return _call(
        body, name="ffn_bwd", grid=(t // tm,), sem=("arbitrary",),
        in_specs=[_rows(tm, d), _rows(tm, d), _rows(tm, 1), _const((1, d)), _const((f, d)), _const((f, d)),
                  _const((f, d)), _rows(tm, f), _rows(tm, f)],
        out_specs=[_rows(tm, d), _rows(tm, d), _rows(tm, f), _rows(tm, f), _acc((1, d)), _acc((1, d))],
        out_shape=[jax.ShapeDtypeStruct((t, d), F32), jax.ShapeDtypeStruct((t, d), MM),
                   jax.ShapeDtypeStruct((t, f), MM), jax.ShapeDtypeStruct((t, f), MM),
                   jax.ShapeDtypeStruct((1, d), F32), jax.ShapeDtypeStruct((1, d), F32)],
    )(dout, xh, rs, gam, wd, wgt, wut, g, u)


def lnbwd_mm(dout, xh, rs, gam, w, *, tm):
    t, d = dout.shape
    n = w.shape[0]

    def body(do_ref, xh_ref, rs_ref, gm_ref, wt_ref, dz_ref, dzb_ref, da_ref, dgam_ref, dbet_ref):
        @pl.when(pl.program_id(0) == 0)
        def _():
            dgam_ref[...] = jnp.zeros_like(dgam_ref)
            dbet_ref[...] = jnp.zeros_like(dbet_ref)

        dout_v, xhat = do_ref[...], xh_ref[...]
        dgam_ref[...] += _colsum(dout_v * xhat)
        dbet_ref[...] += _colsum(dout_v)
        dz = _ln_bwd(dout_v, xhat, rs_ref[...], gm_ref[...])
        dz_ref[...] = dz
        dzb = dz.astype(MM)
        dzb_ref[...] = dzb
        da_ref[...] = _dot_nt(dzb, wt_ref[...]).astype(MM)

    return _call(
        body, name="lnbwd_mm", grid=(t // tm,), sem=("arbitrary",),
        in_specs=[_rows(tm, d), _rows(tm, d), _rows(tm, 1), _const((1, d)), _const((n, d))],
        out_specs=[_rows(tm, d), _rows(tm, d), _rows(tm, n), _acc((1, d)), _acc((1, d))],
        out_shape=[jax.ShapeDtypeStruct((t, d), F32), jax.ShapeDtypeStruct((t, d), MM),
                   jax.ShapeDtypeStruct((t, n), MM), jax.ShapeDtypeStruct((1, d), F32),
                   jax.ShapeDtypeStruct((1, d), F32)],
    )(dout, xh, rs, gam, w)


def mm_add(a, w, r, *, alpha, tm):
    t, k = a.shape
    d = w.shape[1]

    def body(a_ref, w_ref, r_ref, o_ref):
        o_ref[...] = jnp.dot(a_ref[...], w_ref[...], preferred_element_type=F32) + alpha * r_ref[...]

    return _call(
        body, name="mm_add", grid=(t // tm,), sem=("parallel",),
        in_specs=[_rows(tm, k), _const((k, d)), _rows(tm, d)],
        out_specs=_rows(tm, d),
        out_shape=jax.ShapeDtypeStruct((t, d), F32),
    )(a, w, r)


def tn_matmul(a, b, *, tm, after=None):
    t, n = a.shape
    d = b.shape[1]
    nch, nc = _f_chunks(n)
    steps = t // tm

    def body(a_ref, b_ref, *rest):
        o_ref, ob_ref = rest[-2:]

        @pl.when(pl.program_id(1) == 0)
        def _():
            o_ref[...] = jnp.zeros_like(o_ref)

        o_ref[...] += lax.dot_general(a_ref[...], b_ref[...], (((0,), (0,)), ((), ())),
                                      preferred_element_type=F32)

        @pl.when(pl.program_id(1) == steps - 1)
        def _():
            ob_ref[...] = o_ref[...].astype(MM)

    extra = [] if after is None else [after]
    return _call(
        body, name="tn_matmul", grid=(nch, steps), sem=("parallel", "arbitrary"),
        in_specs=[pl.BlockSpec((tm, nc), lambda j, i: (i, j)), pl.BlockSpec((tm, d), lambda j, i: (i, 0))]
        + [pl.BlockSpec(memory_space=pl.ANY)] * len(extra),
        out_specs=[pl.BlockSpec((nc, d), lambda j, i: (j, 0))] * 2,
        out_shape=[jax.ShapeDtypeStruct((n, d), F32), jax.ShapeDtypeStruct((n, d), MM)],
    )(a, b, *extra)


def sum_parts(own, recv):
    r, d = own.shape
    n = recv.shape[0]
    tr = _row_tile(r, 256)

    def body(own_ref, recv_ref, o_ref):
        acc = own_ref[...]
        for k in range(n):
            acc = acc + recv_ref[k].astype(F32)
        o_ref[...] = acc

    return _call(
        body, name="sum_parts", grid=(r // tr,), sem=("parallel",),
        in_specs=[_rows(tr, d), pl.BlockSpec((n, tr, d), lambda i: (0, i, 0))],
        out_specs=_rows(tr, d),
        out_shape=jax.ShapeDtypeStruct((r, d), F32),
    )(own, recv)


def sum_gathered(parts):
    _, r, n = parts.shape

    def body(p_ref, o_ref):
        acc = p_ref[0]
        for j in range(1, N_DEV):
            acc = acc + p_ref[j]
        o_ref[...] = acc

    return _call(
        body, name="sum_gathered",
        in_specs=[pl.BlockSpec(memory_space=pltpu.VMEM)], out_specs=pl.BlockSpec(memory_space=pltpu.VMEM),
        out_shape=jax.ShapeDtypeStruct((r, n), F32),
    )(parts)


def adamw(w, g, m, v):
    r, c = w.shape
    tr = _row_tile(r, 512)

    def body(w_ref, g_ref, m_ref, v_ref, d_ref, mo_ref, vo_ref):
        gv = g_ref[...]
        mn = ADAM_B1 * m_ref[...] + (1.0 - ADAM_B1) * gv
        vn = ADAM_B2 * v_ref[...] + (1.0 - ADAM_B2) * (gv * gv)
        m_hat = mn / (1.0 - ADAM_B1 ** ADAM_STEP)
        v_hat = vn / (1.0 - ADAM_B2 ** ADAM_STEP)
        d_ref[...] = -ADAM_LR * (m_hat / (jnp.sqrt(v_hat) + ADAM_EPS) + ADAM_WD * w_ref[...])
        mo_ref[...] = mn
        vo_ref[...] = vn

    return _call(
        body, name="adamw", grid=(r // tr,), sem=("parallel",),
        in_specs=[_rows(tr, c)] * 4, out_specs=[_rows(tr, c)] * 3,
        out_shape=[jax.ShapeDtypeStruct((r, c), F32)] * 3,
    )(w, g, m, v)


def _mesh_pos():
    return lax.axis_index("x"), lax.axis_index("y"), lax.axis_index("c")


def _two_level_gather(x_ref, out_ref, send_sems, recv_sems, local_sem):
    x, y, c = _mesh_pos()
    me, sibling = (x, y, c), (x, y, 1 - c)
    chips = [(1 - x, y), (x, 1 - y), (1 - x, 1 - y)]

    def slot(px, py, pc):
        return out_ref.at[4 * px + 2 * py + pc]

    def copy(k, block, to, src=None):
        return pltpu.make_async_remote_copy(
            src_ref=slot(*block) if src is None else src, dst_ref=slot(*block),
            send_sem=send_sems.at[k], recv_sem=recv_sems.at[k], device_id=to, device_id_type=MESH)

    mine = pltpu.make_async_copy(x_ref, slot(*me), local_sem)
    mine.start()
    first = [copy(1 + j, me, (*chip, c), src=x_ref) for j, chip in enumerate(chips)]
    first.append(copy(0, me, sibling, src=x_ref))
    for cp in first:
        cp.start()
    passed = [copy(4 + j, (*chip, c), sibling) for j, chip in enumerate(chips)]
    for j, chip in enumerate(chips):
        copy(1 + j, (*chip, c), me).wait_recv()
        passed[j].start()
    copy(0, sibling, me).wait_recv()
    for j, chip in enumerate(chips):
        copy(4 + j, (*chip, 1 - c), me).wait_recv()
    for cp in first + passed:
        cp.wait_send()
    mine.wait()


def gather_small(x):
    return _call(
        _two_level_gather_body(), name="gather_small",
        in_specs=[pl.BlockSpec(memory_space=pltpu.VMEM)], out_specs=pl.BlockSpec(memory_space=pltpu.VMEM),
        out_shape=jax.ShapeDtypeStruct((N_DEV,) + x.shape, x.dtype),
        scratch=[pltpu.SemaphoreType.DMA((7,)), pltpu.SemaphoreType.DMA((7,)), pltpu.SemaphoreType.DMA(())],
    )(x)


def _two_level_gather_body():
    def body(x_ref, out_ref, send_sems, recv_sems, local_sem):
        _two_level_gather(x_ref, out_ref, send_sems, recv_sems, local_sem)
    return body


_HBM = pl.BlockSpec(memory_space=pltpu.HBM)
_SEM = pl.BlockSpec(memory_space=pltpu.SEMAPHORE)
_EFFECT = pltpu.SideEffectType.DATAFLOW_SIDE_EFFECTING
_RELATIONS = (4, 2, 6, 5, 3, 7, 1)


def _split_copy(src_ref, land_ref, send_sems, recv_sems, k, gather):
    x, y, c = _mesh_pos()
    px, py, pc = (1 - x if k & 4 else x), (1 - y if k & 2 else y), (1 - c if k & 1 else c)
    if gather:
        src, dst = src_ref, land_ref.at[4 * x + 2 * y + c]
    else:
        src, dst = src_ref.at[4 * px + 2 * py + pc], land_ref.at[k - 1]
    return pltpu.make_async_remote_copy(src_ref=src, dst_ref=dst, send_sem=send_sems.at[k - 1],
                                        recv_sem=recv_sems.at[k - 1], device_id=(px, py, pc), device_id_type=MESH)


def copy_start(src, land_shape, *, gather, name):
    def body(src_ref, land_ref, send_sems, recv_sems, src_thru, land_thru, token):
        for k in _RELATIONS:
            _split_copy(src_ref, land_ref, send_sems, recv_sems, k, gather).start()
        token[...] = jnp.zeros_like(token)

    return pl.pallas_call(
        body, name=name,
        out_shape=(pltpu.SemaphoreType.DMA((7,)), pltpu.SemaphoreType.DMA((7,)), pltpu.HBM(src.shape, src.dtype),
                   pltpu.HBM(land_shape, src.dtype), jax.ShapeDtypeStruct((8, 128), F32)),
        in_specs=(_HBM, _HBM), out_specs=(_SEM, _SEM, _HBM, _HBM, pl.BlockSpec(memory_space=pltpu.VMEM)),
        input_output_aliases={0: 2, 1: 3},
        compiler_params=pltpu.CompilerParams(has_side_effects=_EFFECT),
    )(pltpu.with_memory_space_constraint(src, pltpu.HBM),
      pltpu.with_memory_space_constraint(lax.empty(land_shape, src.dtype), pltpu.HBM))


def copy_wait(started, after, *, gather, name):
    send_sems, recv_sems, src_thru, land_thru, _ = started

    def body(src_ref, land_ref, send_sems, recv_sems, after_ref, src_dead, got_ref):
        for k in _RELATIONS:
            cp = _split_copy(src_ref, land_ref, send_sems, recv_sems, k, gather)
            cp.wait_send()
            cp.wait_recv()

    return pl.pallas_call(
        body, name=name,
        out_shape=(pltpu.HBM(src_thru.shape, src_thru.dtype), pltpu.HBM(land_thru.shape, land_thru.dtype)),
        in_specs=(_HBM, _HBM, _SEM, _SEM, pl.BlockSpec(memory_space=pl.ANY)), out_specs=(_HBM, _HBM),
        input_output_aliases={0: 0, 1: 1},
        compiler_params=pltpu.CompilerParams(has_side_effects=_EFFECT),
    )(src_thru, land_thru, send_sems, recv_sems, after)


WEIGHTS = ['ln1_g', 'ln1_b', 'ffn1_w_gate', 'ffn1_w_up', 'ffn1_w_down', 'mix_w_in', 'pool_w', 'pool_scale',
           'sconv_w', 'cconv_w', 'cconv_b', 'cnorm_g', 'cnorm_b', 'mix_w_out', 'ln2_g', 'ln2_b',
           'ffn2_w_gate', 'ffn2_w_up', 'ffn2_w_down', 'ln3_g', 'ln3_b']


def _pack_small(pieces):
    flat = jnp.concatenate([p.reshape(-1) for p in pieces])
    pad = -flat.shape[0] % 1024
    return jnp.pad(flat, (0, pad)).reshape(-1, 128)


def _unpack_small(flat, shapes):
    out, off = [], 0
    for sh in shapes:
        n = 1
        for s in sh:
            n *= s
        out.append(flat[off:off + n].reshape(sh))
        off += n
    return out


def _pad_rows(a, rows):
    return jnp.pad(a, ((0, rows - a.shape[0]), (0, 0)))


def kernel(x, ln1_g, ln1_b, ffn1_w_gate, ffn1_w_up, ffn1_w_down, mix_w_in, pool_w, pool_scale, sconv_w, cconv_w, cconv_b, cnorm_g, cnorm_b, mix_w_out, ln2_g, ln2_b, ffn2_w_gate, ffn2_w_up, ffn2_w_down, ln3_g, ln3_b, loss_target, m_ln1_g, m_ln1_b, m_ffn1_w_gate, m_ffn1_w_up, m_ffn1_w_down, m_mix_w_in, m_pool_w, m_pool_scale, m_sconv_w, m_cconv_w, m_cconv_b, m_cnorm_g, m_cnorm_b, m_mix_w_out, m_ln2_g, m_ln2_b, m_ffn2_w_gate, m_ffn2_w_up, m_ffn2_w_down, m_ln3_g, m_ln3_b, v_ln1_g, v_ln1_b, v_ffn1_w_gate, v_ffn1_w_up, v_ffn1_w_down, v_mix_w_in, v_pool_w, v_pool_scale, v_sconv_w, v_cconv_w, v_cconv_b, v_cnorm_g, v_cnorm_b, v_mix_w_out, v_ln2_g, v_ln2_b, v_ffn2_w_gate, v_ffn2_w_up, v_ffn2_w_down, v_ln3_g, v_ln3_b):
    a = dict(locals())
    depth, d = ln1_g.shape
    t = x.shape[1]
    fs = ffn1_w_gate.shape[2]
    f = fs * N_DEV
    ins = mix_w_in.shape[2]
    ncols = ins * N_DEV
    outs = mix_w_out.shape[1]
    p, s = d // 4, 3 * d // 8
    pg = p // 4
    cs = sconv_w.shape[2]
    alpha = (2.0 * depth) ** 0.25
    me = 4 * lax.axis_index("x") + 2 * lax.axis_index("y") + lax.axis_index("c")
    tm = min(512, t)
    tm_bwd = min(256, t)
    tm_tn = min(2048, t)
    tm_down = min(1024, t)
    tt_fwd = min(1024, t)
    tt_bwd = min(512, t)

    sizes = {"wg1": fs, "wu1": fs, "wd1": fs, "win": ins, "wout": outs, "wg2": fs, "wu2": fs, "wd2": fs,
             "conv": 16}
    per_layer = ["wg1", "wu1", "wd1", "win", "wout", "wg2", "wu2", "wd2"]
    wire = jnp.dtype(F32).itemsize // jnp.dtype(MM).itemsize
    n_conv = (SCONV_K + CCONV_K) * cs

    def conv_rows(l):
        flat = jnp.concatenate([sconv_w[l].reshape(-1), cconv_w[l].reshape(-1)])
        bits = lax.bitcast_convert_type(flat, MM).reshape(-1)
        return jnp.pad(bits, (0, sizes["conv"] * d - bits.shape[0])).reshape(sizes["conv"], d)

    def layer_block(l, key):
        if key == "conv":
            return conv_rows(l)
        return {"wg1": ffn1_w_gate[l].T, "wu1": ffn1_w_up[l].T, "wd1": ffn1_w_down[l], "win": mix_w_in[l].T,
                "wout": mix_w_out[l], "wg2": ffn2_w_gate[l].T, "wu2": ffn2_w_up[l].T,
                "wd2": ffn2_w_down[l]}[key].astype(MM)

    def gather_groups(l):
        if l == 0:
            return [["wg1", "wu1"], ["wd1"], ["win", "wout", "wg2", "wu2", "wd2", "conv"]]
        return [per_layer + ["conv"]]

    gathers = {}
    for l in range(depth):
        for gi, keys in enumerate(gather_groups(l)):
            src = jnp.concatenate([layer_block(l, k) for k in keys], axis=0)
            gathers[l, gi] = copy_start(src, (N_DEV,) + src.shape, gather=True, name=f"gather_start_{l}_{gi}")
    started = sum(st[4][0:1, 0:1] for st in gathers.values())

    def gathered(l, gi, after):
        mine, land = copy_wait(gathers[l, gi], after, gather=True, name=f"gather_wait_{l}_{gi}")
        wall = lax.dynamic_update_slice(land, mine[None], (me, 0, 0))
        out, off = {}, 0
        for k in gather_groups(l)[gi]:
            out[k] = wall[:, off:off + sizes[k], :].reshape(N_DEV * sizes[k], d)
            off += sizes[k]
        return out

    def conv_filters(rows):
        bits = rows.reshape(N_DEV, -1)[:, :n_conv * wire]
        vals = lax.bitcast_convert_type(bits.reshape(N_DEV, n_conv, wire) if wire > 1 else bits, F32)
        both = vals.reshape(N_DEV, SCONV_K + CCONV_K, cs).transpose(1, 0, 2).reshape(SCONV_K + CCONV_K, s)
        return _pad_rows(both[:SCONV_K], 8), _pad_rows(both[SCONV_K:], 32)

    eye = jnp.eye(4, dtype=F32)
    wbd_all = (pool_w[:, :, :, None, :] * eye[None, :, None, :, None]).reshape(depth, p, p)

    def row(v):
        return v.reshape(1, -1)

    saved = []
    cur, gam, bet = x[0], jnp.ones((1, d), F32), jnp.zeros((1, d), F32) + started
    for l in range(depth):
        w = gathered(l, 0, started if l == 0 else cur)
        sv = {"w": w}
        sv["xb1"], sv["g1"], sv["u1"], sv["h1"] = ffn_up(cur, gam, bet, w["wg1"], w["wu1"], tm=tm)
        if l == 0:
            w.update(gathered(l, 1, sv["xb1"]))
        sv["xh1"], sv["rs1"] = mm_res_ln(sv["h1"], w["wd1"], cur, gam, bet, alpha=alpha, scale=0.5, tm=tm_down)
        if l == 0:
            w.update(gathered(l, 2, sv["xh1"]))
        g1, b1 = row(ln1_g[l]), row(ln1_b[l])
        sv["proj"], sv["xb2"] = mm_in(sv["xh1"], g1, b1, w["win"], tm=tm_down)
        sv["wbd"] = wbd_all[l].astype(MM)
        sv["sw"], sv["cw"] = conv_filters(w["conv"])
        sv["cat"], sv["conv"] = mixer_fwd(sv["proj"], sv["wbd"], row(pool_scale[l]), sv["sw"], sv["cw"],
                                          row(cconv_b[l]), row(cnorm_g[l]), row(cnorm_b[l]), d=d, tt=tt_fwd)
        sv["xh2"], sv["rs2"] = mm_res_ln(sv["cat"], w["wout"], sv["xh1"], g1, b1, alpha=alpha, scale=1.0, tm=tm_down)
        g2, b2 = row(ln2_g[l]), row(ln2_b[l])
        sv["xb3"], sv["g3"], sv["u3"], sv["h3"] = ffn_up(sv["xh2"], g2, b2, w["wg2"], w["wu2"], tm=tm)
        sv["xh3"], sv["rs3"] = mm_res_ln(sv["h3"], w["wd2"], sv["xh2"], g2, b2, alpha=alpha, scale=0.5, tm=tm_down)
        saved.append(sv)
        cur, gam, bet = sv["xh3"], row(ln3_g[l]), row(ln3_b[l])

    dcur, lsum = loss_head(cur, gam, bet, loss_target[0], tm=tm)
    loss = lax.psum(lsum[0, 0] * (0.5 / d), MESH_AXES)

    exchanges = []
    small = [None] * depth

    def exchange(l, keys, gws):
        rows = sum(sizes[k] for k in keys)
        pad = -rows % 256 if _row_tile(rows, 256) < 64 else 0
        parts = [gb.reshape(N_DEV, sizes[k], d) for k, (_, gb) in zip(keys, gws)]
        mine = [lax.dynamic_slice_in_dim(g, me * sizes[k], sizes[k], axis=0) for k, (g, _) in zip(keys, gws)]
        if pad:
            parts.append(jnp.zeros((N_DEV, pad, d), MM))
            mine.append(jnp.zeros((pad, d), F32))
        st = copy_start(jnp.concatenate(parts, axis=1), (N_DEV - 1, rows + pad, d), gather=False,
                        name=f"exchange_start_{l}_{keys[0]}")
        exchanges.append((l, keys, st, jnp.concatenate(mine, axis=0)))
        return st[4][0:1, 0:1]

    sent = jnp.zeros((1, 1), F32)
    for l in reversed(range(depth)):
        sv = saved[l]
        w = sv["w"]
        dx, dyb, dg, du, dg3, db3 = ffn_bwd(dcur, sv["xh3"], sv["rs3"], row(ln3_g[l]) + sent, w["wd2"], w["wg2"],
                                             w["wu2"], sv["g3"], sv["u3"], alpha=alpha, tm=tm_bwd)
        gw_g2, gw_u2 = tn_matmul(dg, sv["xb3"], tm=tm_tn), tn_matmul(du, sv["xb3"], tm=tm_tn)
        gw_d2 = tn_matmul(sv["h3"], dyb, tm=tm_tn)
        dz, dzb, dcat, dg2, db2 = lnbwd_mm(dx, sv["xh2"], sv["rs2"], row(ln2_g[l]), w["wout"], tm=tm_down)
        dproj, dwbd, dps, dsw, dcw, dcb, dcg, dcbt = mixer_bwd(
            sv["proj"], dcat, sv["conv"], sv["wbd"], sv["wbd"].T, row(pool_scale[l]), sv["sw"], sv["cw"],
            row(cnorm_g[l]), row(cnorm_b[l]), d=d, tt=tt_bwd)
        gw_out = tn_matmul(sv["cat"], dzb, tm=tm_tn)
        gw_in = tn_matmul(dproj, sv["xb2"], tm=tm_tn)
        if l == 0:
            sent = exchange(l, ["wg2", "wu2", "wd2", "wout", "win"], [gw_g2, gw_u2, gw_d2, gw_out, gw_in])
        dx = mm_add(dproj, w["win"], dz, alpha=alpha, tm=tm_down)
        dx, dyb, dg, du, dg1, db1 = ffn_bwd(dx, sv["xh1"], sv["rs1"], row(ln1_g[l]) + sent, w["wd1"], w["wg1"],
                                             w["wu1"], sv["g1"], sv["u1"], alpha=alpha, tm=tm_bwd)
        dcur = dx
        if l == 0:
            sent = exchange(l, ["wg1"], [tn_matmul(dg, sv["xb1"], tm=tm_tn)])
            sent = exchange(l, ["wu1"], [tn_matmul(du, sv["xb1"], tm=tm_tn, after=sent)])
            sent = exchange(l, ["wd1"], [tn_matmul(sv["h1"], dyb, tm=tm_tn, after=sent)])
        else:
            gw_g1, gw_u1 = tn_matmul(dg, sv["xb1"], tm=tm_tn), tn_matmul(du, sv["xb1"], tm=tm_tn)
            gw_d1 = tn_matmul(sv["h1"], dyb, tm=tm_tn)
            sent = exchange(l, per_layer, [gw_g1, gw_u1, gw_d1, gw_in, gw_out, gw_g2, gw_u2, gw_d2])
        dpw = jnp.stack([dwbd[g * pg:(g + 1) * pg, g * pg:(g + 1) * pg] for g in range(4)])
        small[l] = [dg1, db1, dg2, db2, dg3, db3, dpw, dps, dsw[:SCONV_K], dcw[:CCONV_K], dcb, dcg, dcbt]
    grad_x = dcur[None]

    by_key = {k: [None] * depth for k in per_layer}

    def finish(ex, after):
        l, keys, st, mine = ex
        recv = copy_wait(st, after, gather=False, name=f"exchange_wait_{l}_{keys[0]}")[1]
        gsum, off = sum_parts(mine, recv), 0
        for k in keys:
            by_key[k][l] = gsum[off:off + sizes[k]]
            off += sizes[k]
        return gsum

    chain = exchanges[-1][3]
    for ex in exchanges[:-1]:
        chain = finish(ex, chain)

    small_shapes = [g.shape for g in small[0]]
    small_flat = _pack_small([g for l in range(depth) for g in small[l]])
    small_sum = sum_gathered(gather_small(small_flat)).reshape(-1)
    small_g = _unpack_small(small_sum, small_shapes * depth)
    n_small = len(small_shapes)

    grads, deltas, new_m, new_v = {}, {}, {}, {}

    def update(name, grad):
        w = a[name]
        c = w.shape[-1]
        dl, mn, vn = adamw(w.reshape(-1, c), grad.reshape(-1, c), a["m_" + name].reshape(-1, c),
                           a["v_" + name].reshape(-1, c))
        grads[name] = grad
        deltas[name], new_m[name], new_v[name] = dl.reshape(w.shape), mn.reshape(w.shape), vn.reshape(w.shape)
        return dl

    small_names = ["ln1_g", "ln1_b", "ln2_g", "ln2_b", "ln3_g", "ln3_b", "pool_w", "pool_scale", "sconv_w",
                   "cconv_w", "cconv_b", "cnorm_g", "cnorm_b"]
    for idx, name in enumerate(small_names):
        full = jnp.stack([small_g[l * n_small + idx] for l in range(depth)])
        if name in ("sconv_w", "cconv_w"):
            full = lax.dynamic_slice_in_dim(full, me * cs, cs, axis=2)
        chain = update(name, full.reshape(a[name].shape))

    big_names = {"wg1": "ffn1_w_gate", "wu1": "ffn1_w_up", "wd1": "ffn1_w_down", "win": "mix_w_in",
                 "wout": "mix_w_out", "wg2": "ffn2_w_gate", "wu2": "ffn2_w_up", "wd2": "ffn2_w_down"}
    sent_transposed = ("wg1", "wu1", "win", "wg2", "wu2")

    def update_big(k):
        return update(big_names[k], jnp.stack([g.T if k in sent_transposed else g for g in by_key[k]]))

    last_keys = exchanges[-1][1]
    for k in per_layer:
        if k not in last_keys:
            chain = update_big(k)
    finish(exchanges[-1], chain)
    for k in last_keys:
        update_big(k)

    return (loss, grad_x, *[grads[n] for n in WEIGHTS], *[deltas[n] for n in WEIGHTS],
            *[new_m[n] for n in WEIGHTS], *[new_v[n] for n in WEIGHTS])
```

```python
import functools

import jax
import jax.numpy as jnp
from jax import lax
from jax.experimental import pallas as pl
from jax.experimental.pallas import tpu as pltpu

F32 = jnp.float32
MM = jnp.bfloat16
LN_EPS = 1e-5
N_DEV = 8
MESH_AXES = ("x", "y", "c")
HALO = 32
CCONV_K = 31
SCONV_K = 3
CONV_ROWS = 32
FFN_CHUNK = 256
VMEM_LIMIT = 56 * 1024 * 1024
ADAM_LR, ADAM_B1, ADAM_B2, ADAM_EPS, ADAM_WD, ADAM_STEP = 0.001, 0.9, 0.999, 1e-08, 0.01, 10
MESH = pl.DeviceIdType.MESH


def _call(body, *, name, out_shape, in_specs, out_specs, grid=None, scratch=(), sem=None, aliases=None):
    kw = {}
    if grid is not None:
        kw["grid"] = grid
    if aliases:
        kw["input_output_aliases"] = aliases
    params = dict(vmem_limit_bytes=VMEM_LIMIT)
    if sem is not None:
        params["dimension_semantics"] = sem
    return pl.pallas_call(body, name=name, out_shape=out_shape, in_specs=in_specs, out_specs=out_specs,
                          scratch_shapes=list(scratch), compiler_params=pltpu.CompilerParams(**params), **kw)


def _rows(tm, n):
    return pl.BlockSpec((tm, n), lambda i: (i, 0))


def _const(shape):
    nd = len(shape)
    return pl.BlockSpec(shape, lambda *_: (0,) * nd, pipeline_mode=pl.Buffered(1))


def _acc(shape):
    nd = len(shape)
    return pl.BlockSpec(shape, lambda *_: (0,) * nd)


def _row_tile(rows, cap):
    best = None
    for t in range(8, min(rows, cap) + 1, 8):
        if rows % t == 0:
            best = t
    return best if best is not None else rows


def _sig(x):
    return 1.0 / (1.0 + jnp.exp(-x))


def _ln_stats(z):
    mu = jnp.mean(z, axis=-1, keepdims=True)
    zc = z - mu
    var = jnp.mean(zc * zc, axis=-1, keepdims=True)
    rstd = lax.rsqrt(var + LN_EPS)
    return zc * rstd, rstd


def _ln_bwd(dout, xhat, rstd, gamma):
    dxh = dout * gamma
    m1 = jnp.mean(dxh, axis=-1, keepdims=True)
    m2 = jnp.mean(dxh * xhat, axis=-1, keepdims=True)
    return rstd * (dxh - m1 - xhat * m2)


def _colsum(v):
    return jnp.sum(v, axis=0, keepdims=True)


def _f_chunks(f, width=None):
    if width is not None and f % width == 0:
        return f // width, width
    n = 2 if f >= 2048 and f % 256 == 0 else 1
    return n, f // n


def _dot_nt(a, b):
    return lax.dot_general(a, b, (((1,), (1,)), ((), ())), preferred_element_type=F32)


def ffn_up(xin, gam, bet, wgt, wut, *, tm):
    t, d = xin.shape
    f = wgt.shape[0]
    nch, fc = _f_chunks(f, FFN_CHUNK)

    def body(x_ref, g_ref, b_ref, wg_ref, wu_ref, xb_ref, go_ref, uo_ref, h_ref):
        xb = (x_ref[...] * g_ref[...] + b_ref[...]).astype(MM)
        xb_ref[...] = xb
        for c in range(nch):
            sl = slice(c * fc, (c + 1) * fc)
            g = _dot_nt(xb, wg_ref[sl, :])
            u = _dot_nt(xb, wu_ref[sl, :])
            go_ref[:, sl] = g.astype(MM)
            uo_ref[:, sl] = u.astype(MM)
            h_ref[:, sl] = (g * _sig(g) * u).astype(MM)

    return _call(
        body, name="ffn_up", grid=(t // tm,), sem=("parallel",),
        in_specs=[_rows(tm, d), _const((1, d)), _const((1, d)), _const((f, d)), _const((f, d))],
        out_specs=[_rows(tm, d), _rows(tm, f), _rows(tm, f), _rows(tm, f)],
        out_shape=[jax.ShapeDtypeStruct((t, d), MM)] + [jax.ShapeDtypeStruct((t, f), MM)] * 3,
    )(xin, gam, bet, wgt, wut)


def mm_res_ln(a, w, xin, gam, bet, *, alpha, scale, tm):
    t, k = a.shape
    d = w.shape[1]

    def body(a_ref, w_ref, x_ref, g_ref, b_ref, xh_ref, rs_ref):
        y = jnp.dot(a_ref[...], w_ref[...], preferred_element_type=F32)
        x = x_ref[...] * g_ref[...] + b_ref[...]
        xh, rstd = _ln_stats(alpha * x + scale * y)
        xh_ref[...] = xh
        rs_ref[...] = rstd

    return _call(
        body, name="mm_res_ln", grid=(t // tm,), sem=("parallel",),
        in_specs=[_rows(tm, k), _const((k, d)), _rows(tm, d), _const((1, d)), _const((1, d))],
        out_specs=[_rows(tm, d), _rows(tm, 1)],
        out_shape=[jax.ShapeDtypeStruct((t, d), F32), jax.ShapeDtypeStruct((t, 1), F32)],
    )(a, w, xin, gam, bet)


def mm_in(xin, gam, bet, wt, *, tm):
    t, d = xin.shape
    n = wt.shape[0]

    def body(x_ref, g_ref, b_ref, w_ref, o_ref, xb_ref):
        xb = (x_ref[...] * g_ref[...] + b_ref[...]).astype(MM)
        xb_ref[...] = xb
        o_ref[...] = _dot_nt(xb, w_ref[...]).astype(MM)

    return _call(
        body, name="mm_in", grid=(t // tm,), sem=("parallel",),
        in_specs=[_rows(tm, d), _const((1, d)), _const((1, d)), _const((n, d))],
        out_specs=[_rows(tm, n), _rows(tm, d)],
        out_shape=[jax.ShapeDtypeStruct((t, n), MM), jax.ShapeDtypeStruct((t, d), MM)],
    )(xin, gam, bet, wt)


def loss_head(xh, gam, bet, target, *, tm):
    t, d = xh.shape

    def body(x_ref, g_ref, b_ref, t_ref, dy_ref, l_ref):
        @pl.when(pl.program_id(0) == 0)
        def _():
            l_ref[...] = jnp.zeros_like(l_ref)

        e = x_ref[...] * g_ref[...] + b_ref[...] - t_ref[...]
        dy_ref[...] = e * (1.0 / d)
        l_ref[...] += jnp.sum(_colsum(e * e), axis=1, keepdims=True)

    return _call(
        body, name="loss_head", grid=(t // tm,), sem=("arbitrary",),
        in_specs=[_rows(tm, d), _const((1, d)), _const((1, d)), _rows(tm, d)],
        out_specs=[_rows(tm, d), _acc((1, 1))],
        out_shape=[jax.ShapeDtypeStruct((t, d), F32), jax.ShapeDtypeStruct((1, 1), F32)],
    )(xh, gam, bet, target)


def _halo_specs(tt, ncols, t):
    per, last = tt // HALO, t // HALO - 1
    return [pl.BlockSpec((HALO, ncols), lambda i: (jnp.maximum(i * per - 1, 0), 0)),
            pl.BlockSpec((tt, ncols), lambda i: (i, 0)),
            pl.BlockSpec((HALO, ncols), lambda i: (jnp.minimum((i + 1) * per, last), 0))]


def _fill_ext(ext, prev_ref, cur_ref, next_ref, i, nt, tt):
    ext[0:HALO, :] = jnp.where(i > 0, prev_ref[...].astype(F32), 0.0)
    ext[HALO:HALO + tt, :] = cur_ref[...].astype(F32)
    ext[HALO + tt:HALO + tt + HALO, :] = jnp.where(i < nt - 1, next_ref[...].astype(F32), 0.0)


def _make_shifts(sh, n):
    for r in range(1, 8):
        sh[r, 0:n, :] = sh[0, r:r + n, :]


def _shift_reader(sh):
    def read(o, rows):
        r = o % 8
        return sh[r, o - r:o - r + rows, :]
    return read


def _conv_chunks_shifted(sh, w_ref, ktaps, src0, nrows, flip=False):
    for r0 in range(0, nrows, CONV_ROWS):
        acc = None
        base = src0 + r0
        for r in range(8):
            ks = [k for k in range(ktaps) if (base + k) % 8 == r]
            if not ks:
                continue
            lo = base + ks[0] - r
            slab = sh[r, lo:lo + CONV_ROWS + 8 * (len(ks) - 1), :]
            for q, k in enumerate(ks):
                kk = ktaps - 1 - k if flip else k
                term = w_ref[kk:kk + 1, :] * slab[8 * q:8 * q + CONV_ROWS, :]
                acc = term if acc is None else acc + term
        yield r0, acc


def _ref_reader(ref):
    def read(o, rows):
        return ref[o:o + rows, :]
    return read


def _conv_chunks(read, w_ref, ktaps, src0, nrows, flip=False):
    for r0 in range(0, nrows, CONV_ROWS):
        acc = None
        for k in range(ktaps):
            kk = ktaps - 1 - k if flip else k
            term = w_ref[kk:kk + 1, :] * read(src0 + r0 + k, CONV_ROWS)
            acc = term if acc is None else acc + term
        yield r0, acc


def _pool_groups(nrows, p):
    lane = lax.broadcasted_iota(jnp.int32, (nrows, p), 1)
    g = p // 4
    return lane < g, lane < 2 * g, lane < 3 * g


def _pool_select(groups, v2, v4, v8, v16):
    g0, g1, g2 = groups
    return jnp.where(g0, v2, jnp.where(g1, v4, jnp.where(g2, v8, v16)))


def _pool_count(groups, i, tt, row0, nrows, p, t):
    pos = i * tt + (row0 - HALO) + lax.broadcasted_iota(jnp.int32, (nrows, p), 0)
    half = _pool_select(groups, 1, 2, 4, 8)
    lo = jnp.clip(pos - half, 0, t)
    hi = jnp.clip(pos + half, 0, t)
    return jnp.maximum(hi - lo, 1).astype(F32)


def _pool_forward(ext, s2, s4, s8, groups, cnt, tt, p):
    e = tt + 2 * HALO
    s2[8:e - 8, :] = ext[7:e - 9, 0:p] + ext[8:e - 8, 0:p]
    s4[16:e - 16, :] = s2[15:e - 17, :] + s2[17:e - 15, :]
    s8[24:e - 24, :] = s4[22:e - 26, :] + s4[26:e - 22, :]
    s16 = s8[28:e - 36, :] + s8[36:e - 28, :]
    c = slice(HALO, HALO + tt)
    tot = _pool_select(groups, s2[c, :], s4[c, :], s8[c, :], s16)
    return tot / cnt - ext[c, 0:p]


def mixer_fwd(proj, wbd, pscale, sw, cw, cb, cg, cbt, *, d, tt):
    t, ncols = proj.shape
    p, s = d // 4, 3 * d // 8
    o_gb, o_gc, o_v, o_cv, o_cg = p, p + s, p + 2 * s, p + 3 * s, p + 4 * s
    nt, e = t // tt, tt + 2 * HALO

    def body(prev_ref, cur_ref, next_ref, wbd_ref, ps_ref, sw_ref, cw_ref, cb_ref, cg_ref, cbt_ref,
             cat_ref, conv_ref, ext, a_sh, cv_s, s2, s4, s8):
        i = pl.program_id(0)
        _fill_ext(ext, prev_ref, cur_ref, next_ref, i, nt, tt)
        a_sh[0, 0:e, :] = ext[:, o_cv:o_cv + s] * _sig(ext[:, o_cg:o_cg + s])
        a_sh[0, e:e + 8, :] = jnp.zeros((8, s), F32)
        _make_shifts(a_sh, e)
        for r0, acc in _conv_chunks_shifted(a_sh, cw_ref, CCONV_K, HALO - CCONV_K // 2, tt):
            b = acc + cb_ref[...]
            conv_ref[r0:r0 + CONV_ROWS, :] = b
            n, _ = _ln_stats(b)
            yn = n * cg_ref[...] + cbt_ref[...]
            cat_ref[r0:r0 + CONV_ROWS, p + s:d] = (yn * _sig(yn)).astype(MM)
        cv_s[...] = ext[:, o_gc:o_gc + s] * ext[:, o_v:o_v + s]
        for r0, acc in _conv_chunks(_ref_reader(cv_s), sw_ref, SCONV_K, HALO - SCONV_K // 2, tt):
            gb = ext[HALO + r0:HALO + r0 + CONV_ROWS, o_gb:o_gb + s]
            cat_ref[r0:r0 + CONV_ROWS, p:p + s] = (gb * acc).astype(MM)
        groups = _pool_groups(tt, p)
        cnt = _pool_count(groups, i, tt, HALO, tt, p, t)
        pooled = _pool_forward(ext, s2, s4, s8, groups, cnt, tt, p)
        ya = jnp.dot(pooled.astype(MM), wbd_ref[...], preferred_element_type=F32) * ps_ref[...]
        cat_ref[:, 0:p] = ya.astype(MM)

    return _call(
        body, name="mixer_fwd", grid=(nt,), sem=("parallel",),
        in_specs=_halo_specs(tt, ncols, t) + [_const((p, p)), _const((1, p)), _const((8, s)), _const((32, s)),
                                               _const((1, s)), _const((1, s)), _const((1, s))],
        out_specs=[_rows(tt, d), _rows(tt, s)],
        out_shape=[jax.ShapeDtypeStruct((t, d), MM), jax.ShapeDtypeStruct((t, s), F32)],
        scratch=[pltpu.VMEM((e, ncols), F32), pltpu.VMEM((8, e + 8, s), F32), pltpu.VMEM((e, s), F32),
                 pltpu.VMEM((e, p), F32), pltpu.VMEM((e, p), F32), pltpu.VMEM((e, p), F32)],
    )(proj, proj, proj, wbd, pscale, sw, cw, cb, cg, cbt)


def mixer_bwd(proj, dcat, conv, wbd, wbdt, pscale, sw, cw, cg, cbt, *, d, tt):
    t, ncols = proj.shape
    p, s = d // 4, 3 * d // 8
    o_gb, o_gc, o_v, o_cv, o_cg = p, p + s, p + 2 * s, p + 3 * s, p + 4 * s
    nt, e = t // tt, tt + 2 * HALO

    def body(pp_ref, pc_ref, pn_ref, dp_ref, dc_ref, dn_ref, bp_ref, bc_ref, bn_ref, wbd_ref, wbdt_ref, ps_ref,
             sw_ref, cw_ref, cg_ref, cbt_ref,
             dproj_ref, dwbd_ref, dps_ref, dsw_ref, dcw_ref, dcb_ref, dcg_ref, dcbt_ref,
             ext, dext, bext, a_sh, b_sh, sg_s, cv_s, ds_s, q_s, r2, r4, r8):
        i = pl.program_id(0)

        @pl.when(i == 0)
        def _():
            for ref in (dwbd_ref, dps_ref, dsw_ref, dcw_ref, dcb_ref, dcg_ref, dcbt_ref):
                ref[...] = jnp.zeros_like(ref)

        _fill_ext(ext, pp_ref, pc_ref, pn_ref, i, nt, tt)
        _fill_ext(dext, dp_ref, dc_ref, dn_ref, i, nt, tt)
        _fill_ext(bext, bp_ref, bc_ref, bn_ref, i, nt, tt)
        c = slice(HALO, HALO + tt)

        sg_s[...] = _sig(ext[:, o_cg:o_cg + s])
        a_sh[0, 0:e, :] = ext[:, o_cv:o_cv + s] * sg_s[...]
        a_sh[0, e:e + 8, :] = jnp.zeros((8, s), F32)
        _make_shifts(a_sh, e)
        read_a = _shift_reader(a_sh)
        for r0 in range(0, e, CONV_ROWS):
            b = bext[r0:r0 + CONV_ROWS, :]
            n, rstd = _ln_stats(b)
            yn = n * cg_ref[...] + cbt_ref[...]
            sy = _sig(yn)
            rows = slice(r0, r0 + CONV_ROWS)
            dyn = dext[rows, p + s:d] * (sy * (1.0 + yn * (1.0 - sy)))
            db = _ln_bwd(dyn, n, rstd, cg_ref[...])
            b_sh[0, rows, :] = db
            if HALO <= r0 < HALO + tt:
                dcg_ref[...] += _colsum(dyn * n)
                dcbt_ref[...] += _colsum(dyn)
                dcb_ref[...] += _colsum(db)
        b_sh[0, e:e + 8, :] = jnp.zeros((8, s), F32)
        _make_shifts(b_sh, e)
        for r0, da in _conv_chunks_shifted(b_sh, cw_ref, CCONV_K, HALO - CCONV_K // 2, tt, flip=True):
            rows = slice(HALO + r0, HALO + r0 + CONV_ROWS)
            sg = sg_s[rows, :]
            dproj_ref[r0:r0 + CONV_ROWS, o_cv:o_cv + s] = (da * sg).astype(MM)
            dproj_ref[r0:r0 + CONV_ROWS, o_cg:o_cg + s] = (
                da * ext[rows, o_cv:o_cv + s] * sg * (1.0 - sg)).astype(MM)
        for k in range(CCONV_K):
            lo = HALO + k - CCONV_K // 2
            dcw_ref[k:k + 1, :] += _colsum(b_sh[0, c, :] * read_a(lo, tt))

        cv_s[...] = ext[:, o_gc:o_gc + s] * ext[:, o_v:o_v + s]
        ds_s[...] = dext[:, p:p + s] * ext[:, o_gb:o_gb + s]
        for r0, acc in _conv_chunks(_ref_reader(cv_s), sw_ref, SCONV_K, HALO - SCONV_K // 2, tt):
            rows = slice(HALO + r0, HALO + r0 + CONV_ROWS)
            dproj_ref[r0:r0 + CONV_ROWS, o_gb:o_gb + s] = (dext[rows, p:p + s] * acc).astype(MM)
        for r0, dcv in _conv_chunks(_ref_reader(ds_s), sw_ref, SCONV_K, HALO - SCONV_K // 2, tt, flip=True):
            rows = slice(HALO + r0, HALO + r0 + CONV_ROWS)
            dproj_ref[r0:r0 + CONV_ROWS, o_gc:o_gc + s] = (dcv * ext[rows, o_v:o_v + s]).astype(MM)
            dproj_ref[r0:r0 + CONV_ROWS, o_v:o_v + s] = (dcv * ext[rows, o_gc:o_gc + s]).astype(MM)
        for k in range(SCONV_K):
            lo = HALO + k - SCONV_K // 2
            dsw_ref[k:k + 1, :] += _colsum(ds_s[c, :] * cv_s[lo:lo + tt, :])

        groups = _pool_groups(tt, p)
        cnt = _pool_count(groups, i, tt, HALO, tt, p, t)
        pooled = _pool_forward(ext, r2, r4, r8, groups, cnt, tt, p).astype(MM)
        ta = jnp.dot(pooled, wbd_ref[...], preferred_element_type=F32)
        dps_ref[...] += _colsum(dext[c, 0:p] * ta)
        dta = (dext[:, 0:p] * ps_ref[...]).astype(MM)
        dwbd_ref[...] += lax.dot_general(pooled, dta[HALO:HALO + tt, :], (((0,), (0,)), ((), ())),
                                         preferred_element_type=F32)
        dpool = jnp.dot(dta, wbdt_ref[...], preferred_element_type=F32)
        groups_e = _pool_groups(e, p)
        q_s[...] = dpool / _pool_count(groups_e, i, tt, 0, e, p, t)
        r2[8:e - 8, :] = q_s[8:e - 8, :] + q_s[9:e - 7, :]
        r4[16:e - 16, :] = r2[15:e - 17, :] + r2[17:e - 15, :]
        r8[24:e - 24, :] = r4[22:e - 26, :] + r4[26:e - 22, :]
        r16 = r8[28:e - 36, :] + r8[36:e - 28, :]
        du = _pool_select(groups, r2[c, :], r4[c, :], r8[c, :], r16) - dpool[HALO:HALO + tt, :]
        dproj_ref[:, 0:p] = du.astype(MM)

    small = [(p, p), (1, p), (8, s), (32, s), (1, s), (1, s), (1, s)]
    return _call(
        body, name="mixer_bwd", grid=(nt,), sem=("arbitrary",),
        in_specs=_halo_specs(tt, ncols, t) + _halo_specs(tt, d, t) + _halo_specs(tt, s, t) + [
            _const((p, p)), _const((p, p)), _const((1, p)), _const((8, s)), _const((32, s)),
            _const((1, s)), _const((1, s))],
        out_specs=[_rows(tt, ncols)] + [_acc(sh) for sh in small],
        out_shape=[jax.ShapeDtypeStruct((t, ncols), MM)] + [jax.ShapeDtypeStruct(sh, F32) for sh in small],
        scratch=[pltpu.VMEM((e, ncols), F32), pltpu.VMEM((e, d), F32), pltpu.VMEM((e, s), F32),
                 pltpu.VMEM((8, e + 8, s), F32), pltpu.VMEM((8, e + 8, s), F32)]
        + [pltpu.VMEM((e, s), F32)] * 3 + [pltpu.VMEM((e, p), F32)] * 4,
    )(proj, proj, proj, dcat, dcat, dcat, conv, conv, conv, wbd, wbdt, pscale, sw, cw, cg, cbt)


def ffn_bwd(dout, xh, rs, gam, wd, wgt, wut, g, u, *, alpha, tm):
    t, d = dout.shape
    f = g.shape[1]
    nch, fc = _f_chunks(f, FFN_CHUNK)

    def body(do_ref, xh_ref, rs_ref, gm_ref, wd_ref, wgt_ref, wut_ref, g_ref, u_ref,
             dx_ref, dyb_ref, dg_ref, du_ref, dgam_ref, dbet_ref):
        @pl.when(pl.program_id(0) == 0)
        def _():
            dgam_ref[...] = jnp.zeros_like(dgam_ref)
            dbet_ref[...] = jnp.zeros_like(dbet_ref)

        dout_v, xhat = do_ref[...], xh_ref[...]
        dgam_ref[...] += _colsum(dout_v * xhat)
        dbet_ref[...] += _colsum(dout_v)
        dz = _ln_bwd(dout_v, xhat, rs_ref[...], gm_ref[...])
        dyb = (0.5 * dz).astype(MM)
        dyb_ref[...] = dyb
        for c in range(nch):
            sl = slice(c * fc, (c + 1) * fc)
            dh = _dot_nt(dyb, wd_ref[sl, :])
            gv, uv = g_ref[:, sl].astype(F32), u_ref[:, sl].astype(F32)
            sg = _sig(gv)
            dg_ref[:, sl] = (dh * uv * (sg * (1.0 + gv * (1.0 - sg)))).astype(MM)
            du_ref[:, sl] = (dh * (gv * sg)).astype(MM)
        dx_ref[...] = (alpha * dz + jnp.dot(dg_ref[...], wgt_ref[...], preferred_element_type=F32)
                       + jnp.dot(du_ref[...], wut_ref[...], preferred_element_type=F32))

    return _call(
        body, name="ffn_bwd", grid=(t // tm,), sem=("arbitrary",),
        in_specs=[_rows(tm, d), _rows(tm, d), _rows(tm, 1), _const((1, d)), _const((f, d)), _const((f, d)),
                  _const((f, d)), _rows(tm, f), _rows(tm, f)],
        out_specs=[_rows(tm, d), _rows(tm, d), _rows(tm, f), _rows(tm, f), _acc((1, d)), _acc((1, d))],
        out_shape=[jax.ShapeDtypeStruct((t, d), F32), jax.ShapeDtypeStruct((t, d), MM),
                   jax.ShapeDtypeStruct((t, f), MM), jax.ShapeDtypeStruct((t, f), MM),
                   jax.ShapeDtypeStruct((1, d), F32), jax.ShapeDtypeStruct((1, d), F32)],
    )(dout, xh, rs, gam, wd, wgt, wut, g, u)


def lnbwd_mm(dout, xh, rs, gam, w, *, tm):
    t, d = dout.shape
    n = w.shape[0]

    def body(do_ref, xh_ref, rs_ref, gm_ref, wt_ref, dz_ref, dzb_ref, da_ref, dgam_ref, dbet_ref):
        @pl.when(pl.program_id(0) == 0)
        def _():
            dgam_ref[...] = jnp.zeros_like(dgam_ref)
            dbet_ref[...] = jnp.zeros_like(dbet_ref)

        dout_v, xhat = do_ref[...], xh_ref[...]
        dgam_ref[...] += _colsum(dout_v * xhat)
        dbet_ref[...] += _colsum(dout_v)
        dz = _ln_bwd(dout_v, xhat, rs_ref[...], gm_ref[...])
        dz_ref[...] = dz
        dzb = dz.astype(MM)
        dzb_ref[...] = dzb
        da_ref[...] = _dot_nt(dzb, wt_ref[...]).astype(MM)

    return _call(
        body, name="lnbwd_mm", grid=(t // tm,), sem=("arbitrary",),
        in_specs=[_rows(tm, d), _rows(tm, d), _rows(tm, 1), _const((1, d)), _const((n, d))],
        out_specs=[_rows(tm, d), _rows(tm, d), _rows(tm, n), _acc((1, d)), _acc((1, d))],
        out_shape=[jax.ShapeDtypeStruct((t, d), F32), jax.ShapeDtypeStruct((t, d), MM),
                   jax.ShapeDtypeStruct((t, n), MM), jax.ShapeDtypeStruct((1, d), F32),
                   jax.ShapeDtypeStruct((1, d), F32)],
    )(dout, xh, rs, gam, w)


def mm_add(a, w, r, *, alpha, tm):
    t, k = a.shape
    d = w.shape[1]

    def body(a_ref, w_ref, r_ref, o_ref):
        o_ref[...] = jnp.dot(a_ref[...], w_ref[...], preferred_element_type=F32) + alpha * r_ref[...]

    return _call(
        body, name="mm_add", grid=(t // tm,), sem=("parallel",),
        in_specs=[_rows(tm, k), _const((k, d)), _rows(tm, d)],
        out_specs=_rows(tm, d),
        out_shape=jax.ShapeDtypeStruct((t, d), F32),
    )(a, w, r)


def tn_matmul(a, b, *, tm, after=None, into=None):
    t, n = a.shape
    d = b.shape[1]
    nch, nc = _f_chunks(n)
    steps = t // tm
    per = n // N_DEV
    slots = N_DEV // nch

    def body(a_ref, b_ref, *rest):
        o_ref, ob_ref = rest[-2:]

        @pl.when(pl.program_id(1) == 0)
        def _():
            o_ref[...] = jnp.zeros_like(o_ref)

        o_ref[...] += lax.dot_general(a_ref[...], b_ref[...], (((0,), (0,)), ((), ())),
                                      preferred_element_type=F32)

        @pl.when(pl.program_id(1) == steps - 1)
        def _():
            if into is None:
                ob_ref[...] = o_ref[...].astype(MM)
            else:
                for s in range(slots):
                    ob_ref[s] = o_ref[s * per:(s + 1) * per, :].astype(MM)

    extra = [] if after is None else [after]
    if into is None:
        ob_spec, ob_shape, alias = pl.BlockSpec((nc, d), lambda j, i: (j, 0)), jax.ShapeDtypeStruct((n, d), MM), {}
    else:
        buf, off = into
        assert off % per == 0 and buf.shape[0] == N_DEV and buf.dtype == MM
        extra.append(buf)
        ob_spec = pl.BlockSpec((slots, per, d), lambda j, i: (j, off // per, 0))
        ob_shape, alias = jax.ShapeDtypeStruct(buf.shape, MM), {1 + len(extra): 1}
    return _call(
        body, name="tn_matmul", grid=(nch, steps), sem=("parallel", "arbitrary"), aliases=alias,
        in_specs=[pl.BlockSpec((tm, nc), lambda j, i: (i, j)), pl.BlockSpec((tm, d), lambda j, i: (i, 0))]
        + [pl.BlockSpec(memory_space=pl.ANY)] * len(extra),
        out_specs=[pl.BlockSpec((nc, d), lambda j, i: (j, 0)), ob_spec],
        out_shape=[jax.ShapeDtypeStruct((n, d), F32), ob_shape],
    )(a, b, *extra)


def sum_parts(own, recv):
    r, d = own.shape
    n = recv.shape[0]
    tr = _row_tile(r, 256)

    def body(own_ref, recv_ref, o_ref):
        acc = own_ref[...]
        for k in range(n):
            acc = acc + recv_ref[k].astype(F32)
        o_ref[...] = acc

    return _call(
        body, name="sum_parts", grid=(r // tr,), sem=("parallel",),
        in_specs=[_rows(tr, d), pl.BlockSpec((n, tr, d), lambda i: (0, i, 0))],
        out_specs=_rows(tr, d),
        out_shape=jax.ShapeDtypeStruct((r, d), F32),
    )(own, recv)


def sum_gathered(parts):
    _, r, n = parts.shape

    def body(p_ref, o_ref):
        acc = p_ref[0]
        for j in range(1, N_DEV):
            acc = acc + p_ref[j]
        o_ref[...] = acc

    return _call(
        body, name="sum_gathered",
        in_specs=[pl.BlockSpec(memory_space=pltpu.VMEM)], out_specs=pl.BlockSpec(memory_space=pltpu.VMEM),
        out_shape=jax.ShapeDtypeStruct((r, n), F32),
    )(parts)


def adamw(w, g, m, v):
    r, c = w.shape
    tr = _row_tile(r, 512)

    def body(w_ref, g_ref, m_ref, v_ref, d_ref, mo_ref, vo_ref):
        gv = g_ref[...]
        mn = ADAM_B1 * m_ref[...] + (1.0 - ADAM_B1) * gv
        vn = ADAM_B2 * v_ref[...] + (1.0 - ADAM_B2) * (gv * gv)
        m_hat = mn / (1.0 - ADAM_B1 ** ADAM_STEP)
        v_hat = vn / (1.0 - ADAM_B2 ** ADAM_STEP)
        d_ref[...] = -ADAM_LR * (m_hat / (jnp.sqrt(v_hat) + ADAM_EPS) + ADAM_WD * w_ref[...])
        mo_ref[...] = mn
        vo_ref[...] = vn

    return _call(
        body, name="adamw", grid=(r // tr,), sem=("parallel",),
        in_specs=[_rows(tr, c)] * 4, out_specs=[_rows(tr, c)] * 3,
        out_shape=[jax.ShapeDtypeStruct((r, c), F32)] * 3,
    )(w, g, m, v)


def _mesh_pos():
    return lax.axis_index("x"), lax.axis_index("y"), lax.axis_index("c")


def _two_level_gather(x_ref, out_ref, send_sems, recv_sems, local_sem):
    x, y, c = _mesh_pos()
    me, sibling = (x, y, c), (x, y, 1 - c)
    chips = [(1 - x, y), (x, 1 - y), (1 - x, 1 - y)]

    def slot(px, py, pc):
        return out_ref.at[4 * px + 2 * py + pc]

    def copy(k, block, to, src=None):
        return pltpu.make_async_remote_copy(
            src_ref=slot(*block) if src is None else src, dst_ref=slot(*block),
            send_sem=send_sems.at[k], recv_sem=recv_sems.at[k], device_id=to, device_id_type=MESH)

    mine = pltpu.make_async_copy(x_ref, slot(*me), local_sem)
    mine.start()
    first = [copy(1 + j, me, (*chip, c), src=x_ref) for j, chip in enumerate(chips)]
    first.append(copy(0, me, sibling, src=x_ref))
    for cp in first:
        cp.start()
    passed = [copy(4 + j, (*chip, c), sibling) for j, chip in enumerate(chips)]
    for j, chip in enumerate(chips):
        copy(1 + j, (*chip, c), me).wait_recv()
        passed[j].start()
    copy(0, sibling, me).wait_recv()
    for j, chip in enumerate(chips):
        copy(4 + j, (*chip, 1 - c), me).wait_recv()
    for cp in first + passed:
        cp.wait_send()
    mine.wait()


def gather_small(x):
    return _call(
        _two_level_gather_body(), name="gather_small",
        in_specs=[pl.BlockSpec(memory_space=pltpu.VMEM)], out_specs=pl.BlockSpec(memory_space=pltpu.VMEM),
        out_shape=jax.ShapeDtypeStruct((N_DEV,) + x.shape, x.dtype),
        scratch=[pltpu.SemaphoreType.DMA((7,)), pltpu.SemaphoreType.DMA((7,)), pltpu.SemaphoreType.DMA(())],
    )(x)


def _two_level_gather_body():
    def body(x_ref, out_ref, send_sems, recv_sems, local_sem):
        _two_level_gather(x_ref, out_ref, send_sems, recv_sems, local_sem)
    return body


_HBM = pl.BlockSpec(memory_space=pltpu.HBM)
_SEM = pl.BlockSpec(memory_space=pltpu.SEMAPHORE)
_EFFECT = pltpu.SideEffectType.DATAFLOW_SIDE_EFFECTING
_RELATIONS = (4, 2, 6, 5, 3, 7, 1)


def _split_copy(src_ref, land_ref, send_sems, recv_sems, k, gather):
    x, y, c = _mesh_pos()
    px, py, pc = (1 - x if k & 4 else x), (1 - y if k & 2 else y), (1 - c if k & 1 else c)
    if gather:
        src, dst = src_ref, land_ref.at[4 * x + 2 * y + c]
    else:
        src, dst = src_ref.at[4 * px + 2 * py + pc], land_ref.at[k - 1]
    return pltpu.make_async_remote_copy(src_ref=src, dst_ref=dst, send_sem=send_sems.at[k - 1],
                                        recv_sem=recv_sems.at[k - 1], device_id=(px, py, pc), device_id_type=MESH)


def copy_start(src, land_shape, *, gather, name):
    def body(src_ref, land_ref, send_sems, recv_sems, src_thru, land_thru, token):
        for k in _RELATIONS:
            _split_copy(src_ref, land_ref, send_sems, recv_sems, k, gather).start()
        token[...] = jnp.zeros_like(token)

    return pl.pallas_call(
        body, name=name,
        out_shape=(pltpu.SemaphoreType.DMA((7,)), pltpu.SemaphoreType.DMA((7,)), pltpu.HBM(src.shape, src.dtype),
                   pltpu.HBM(land_shape, src.dtype), jax.ShapeDtypeStruct((8, 128), F32)),
        in_specs=(_HBM, _HBM), out_specs=(_SEM, _SEM, _HBM, _HBM, pl.BlockSpec(memory_space=pltpu.VMEM)),
        input_output_aliases={0: 2, 1: 3},
        compiler_params=pltpu.CompilerParams(has_side_effects=_EFFECT),
    )(pltpu.with_memory_space_constraint(src, pltpu.HBM),
      pltpu.with_memory_space_constraint(lax.empty(land_shape, src.dtype), pltpu.HBM))


def copy_wait(started, after, *, gather, name):
    send_sems, recv_sems, src_thru, land_thru, _ = started

    def body(src_ref, land_ref, send_sems, recv_sems, after_ref, src_dead, got_ref):
        for k in _RELATIONS:
            cp = _split_copy(src_ref, land_ref, send_sems, recv_sems, k, gather)
            cp.wait_send()
            cp.wait_recv()

    return pl.pallas_call(
        body, name=name,
        out_shape=(pltpu.HBM(src_thru.shape, src_thru.dtype), pltpu.HBM(land_thru.shape, land_thru.dtype)),
        in_specs=(_HBM, _HBM, _SEM, _SEM, pl.BlockSpec(memory_space=pl.ANY)), out_specs=(_HBM, _HBM),
        input_output_aliases={0: 0, 1: 1},
        compiler_params=pltpu.CompilerParams(has_side_effects=_EFFECT),
    )(src_thru, land_thru, send_sems, recv_sems, after)


WEIGHTS = ['ln1_g', 'ln1_b', 'ffn1_w_gate', 'ffn1_w_up', 'ffn1_w_down', 'mix_w_in', 'pool_w', 'pool_scale',
           'sconv_w', 'cconv_w', 'cconv_b', 'cnorm_g', 'cnorm_b', 'mix_w_out', 'ln2_g', 'ln2_b',
           'ffn2_w_gate', 'ffn2_w_up', 'ffn2_w_down', 'ln3_g', 'ln3_b']


def _pack_small(pieces):
    flat = jnp.concatenate([p.reshape(-1) for p in pieces])
    pad = -flat.shape[0] % 1024
    return jnp.pad(flat, (0, pad)).reshape(-1, 128)


def _unpack_small(flat, shapes):
    out, off = [], 0
    for sh in shapes:
        n = 1
        for s in sh:
            n *= s
        out.append(flat[off:off + n].reshape(sh))
        off += n
    return out


def _pad_rows(a, rows):
    return jnp.pad(a, ((0, rows - a.shape[0]), (0, 0)))


def kernel(x, ln1_g, ln1_b, ffn1_w_gate, ffn1_w_up, ffn1_w_down, mix_w_in, pool_w, pool_scale, sconv_w, cconv_w, cconv_b, cnorm_g, cnorm_b, mix_w_out, ln2_g, ln2_b, ffn2_w_gate, ffn2_w_up, ffn2_w_down, ln3_g, ln3_b, loss_target, m_ln1_g, m_ln1_b, m_ffn1_w_gate, m_ffn1_w_up, m_ffn1_w_down, m_mix_w_in, m_pool_w, m_pool_scale, m_sconv_w, m_cconv_w, m_cconv_b, m_cnorm_g, m_cnorm_b, m_mix_w_out, m_ln2_g, m_ln2_b, m_ffn2_w_gate, m_ffn2_w_up, m_ffn2_w_down, m_ln3_g, m_ln3_b, v_ln1_g, v_ln1_b, v_ffn1_w_gate, v_ffn1_w_up, v_ffn1_w_down, v_mix_w_in, v_pool_w, v_pool_scale, v_sconv_w, v_cconv_w, v_cconv_b, v_cnorm_g, v_cnorm_b, v_mix_w_out, v_ln2_g, v_ln2_b, v_ffn2_w_gate, v_ffn2_w_up, v_ffn2_w_down, v_ln3_g, v_ln3_b):
    a = dict(locals())
    depth, d = ln1_g.shape
    t = x.shape[1]
    fs = ffn1_w_gate.shape[2]
    f = fs * N_DEV
    ins = mix_w_in.shape[2]
    ncols = ins * N_DEV
    outs = mix_w_out.shape[1]
    p, s = d // 4, 3 * d // 8
    pg = p // 4
    cs = sconv_w.shape[2]
    alpha = (2.0 * depth) ** 0.25
    me = 4 * lax.axis_index("x") + 2 * lax.axis_index("y") + lax.axis_index("c")
    tm = min(512, t)
    tm_bwd = min(256, t)
    tm_tn = min(2048, t)
    tm_down = min(1024, t)
    tt_fwd = min(1024, t)
    tt_bwd = min(512, t)

    sizes = {"wg1": fs, "wu1": fs, "wd1": fs, "win": ins, "wout": outs, "wg2": fs, "wu2": fs, "wd2": fs,
             "conv": 16}
    per_layer = ["wg1", "wu1", "wd1", "win", "wout", "wg2", "wu2", "wd2"]
    wire = jnp.dtype(F32).itemsize // jnp.dtype(MM).itemsize
    n_conv = (SCONV_K + CCONV_K) * cs

    def conv_rows(l):
        flat = jnp.concatenate([sconv_w[l].reshape(-1), cconv_w[l].reshape(-1)])
        bits = lax.bitcast_convert_type(flat, MM).reshape(-1)
        return jnp.pad(bits, (0, sizes["conv"] * d - bits.shape[0])).reshape(sizes["conv"], d)

    def layer_block(l, key):
        if key == "conv":
            return conv_rows(l)
        return {"wg1": ffn1_w_gate[l].T, "wu1": ffn1_w_up[l].T, "wd1": ffn1_w_down[l], "win": mix_w_in[l].T,
                "wout": mix_w_out[l], "wg2": ffn2_w_gate[l].T, "wu2": ffn2_w_up[l].T,
                "wd2": ffn2_w_down[l]}[key].astype(MM)

    def gather_groups(l):
        if l == 0:
            return [["wg1", "wu1"], ["wd1"], ["win", "wout", "wg2", "wu2", "wd2", "conv"]]
        return [per_layer + ["conv"]]

    gathers = {}
    for l in range(depth):
        for gi, keys in enumerate(gather_groups(l)):
            src = jnp.concatenate([layer_block(l, k) for k in keys], axis=0)
            gathers[l, gi] = copy_start(src, (N_DEV,) + src.shape, gather=True, name=f"gather_start_{l}_{gi}")
    started = sum(st[4][0:1, 0:1] for st in gathers.values())

    def gathered(l, gi, after):
        mine, land = copy_wait(gathers[l, gi], after, gather=True, name=f"gather_wait_{l}_{gi}")
        wall = lax.dynamic_update_slice(land, mine[None], (me, 0, 0))
        out, off = {}, 0
        for k in gather_groups(l)[gi]:
            out[k] = wall[:, off:off + sizes[k], :].reshape(N_DEV * sizes[k], d)
            off += sizes[k]
        return out

    def conv_filters(rows):
        bits = rows.reshape(N_DEV, -1)[:, :n_conv * wire]
        vals = lax.bitcast_convert_type(bits.reshape(N_DEV, n_conv, wire) if wire > 1 else bits, F32)
        both = vals.reshape(N_DEV, SCONV_K + CCONV_K, cs).transpose(1, 0, 2).reshape(SCONV_K + CCONV_K, s)
        return _pad_rows(both[:SCONV_K], 8), _pad_rows(both[SCONV_K:], 32)

    eye = jnp.eye(4, dtype=F32)
    wbd_all = (pool_w[:, :, :, None, :] * eye[None, :, None, :, None]).reshape(depth, p, p)

    def row(v):
        return v.reshape(1, -1)

    saved = []
    cur, gam, bet = x[0], jnp.ones((1, d), F32), jnp.zeros((1, d), F32) + started
    for l in range(depth):
        w = gathered(l, 0, started if l == 0 else cur)
        sv = {"w": w}
        sv["xb1"], sv["g1"], sv["u1"], sv["h1"] = ffn_up(cur, gam, bet, w["wg1"], w["wu1"], tm=tm)
        if l == 0:
            w.update(gathered(l, 1, sv["xb1"]))
        sv["xh1"], sv["rs1"] = mm_res_ln(sv["h1"], w["wd1"], cur, gam, bet, alpha=alpha, scale=0.5, tm=tm_down)
        if l == 0:
            w.update(gathered(l, 2, sv["xh1"]))
        g1, b1 = row(ln1_g[l]), row(ln1_b[l])
        sv["proj"], sv["xb2"] = mm_in(sv["xh1"], g1, b1, w["win"], tm=tm_down)
        sv["wbd"] = wbd_all[l].astype(MM)
        sv["sw"], sv["cw"] = conv_filters(w["conv"])
        sv["cat"], sv["conv"] = mixer_fwd(sv["proj"], sv["wbd"], row(pool_scale[l]), sv["sw"], sv["cw"],
                                          row(cconv_b[l]), row(cnorm_g[l]), row(cnorm_b[l]), d=d, tt=tt_fwd)
        sv["xh2"], sv["rs2"] = mm_res_ln(sv["cat"], w["wout"], sv["xh1"], g1, b1, alpha=alpha, scale=1.0, tm=tm_down)
        g2, b2 = row(ln2_g[l]), row(ln2_b[l])
        sv["xb3"], sv["g3"], sv["u3"], sv["h3"] = ffn_up(sv["xh2"], g2, b2, w["wg2"], w["wu2"], tm=tm)
        sv["xh3"], sv["rs3"] = mm_res_ln(sv["h3"], w["wd2"], sv["xh2"], g2, b2, alpha=alpha, scale=0.5, tm=tm_down)
        saved.append(sv)
        cur, gam, bet = sv["xh3"], row(ln3_g[l]), row(ln3_b[l])

    dcur, lsum = loss_head(cur, gam, bet, loss_target[0], tm=tm)
    loss = lax.psum(lsum[0, 0] * (0.5 / d), MESH_AXES)

    exchanges = []
    small = [None] * depth

    def outbox(keys):
        offs, off = {}, 0
        for k in keys:
            off = -(-off // sizes[k]) * sizes[k]
            offs[k] = off
            off += sizes[k]
        if _row_tile(off, 256) < 64:
            off = -(-off // 256) * 256
        return {"keys": keys, "offs": offs, "rows": off, "buf": lax.empty((N_DEV, off, d), MM), "own": {}}

    def grad_into(box, k, lhs, rhs, after=None):
        g, box["buf"] = tn_matmul(lhs, rhs, tm=tm_tn, after=after, into=(box["buf"], box["offs"][k]))
        box["own"][k] = lax.dynamic_slice_in_dim(g, me * sizes[k], sizes[k], axis=0)

    def exchange(l, box):
        mine, at = [], 0
        for k in box["keys"]:
            if box["offs"][k] > at:
                mine.append(jnp.zeros((box["offs"][k] - at, d), F32))
            mine.append(box["own"][k])
            at = box["offs"][k] + sizes[k]
        if box["rows"] > at:
            mine.append(jnp.zeros((box["rows"] - at, d), F32))
        st = copy_start(box["buf"], (N_DEV - 1, box["rows"], d), gather=False,
                        name=f"exchange_start_{l}_{box['keys'][0]}")
        exchanges.append((l, box, st, jnp.concatenate(mine, axis=0)))
        return st[4][0:1, 0:1]

    sent = jnp.zeros((1, 1), F32)
    for l in reversed(range(depth)):
        sv = saved[l]
        w = sv["w"]
        dx, dyb, dg, du, dg3, db3 = ffn_bwd(dcur, sv["xh3"], sv["rs3"], row(ln3_g[l]) + sent, w["wd2"], w["wg2"],
                                             w["wu2"], sv["g3"], sv["u3"], alpha=alpha, tm=tm_bwd)
        box = outbox(["wg2", "wu2", "wd2", "win", "wout"] + ([] if l == 0 else ["wg1", "wu1", "wd1"]))
        grad_into(box, "wg2", dg, sv["xb3"])
        grad_into(box, "wu2", du, sv["xb3"])
        grad_into(box, "wd2", sv["h3"], dyb)
        dz, dzb, dcat, dg2, db2 = lnbwd_mm(dx, sv["xh2"], sv["rs2"], row(ln2_g[l]), w["wout"], tm=tm_down)
        dproj, dwbd, dps, dsw, dcw, dcb, dcg, dcbt = mixer_bwd(
            sv["proj"], dcat, sv["conv"], sv["wbd"], sv["wbd"].T, row(pool_scale[l]), sv["sw"], sv["cw"],
            row(cnorm_g[l]), row(cnorm_b[l]), d=d, tt=tt_bwd)
        grad_into(box, "wout", sv["cat"], dzb)
        grad_into(box, "win", dproj, sv["xb2"])
        if l == 0:
            sent = exchange(l, box)
        dx = mm_add(dproj, w["win"], dz, alpha=alpha, tm=tm_down)
        dx, dyb, dg, du, dg1, db1 = ffn_bwd(dx, sv["xh1"], sv["rs1"], row(ln1_g[l]) + sent, w["wd1"], w["wg1"],
                                             w["wu1"], sv["g1"], sv["u1"], alpha=alpha, tm=tm_bwd)
        dcur = dx
        for k, lhs, rhs in (("wg1", dg, sv["xb1"]), ("wu1", du, sv["xb1"]), ("wd1", sv["h1"], dyb)):
            if l == 0:
                box = outbox([k])
                grad_into(box, k, lhs, rhs, after=sent)
                sent = exchange(l, box)
            else:
                grad_into(box, k, lhs, rhs)
        if l > 0:
            sent = exchange(l, box)
        dpw = jnp.stack([dwbd[g * pg:(g + 1) * pg, g * pg:(g + 1) * pg] for g in range(4)])
        small[l] = [dg1, db1, dg2, db2, dg3, db3, dpw, dps, dsw[:SCONV_K], dcw[:CCONV_K], dcb, dcg, dcbt]
    grad_x = dcur[None]

    by_key = {k: [None] * depth for k in per_layer}

    def finish(ex, after):
        l, box, st, mine = ex
        recv = copy_wait(st, after, gather=False, name=f"exchange_wait_{l}_{box['keys'][0]}")[1]
        gsum = sum_parts(mine, recv)
        for k in box["keys"]:
            by_key[k][l] = gsum[box["offs"][k]:box["offs"][k] + sizes[k]]
        return gsum

    chain = exchanges[-1][3]
    for ex in exchanges[:-1]:
        chain = finish(ex, chain)

    small_shapes = [g.shape for g in small[0]]
    small_flat = _pack_small([g for l in range(depth) for g in small[l]])
    small_sum = sum_gathered(gather_small(small_flat)).reshape(-1)
    small_g = _unpack_small(small_sum, small_shapes * depth)
    n_small = len(small_shapes)

    grads, deltas, new_m, new_v = {}, {}, {}, {}

    def update(name, grad):
        w = a[name]
        c = w.shape[-1]
        dl, mn, vn = adamw(w.reshape(-1, c), grad.reshape(-1, c), a["m_" + name].reshape(-1, c),
                           a["v_" + name].reshape(-1, c))
        grads[name] = grad
        deltas[name], new_m[name], new_v[name] = dl.reshape(w.shape), mn.reshape(w.shape), vn.reshape(w.shape)
        return dl

    small_names = ["ln1_g", "ln1_b", "ln2_g", "ln2_b", "ln3_g", "ln3_b", "pool_w", "pool_scale", "sconv_w",
                   "cconv_w", "cconv_b", "cnorm_g", "cnorm_b"]
    for idx, name in enumerate(small_names):
        full = jnp.stack([small_g[l * n_small + idx] for l in range(depth)])
        if name in ("sconv_w", "cconv_w"):
            full = lax.dynamic_slice_in_dim(full, me * cs, cs, axis=2)
        chain = update(name, full.reshape(a[name].shape))

    big_names = {"wg1": "ffn1_w_gate", "wu1": "ffn1_w_up", "wd1": "ffn1_w_down", "win": "mix_w_in",
                 "wout": "mix_w_out", "wg2": "ffn2_w_gate", "wu2": "ffn2_w_up", "wd2": "ffn2_w_down"}
    sent_transposed = ("wg1", "wu1", "win", "wg2", "wu2")

    def update_big(k):
        return update(big_names[k], jnp.stack([g.T if k in sent_transposed else g for g in by_key[k]]))

    last_keys = exchanges[-1][1]["keys"]
    for k in per_layer:
        if k not in last_keys:
            chain = update_big(k)
    finish(exchanges[-1], chain)
    for k in last_keys:
        update_big(k)

    return (loss, grad_x, *[grads[n] for n in WEIGHTS], *[deltas[n] for n in WEIGHTS],
            *[new_m[n] for n in WEIGHTS], *[new_v[n] for n in WEIGHTS])
```

```python
import functools

import jax
import jax.numpy as jnp
from jax import lax
from jax.experimental import pallas as pl
from jax.experimental.pallas import tpu as pltpu

F32 = jnp.float32
MM = jnp.bfloat16
LN_EPS = 1e-5
N_DEV = 8
MESH_AXES = ("x", "y", "c")
HALO = 32
CCONV_K = 31
SCONV_K = 3
CONV_ROWS = 32
FFN_CHUNK = 256
VMEM_LIMIT = 56 * 1024 * 1024
ADAM_LR, ADAM_B1, ADAM_B2, ADAM_EPS, ADAM_WD, ADAM_STEP = 0.001, 0.9, 0.999, 1e-08, 0.01, 10
MESH = pl.DeviceIdType.MESH


def _call(body, *, name, out_shape, in_specs, out_specs, grid=None, scratch=(), sem=None, aliases=None):
    kw = {}
    if grid is not None:
        kw["grid"] = grid
    if aliases:
        kw["input_output_aliases"] = aliases
    params = dict(vmem_limit_bytes=VMEM_LIMIT)
    if sem is not None:
        params["dimension_semantics"] = sem
    return pl.pallas_call(body, name=name, out_shape=out_shape, in_specs=in_specs, out_specs=out_specs,
                          scratch_shapes=list(scratch), compiler_params=pltpu.CompilerParams(**params), **kw)


def _rows(tm, n):
    return pl.BlockSpec((tm, n), lambda i: (i, 0))


def _const(shape):
    nd = len(shape)
    return pl.BlockSpec(shape, lambda *_: (0,) * nd, pipeline_mode=pl.Buffered(1))


def _acc(shape):
    nd = len(shape)
    return pl.BlockSpec(shape, lambda *_: (0,) * nd)


def _row_tile(rows, cap):
    best = None
    for t in range(8, min(rows, cap) + 1, 8):
        if rows % t == 0:
            best = t
    return best if best is not None else rows


def _sig(x):
    return 1.0 / (1.0 + jnp.exp(-x))


def _ln_stats(z):
    mu = jnp.mean(z, axis=-1, keepdims=True)
    zc = z - mu
    var = jnp.mean(zc * zc, axis=-1, keepdims=True)
    rstd = lax.rsqrt(var + LN_EPS)
    return zc * rstd, rstd


def _ln_bwd(dout, xhat, rstd, gamma):
    dxh = dout * gamma
    m1 = jnp.mean(dxh, axis=-1, keepdims=True)
    m2 = jnp.mean(dxh * xhat, axis=-1, keepdims=True)
    return rstd * (dxh - m1 - xhat * m2)


def _colsum(v):
    return jnp.sum(v, axis=0, keepdims=True)


def _f_chunks(f, width=None):
    if width is not None and f % width == 0:
        return f // width, width
    n = 2 if f >= 2048 and f % 256 == 0 else 1
    return n, f // n


def _dot_nt(a, b):
    return lax.dot_general(a, b, (((1,), (1,)), ((), ())), preferred_element_type=F32)


def ffn_up(xin, gam, bet, wgt, wut, *, tm):
    t, d = xin.shape
    f = wgt.shape[0]
    nch, fc = _f_chunks(f, FFN_CHUNK)

    def body(x_ref, g_ref, b_ref, wg_ref, wu_ref, xb_ref, go_ref, uo_ref, h_ref):
        xb = (x_ref[...] * g_ref[...] + b_ref[...]).astype(MM)
        xb_ref[...] = xb
        for c in range(nch):
            sl = slice(c * fc, (c + 1) * fc)
            g = _dot_nt(xb, wg_ref[sl, :])
            u = _dot_nt(xb, wu_ref[sl, :])
            go_ref[:, sl] = g.astype(MM)
            uo_ref[:, sl] = u.astype(MM)
            h_ref[:, sl] = (g * _sig(g) * u).astype(MM)

    return _call(
        body, name="ffn_up", grid=(t // tm,), sem=("parallel",),
        in_specs=[_rows(tm, d), _const((1, d)), _const((1, d)), _const((f, d)), _const((f, d))],
        out_specs=[_rows(tm, d), _rows(tm, f), _rows(tm, f), _rows(tm, f)],
        out_shape=[jax.ShapeDtypeStruct((t, d), MM)] + [jax.ShapeDtypeStruct((t, f), MM)] * 3,
    )(xin, gam, bet, wgt, wut)


def mm_res_ln(a, w, xin, gam, bet, *, alpha, scale, tm):
    t, k = a.shape
    d = w.shape[1]

    def body(a_ref, w_ref, x_ref, g_ref, b_ref, xh_ref, rs_ref):
        y = jnp.dot(a_ref[...], w_ref[...], preferred_element_type=F32)
        x = x_ref[...] * g_ref[...] + b_ref[...]
        xh, rstd = _ln_stats(alpha * x + scale * y)
        xh_ref[...] = xh
        rs_ref[...] = rstd

    return _call(
        body, name="mm_res_ln", grid=(t // tm,), sem=("parallel",),
        in_specs=[_rows(tm, k), _const((k, d)), _rows(tm, d), _const((1, d)), _const((1, d))],
        out_specs=[_rows(tm, d), _rows(tm, 1)],
        out_shape=[jax.ShapeDtypeStruct((t, d), F32), jax.ShapeDtypeStruct((t, 1), F32)],
    )(a, w, xin, gam, bet)


def mm_in(xin, gam, bet, wt, *, tm):
    t, d = xin.shape
    n = wt.shape[0]

    def body(x_ref, g_ref, b_ref, w_ref, o_ref, xb_ref):
        xb = (x_ref[...] * g_ref[...] + b_ref[...]).astype(MM)
        xb_ref[...] = xb
        o_ref[...] = _dot_nt(xb, w_ref[...]).astype(MM)

    return _call(
        body, name="mm_in", grid=(t // tm,), sem=("parallel",),
        in_specs=[_rows(tm, d), _const((1, d)), _const((1, d)), _const((n, d))],
        out_specs=[_rows(tm, n), _rows(tm, d)],
        out_shape=[jax.ShapeDtypeStruct((t, n), MM), jax.ShapeDtypeStruct((t, d), MM)],
    )(xin, gam, bet, wt)


def loss_head(xh, gam, bet, target, *, tm):
    t, d = xh.shape

    def body(x_ref, g_ref, b_ref, t_ref, dy_ref, l_ref):
        @pl.when(pl.program_id(0) == 0)
        def _():
            l_ref[...] = jnp.zeros_like(l_ref)

        e = x_ref[...] * g_ref[...] + b_ref[...] - t_ref[...]
        dy_ref[...] = e * (1.0 / d)
        l_ref[...] += jnp.sum(_colsum(e * e), axis=1, keepdims=True)

    return _call(
        body, name="loss_head", grid=(t // tm,), sem=("arbitrary",),
        in_specs=[_rows(tm, d), _const((1, d)), _const((1, d)), _rows(tm, d)],
        out_specs=[_rows(tm, d), _acc((1, 1))],
        out_shape=[jax.ShapeDtypeStruct((t, d), F32), jax.ShapeDtypeStruct((1, 1), F32)],
    )(xh, gam, bet, target)


def _halo_specs(tt, ncols, t):
    per, last = tt // HALO, t // HALO - 1
    return [pl.BlockSpec((HALO, ncols), lambda i: (jnp.maximum(i * per - 1, 0), 0)),
            pl.BlockSpec((tt, ncols), lambda i: (i, 0)),
            pl.BlockSpec((HALO, ncols), lambda i: (jnp.minimum((i + 1) * per, last), 0))]


def _fill_ext(ext, prev_ref, cur_ref, next_ref, i, nt, tt):
    ext[0:HALO, :] = jnp.where(i > 0, prev_ref[...].astype(F32), 0.0)
    ext[HALO:HALO + tt, :] = cur_ref[...].astype(F32)
    ext[HALO + tt:HALO + tt + HALO, :] = jnp.where(i < nt - 1, next_ref[...].astype(F32), 0.0)


def _make_shifts(sh, n):
    for r in range(1, 8):
        sh[r, 0:n, :] = sh[0, r:r + n, :]


def _shift_reader(sh):
    def read(o, rows):
        r = o % 8
        return sh[r, o - r:o - r + rows, :]
    return read


def _conv_chunks_shifted(sh, w_ref, ktaps, src0, nrows, flip=False):
    for r0 in range(0, nrows, CONV_ROWS):
        acc = None
        base = src0 + r0
        for r in range(8):
            ks = [k for k in range(ktaps) if (base + k) % 8 == r]
            if not ks:
                continue
            lo = base + ks[0] - r
            slab = sh[r, lo:lo + CONV_ROWS + 8 * (len(ks) - 1), :]
            for q, k in enumerate(ks):
                kk = ktaps - 1 - k if flip else k
                term = w_ref[kk:kk + 1, :] * slab[8 * q:8 * q + CONV_ROWS, :]
                acc = term if acc is None else acc + term
        yield r0, acc


def _ref_reader(ref):
    def read(o, rows):
        return ref[o:o + rows, :]
    return read


def _conv_chunks(read, w_ref, ktaps, src0, nrows, flip=False):
    for r0 in range(0, nrows, CONV_ROWS):
        acc = None
        for k in range(ktaps):
            kk = ktaps - 1 - k if flip else k
            term = w_ref[kk:kk + 1, :] * read(src0 + r0 + k, CONV_ROWS)
            acc = term if acc is None else acc + term
        yield r0, acc


def _pool_groups(nrows, p):
    lane = lax.broadcasted_iota(jnp.int32, (nrows, p), 1)
    g = p // 4
    return lane < g, lane < 2 * g, lane < 3 * g


def _pool_select(groups, v2, v4, v8, v16):
    g0, g1, g2 = groups
    return jnp.where(g0, v2, jnp.where(g1, v4, jnp.where(g2, v8, v16)))


def _pool_count(groups, i, tt, row0, nrows, p, t):
    pos = i * tt + (row0 - HALO) + lax.broadcasted_iota(jnp.int32, (nrows, p), 0)
    half = _pool_select(groups, 1, 2, 4, 8)
    lo = jnp.clip(pos - half, 0, t)
    hi = jnp.clip(pos + half, 0, t)
    return jnp.maximum(hi - lo, 1).astype(F32)


def _pool_forward(ext, s2, s4, s8, groups, cnt, tt, p):
    e = tt + 2 * HALO
    s2[8:e - 8, :] = ext[7:e - 9, 0:p] + ext[8:e - 8, 0:p]
    s4[16:e - 16, :] = s2[15:e - 17, :] + s2[17:e - 15, :]
    s8[24:e - 24, :] = s4[22:e - 26, :] + s4[26:e - 22, :]
    s16 = s8[28:e - 36, :] + s8[36:e - 28, :]
    c = slice(HALO, HALO + tt)
    tot = _pool_select(groups, s2[c, :], s4[c, :], s8[c, :], s16)
    return tot / cnt - ext[c, 0:p]


def mixer_fwd(proj, wbd, pscale, sw, cw, cb, cg, cbt, *, d, tt):
    t, ncols = proj.shape
    p, s = d // 4, 3 * d // 8
    o_gb, o_gc, o_v, o_cv, o_cg = p, p + s, p + 2 * s, p + 3 * s, p + 4 * s
    nt, e = t // tt, tt + 2 * HALO

    def body(prev_ref, cur_ref, next_ref, wbd_ref, ps_ref, sw_ref, cw_ref, cb_ref, cg_ref, cbt_ref,
             cat_ref, conv_ref, ext, a_sh, cv_s, s2, s4, s8):
        i = pl.program_id(0)
        _fill_ext(ext, prev_ref, cur_ref, next_ref, i, nt, tt)
        a_sh[0, 0:e, :] = ext[:, o_cv:o_cv + s] * _sig(ext[:, o_cg:o_cg + s])
        a_sh[0, e:e + 8, :] = jnp.zeros((8, s), F32)
        _make_shifts(a_sh, e)
        for r0, acc in _conv_chunks_shifted(a_sh, cw_ref, CCONV_K, HALO - CCONV_K // 2, tt):
            b = acc + cb_ref[...]
            conv_ref[r0:r0 + CONV_ROWS, :] = b
            n, _ = _ln_stats(b)
            yn = n * cg_ref[...] + cbt_ref[...]
            cat_ref[r0:r0 + CONV_ROWS, p + s:d] = (yn * _sig(yn)).astype(MM)
        cv_s[...] = ext[:, o_gc:o_gc + s] * ext[:, o_v:o_v + s]
        for r0, acc in _conv_chunks(_ref_reader(cv_s), sw_ref, SCONV_K, HALO - SCONV_K // 2, tt):
            gb = ext[HALO + r0:HALO + r0 + CONV_ROWS, o_gb:o_gb + s]
            cat_ref[r0:r0 + CONV_ROWS, p:p + s] = (gb * acc).astype(MM)
        groups = _pool_groups(tt, p)
        cnt = _pool_count(groups, i, tt, HALO, tt, p, t)
        pooled = _pool_forward(ext, s2, s4, s8, groups, cnt, tt, p)
        ya = jnp.dot(pooled.astype(MM), wbd_ref[...], preferred_element_type=F32) * ps_ref[...]
        cat_ref[:, 0:p] = ya.astype(MM)

    return _call(
        body, name="mixer_fwd", grid=(nt,), sem=("parallel",),
        in_specs=_halo_specs(tt, ncols, t) + [_const((p, p)), _const((1, p)), _const((8, s)), _const((32, s)),
                                               _const((1, s)), _const((1, s)), _const((1, s))],
        out_specs=[_rows(tt, d), _rows(tt, s)],
        out_shape=[jax.ShapeDtypeStruct((t, d), MM), jax.ShapeDtypeStruct((t, s), F32)],
        scratch=[pltpu.VMEM((e, ncols), F32), pltpu.VMEM((8, e + 8, s), F32), pltpu.VMEM((e, s), F32),
                 pltpu.VMEM((e, p), F32), pltpu.VMEM((e, p), F32), pltpu.VMEM((e, p), F32)],
    )(proj, proj, proj, wbd, pscale, sw, cw, cb, cg, cbt)


def mixer_bwd(proj, dcat, conv, wbd, wbdt, pscale, sw, cw, cg, cbt, *, d, tt):
    t, ncols = proj.shape
    p, s = d // 4, 3 * d // 8
    o_gb, o_gc, o_v, o_cv, o_cg = p, p + s, p + 2 * s, p + 3 * s, p + 4 * s
    nt, e = t // tt, tt + 2 * HALO

    def body(pp_ref, pc_ref, pn_ref, dp_ref, dc_ref, dn_ref, bp_ref, bc_ref, bn_ref, wbd_ref, wbdt_ref, ps_ref,
             sw_ref, cw_ref, cg_ref, cbt_ref,
             dproj_ref, dwbd_ref, dps_ref, dsw_ref, dcw_ref, dcb_ref, dcg_ref, dcbt_ref,
             ext, dext, bext, a_sh, b_sh, sg_s, cv_s, ds_s, q_s, r2, r4, r8):
        i = pl.program_id(0)

        @pl.when(i == 0)
        def _():
            for ref in (dwbd_ref, dps_ref, dsw_ref, dcw_ref, dcb_ref, dcg_ref, dcbt_ref):
                ref[...] = jnp.zeros_like(ref)

        _fill_ext(ext, pp_ref, pc_ref, pn_ref, i, nt, tt)
        _fill_ext(dext, dp_ref, dc_ref, dn_ref, i, nt, tt)
        _fill_ext(bext, bp_ref, bc_ref, bn_ref, i, nt, tt)
        c = slice(HALO, HALO + tt)

        sg_s[...] = _sig(ext[:, o_cg:o_cg + s])
        a_sh[0, 0:e, :] = ext[:, o_cv:o_cv + s] * sg_s[...]
        a_sh[0, e:e + 8, :] = jnp.zeros((8, s), F32)
        _make_shifts(a_sh, e)
        read_a = _shift_reader(a_sh)
        for r0 in range(0, e, CONV_ROWS):
            b = bext[r0:r0 + CONV_ROWS, :]
            n, rstd = _ln_stats(b)
            yn = n * cg_ref[...] + cbt_ref[...]
            sy = _sig(yn)
            rows = slice(r0, r0 + CONV_ROWS)
            dyn = dext[rows, p + s:d] * (sy * (1.0 + yn * (1.0 - sy)))
            db = _ln_bwd(dyn, n, rstd, cg_ref[...])
            b_sh[0, rows, :] = db
            if HALO <= r0 < HALO + tt:
                dcg_ref[...] += _colsum(dyn * n)
                dcbt_ref[...] += _colsum(dyn)
                dcb_ref[...] += _colsum(db)
        b_sh[0, e:e + 8, :] = jnp.zeros((8, s), F32)
        _make_shifts(b_sh, e)
        for r0, da in _conv_chunks_shifted(b_sh, cw_ref, CCONV_K, HALO - CCONV_K // 2, tt, flip=True):
            rows = slice(HALO + r0, HALO + r0 + CONV_ROWS)
            sg = sg_s[rows, :]
            dproj_ref[r0:r0 + CONV_ROWS, o_cv:o_cv + s] = (da * sg).astype(MM)
            dproj_ref[r0:r0 + CONV_ROWS, o_cg:o_cg + s] = (
                da * ext[rows, o_cv:o_cv + s] * sg * (1.0 - sg)).astype(MM)
        for k in range(CCONV_K):
            lo = HALO + k - CCONV_K // 2
            dcw_ref[k:k + 1, :] += _colsum(b_sh[0, c, :] * read_a(lo, tt))

        cv_s[...] = ext[:, o_gc:o_gc + s] * ext[:, o_v:o_v + s]
        ds_s[...] = dext[:, p:p + s] * ext[:, o_gb:o_gb + s]
        for r0, acc in _conv_chunks(_ref_reader(cv_s), sw_ref, SCONV_K, HALO - SCONV_K // 2, tt):
            rows = slice(HALO + r0, HALO + r0 + CONV_ROWS)
            dproj_ref[r0:r0 + CONV_ROWS, o_gb:o_gb + s] = (dext[rows, p:p + s] * acc).astype(MM)
        for r0, dcv in _conv_chunks(_ref_reader(ds_s), sw_ref, SCONV_K, HALO - SCONV_K // 2, tt, flip=True):
            rows = slice(HALO + r0, HALO + r0 + CONV_ROWS)
            dproj_ref[r0:r0 + CONV_ROWS, o_gc:o_gc + s] = (dcv * ext[rows, o_v:o_v + s]).astype(MM)
            dproj_ref[r0:r0 + CONV_ROWS, o_v:o_v + s] = (dcv * ext[rows, o_gc:o_gc + s]).astype(MM)
        for k in range(SCONV_K):
            lo = HALO + k - SCONV_K // 2
            dsw_ref[k:k + 1, :] += _colsum(ds_s[c, :] * cv_s[lo:lo + tt, :])

        groups = _pool_groups(tt, p)
        cnt = _pool_count(groups, i, tt, HALO, tt, p, t)
        pooled = _pool_forward(ext, r2, r4, r8, groups, cnt, tt, p).astype(MM)
        ta = jnp.dot(pooled, wbd_ref[...], preferred_element_type=F32)
        dps_ref[...] += _colsum(dext[c, 0:p] * ta)
        dta = (dext[:, 0:p] * ps_ref[...]).astype(MM)
        dwbd_ref[...] += lax.dot_general(pooled, dta[HALO:HALO + tt, :], (((0,), (0,)), ((), ())),
                                         preferred_element_type=F32)
        dpool = jnp.dot(dta, wbdt_ref[...], preferred_element_type=F32)
        groups_e = _pool_groups(e, p)
        q_s[...] = dpool / _pool_count(groups_e, i, tt, 0, e, p, t)
        r2[8:e - 8, :] = q_s[8:e - 8, :] + q_s[9:e - 7, :]
        r4[16:e - 16, :] = r2[15:e - 17, :] + r2[17:e - 15, :]
        r8[24:e - 24, :] = r4[22:e - 26, :] + r4[26:e - 22, :]
        r16 = r8[28:e - 36, :] + r8[36:e - 28, :]
        du = _pool_select(groups, r2[c, :], r4[c, :], r8[c, :], r16) - dpool[HALO:HALO + tt, :]
        dproj_ref[:, 0:p] = du.astype(MM)

    small = [(p, p), (1, p), (8, s), (32, s), (1, s), (1, s), (1, s)]
    return _call(
        body, name="mixer_bwd", grid=(nt,), sem=("arbitrary",),
        in_specs=_halo_specs(tt, ncols, t) + _halo_specs(tt, d, t) + _halo_specs(tt, s, t) + [
            _const((p, p)), _const((p, p)), _const((1, p)), _const((8, s)), _const((32, s)),
            _const((1, s)), _const((1, s))],
        out_specs=[_rows(tt, ncols)] + [_acc(sh) for sh in small],
        out_shape=[jax.ShapeDtypeStruct((t, ncols), MM)] + [jax.ShapeDtypeStruct(sh, F32) for sh in small],
        scratch=[pltpu.VMEM((e, ncols), F32), pltpu.VMEM((e, d), F32), pltpu.VMEM((e, s), F32),
                 pltpu.VMEM((8, e + 8, s), F32), pltpu.VMEM((8, e + 8, s), F32)]
        + [pltpu.VMEM((e, s), F32)] * 3 + [pltpu.VMEM((e, p), F32)] * 4,
    )(proj, proj, proj, dcat, dcat, dcat, conv, conv, conv, wbd, wbdt, pscale, sw, cw, cg, cbt)


def ffn_bwd(dout, xh, rs, gam, wd, wgt, wut, g, u, *, alpha, tm):
    t, d = dout.shape
    f = g.shape[1]
    nch, fc = _f_chunks(f, FFN_CHUNK)

    def body(do_ref, xh_ref, rs_ref, gm_ref, wd_ref, wgt_ref, wut_ref, g_ref, u_ref,
             dx_ref, dyb_ref, dg_ref, du_ref, dgam_ref, dbet_ref):
        @pl.when(pl.program_id(0) == 0)
        def _():
            dgam_ref[...] = jnp.zeros_like(dgam_ref)
            dbet_ref[...] = jnp.zeros_like(dbet_ref)

        dout_v, xhat = do_ref[...], xh_ref[...]
        dgam_ref[...] += _colsum(dout_v * xhat)
        dbet_ref[...] += _colsum(dout_v)
        dz = _ln_bwd(dout_v, xhat, rs_ref[...], gm_ref[...])
        dyb = (0.5 * dz).astype(MM)
        dyb_ref[...] = dyb
        for c in range(nch):
            sl = slice(c * fc, (c + 1) * fc)
            dh = _dot_nt(dyb, wd_ref[sl, :])
            gv, uv = g_ref[:, sl].astype(F32), u_ref[:, sl].astype(F32)
            sg = _sig(gv)
            dg_ref[:, sl] = (dh * uv * (sg * (1.0 + gv * (1.0 - sg)))).astype(MM)
            du_ref[:, sl] = (dh * (gv * sg)).astype(MM)
        dx_ref[...] = (alpha * dz + jnp.dot(dg_ref[...], wgt_ref[...], preferred_element_type=F32)
                       + jnp.dot(du_ref[...], wut_ref[...], preferred_element_type=F32))

    return _call(
        body, name="ffn_bwd", grid=(t // tm,), sem=("arbitrary",),
        in_specs=[_rows(tm, d), _rows(tm, d), _rows(tm, 1), _const((1, d)), _const((f, d)), _const((f, d)),
                  _const((f, d)), _rows(tm, f), _rows(tm, f)],
        out_specs=[_rows(tm, d), _rows(tm, d), _rows(tm, f), _rows(tm, f), _acc((1, d)), _acc((1, d))],
        out_shape=[jax.ShapeDtypeStruct((t, d), F32), jax.ShapeDtypeStruct((t, d), MM),
                   jax.ShapeDtypeStruct((t, f), MM), jax.ShapeDtypeStruct((t, f), MM),
                   jax.ShapeDtypeStruct((1, d), F32), jax.ShapeDtypeStruct((1, d), F32)],
    )(dout, xh, rs, gam, wd, wgt, wut, g, u)


def lnbwd_mm(dout, xh, rs, gam, w, *, tm):
    t, d = dout.shape
    n = w.shape[0]

    def body(do_ref, xh_ref, rs_ref, gm_ref, wt_ref, dz_ref, dzb_ref, da_ref, dgam_ref, dbet_ref):
        @pl.when(pl.program_id(0) == 0)
        def _():
            dgam_ref[...] = jnp.zeros_like(dgam_ref)
            dbet_ref[...] = jnp.zeros_like(dbet_ref)

        dout_v, xhat = do_ref[...], xh_ref[...]
        dgam_ref[...] += _colsum(dout_v * xhat)
        dbet_ref[...] += _colsum(dout_v)
        dz = _ln_bwd(dout_v, xhat, rs_ref[...], gm_ref[...])
        dz_ref[...] = dz
        dzb = dz.astype(MM)
        dzb_ref[...] = dzb
        da_ref[...] = _dot_nt(dzb, wt_ref[...]).astype(MM)

    return _call(
        body, name="lnbwd_mm", grid=(t // tm,), sem=("arbitrary",),
        in_specs=[_rows(tm, d), _rows(tm, d), _rows(tm, 1), _const((1, d)), _const((n, d))],
        out_specs=[_rows(tm, d), _rows(tm, d), _rows(tm, n), _acc((1, d)), _acc((1, d))],
        out_shape=[jax.ShapeDtypeStruct((t, d), F32), jax.ShapeDtypeStruct((t, d), MM),
                   jax.ShapeDtypeStruct((t, n), MM), jax.ShapeDtypeStruct((1, d), F32),
                   jax.ShapeDtypeStruct((1, d), F32)],
    )(dout, xh, rs, gam, w)


def mm_add(a, w, r, *, alpha, tm):
    t, k = a.shape
    d = w.shape[1]

    def body(a_ref, w_ref, r_ref, o_ref):
        o_ref[...] = jnp.dot(a_ref[...], w_ref[...], preferred_element_type=F32) + alpha * r_ref[...]

    return _call(
        body, name="mm_add", grid=(t // tm,), sem=("parallel",),
        in_specs=[_rows(tm, k), _const((k, d)), _rows(tm, d)],
        out_specs=_rows(tm, d),
        out_shape=jax.ShapeDtypeStruct((t, d), F32),
    )(a, w, r)


def tn_matmul(a, b, *, tm, after=None, into=None):
    t, n = a.shape
    d = b.shape[1]
    nch, nc = _f_chunks(n)
    steps = t // tm
    per = n // N_DEV
    slots = N_DEV // nch

    def body(a_ref, b_ref, *rest):
        o_ref, ob_ref = rest[-2:]

        @pl.when(pl.program_id(1) == 0)
        def _():
            o_ref[...] = jnp.zeros_like(o_ref)

        o_ref[...] += lax.dot_general(a_ref[...], b_ref[...], (((0,), (0,)), ((), ())),
                                      preferred_element_type=F32)

        @pl.when(pl.program_id(1) == steps - 1)
        def _():
            if into is None:
                ob_ref[...] = o_ref[...].astype(MM)
            else:
                for s in range(slots):
                    ob_ref[s] = o_ref[s * per:(s + 1) * per, :].astype(MM)

    extra = [] if after is None else [after]
    if into is None:
        ob_spec, ob_shape, alias = pl.BlockSpec((nc, d), lambda j, i: (j, 0)), jax.ShapeDtypeStruct((n, d), MM), {}
    else:
        buf, off = into
        assert off % per == 0 and buf.shape[0] == N_DEV and buf.dtype == MM
        extra.append(buf)
        ob_spec = pl.BlockSpec((slots, per, d), lambda j, i: (j, off // per, 0))
        ob_shape, alias = jax.ShapeDtypeStruct(buf.shape, MM), {1 + len(extra): 1}
    return _call(
        body, name="tn_matmul", grid=(nch, steps), sem=("parallel", "arbitrary"), aliases=alias,
        in_specs=[pl.BlockSpec((tm, nc), lambda j, i: (i, j)), pl.BlockSpec((tm, d), lambda j, i: (i, 0))]
        + [pl.BlockSpec(memory_space=pl.ANY)] * len(extra),
        out_specs=[pl.BlockSpec((nc, d), lambda j, i: (j, 0)), ob_spec],
        out_shape=[jax.ShapeDtypeStruct((n, d), F32), ob_shape],
    )(a, b, *extra)


def sum_parts(own, recv):
    r, d = own.shape
    n = recv.shape[0]
    tr = _row_tile(r, 256)

    def body(own_ref, recv_ref, o_ref):
        acc = own_ref[...]
        for k in range(n):
            acc = acc + recv_ref[k].astype(F32)
        o_ref[...] = acc

    return _call(
        body, name="sum_parts", grid=(r // tr,), sem=("parallel",),
        in_specs=[_rows(tr, d), pl.BlockSpec((n, tr, d), lambda i: (0, i, 0))],
        out_specs=_rows(tr, d),
        out_shape=jax.ShapeDtypeStruct((r, d), F32),
    )(own, recv)


def sum_gathered(parts):
    _, r, n = parts.shape

    def body(p_ref, o_ref):
        acc = p_ref[0]
        for j in range(1, N_DEV):
            acc = acc + p_ref[j]
        o_ref[...] = acc

    return _call(
        body, name="sum_gathered",
        in_specs=[pl.BlockSpec(memory_space=pltpu.VMEM)], out_specs=pl.BlockSpec(memory_space=pltpu.VMEM),
        out_shape=jax.ShapeDtypeStruct((r, n), F32),
    )(parts)


def adamw(w, g, m, v):
    r, c = w.shape
    tr = _row_tile(r, 512)

    def body(w_ref, g_ref, m_ref, v_ref, d_ref, mo_ref, vo_ref):
        gv = g_ref[...]
        mn = ADAM_B1 * m_ref[...] + (1.0 - ADAM_B1) * gv
        vn = ADAM_B2 * v_ref[...] + (1.0 - ADAM_B2) * (gv * gv)
        m_hat = mn / (1.0 - ADAM_B1 ** ADAM_STEP)
        v_hat = vn / (1.0 - ADAM_B2 ** ADAM_STEP)
        d_ref[...] = -ADAM_LR * (m_hat / (jnp.sqrt(v_hat) + ADAM_EPS) + ADAM_WD * w_ref[...])
        mo_ref[...] = mn
        vo_ref[...] = vn

    return _call(
        body, name="adamw", grid=(r // tr,), sem=("parallel",),
        in_specs=[_rows(tr, c)] * 4, out_specs=[_rows(tr, c)] * 3,
        out_shape=[jax.ShapeDtypeStruct((r, c), F32)] * 3,
    )(w, g, m, v)


def _mesh_pos():
    return lax.axis_index("x"), lax.axis_index("y"), lax.axis_index("c")


def _two_level_gather(x_ref, out_ref, send_sems, recv_sems, local_sem):
    x, y, c = _mesh_pos()
    me, sibling = (x, y, c), (x, y, 1 - c)
    chips = [(1 - x, y), (x, 1 - y), (1 - x, 1 - y)]

    def slot(px, py, pc):
        return out_ref.at[4 * px + 2 * py + pc]

    def copy(k, block, to, src=None):
        return pltpu.make_async_remote_copy(
            src_ref=slot(*block) if src is None else src, dst_ref=slot(*block),
            send_sem=send_sems.at[k], recv_sem=recv_sems.at[k], device_id=to, device_id_type=MESH)

    mine = pltpu.make_async_copy(x_ref, slot(*me), local_sem)
    mine.start()
    first = [copy(1 + j, me, (*chip, c), src=x_ref) for j, chip in enumerate(chips)]
    first.append(copy(0, me, sibling, src=x_ref))
    for cp in first:
        cp.start()
    passed = [copy(4 + j, (*chip, c), sibling) for j, chip in enumerate(chips)]
    for j, chip in enumerate(chips):
        copy(1 + j, (*chip, c), me).wait_recv()
        passed[j].start()
    copy(0, sibling, me).wait_recv()
    for j, chip in enumerate(chips):
        copy(4 + j, (*chip, 1 - c), me).wait_recv()
    for cp in first + passed:
        cp.wait_send()
    mine.wait()


def gather_small(x):
    return _call(
        _two_level_gather_body(), name="gather_small",
        in_specs=[pl.BlockSpec(memory_space=pltpu.VMEM)], out_specs=pl.BlockSpec(memory_space=pltpu.VMEM),
        out_shape=jax.ShapeDtypeStruct((N_DEV,) + x.shape, x.dtype),
        scratch=[pltpu.SemaphoreType.DMA((7,)), pltpu.SemaphoreType.DMA((7,)), pltpu.SemaphoreType.DMA(())],
    )(x)


def _two_level_gather_body():
    def body(x_ref, out_ref, send_sems, recv_sems, local_sem):
        _two_level_gather(x_ref, out_ref, send_sems, recv_sems, local_sem)
    return body


_HBM = pl.BlockSpec(memory_space=pltpu.HBM)
_SEM = pl.BlockSpec(memory_space=pltpu.SEMAPHORE)
_EFFECT = pltpu.SideEffectType.DATAFLOW_SIDE_EFFECTING
_RELATIONS = (4, 2, 6, 5, 3, 7, 1)


def _split_copy(src_ref, land_ref, send_sems, recv_sems, k, gather, a):
    x, y, c = _mesh_pos()
    px, py, pc = (1 - x if k & 4 else x), (1 - y if k & 2 else y), (1 - c if k & 1 else c)
    if gather:
        src, dst = src_ref, land_ref.at[4 * x + 2 * y + c]
    else:
        src, dst = src_ref.at[4 * px + 2 * py + pc], land_ref.at[k - 1]
    return pltpu.make_async_remote_copy(src_ref=src, dst_ref=dst, send_sem=send_sems.at[7 * a + k - 1],
                                        recv_sem=recv_sems.at[7 * a + k - 1], device_id=(px, py, pc),
                                        device_id_type=MESH)


def copy_start(srcs, land_shapes, *, gather, name):
    m = len(srcs)

    def body(*refs):
        send_sems, recv_sems = refs[2 * m], refs[2 * m + 1]
        for a in range(m):
            for k in _RELATIONS:
                _split_copy(refs[a], refs[m + a], send_sems, recv_sems, k, gather, a).start()
        refs[-1][...] = jnp.zeros_like(refs[-1])

    res = pl.pallas_call(
        body, name=name,
        out_shape=(pltpu.SemaphoreType.DMA((7 * m,)), pltpu.SemaphoreType.DMA((7 * m,)),
                   *[pltpu.HBM(s.shape, s.dtype) for s in srcs],
                   *[pltpu.HBM(ls, s.dtype) for ls, s in zip(land_shapes, srcs)], jax.ShapeDtypeStruct((8, 128), F32)),
        in_specs=(_HBM,) * (2 * m),
        out_specs=(_SEM, _SEM) + (_HBM,) * (2 * m) + (pl.BlockSpec(memory_space=pltpu.VMEM),),
        input_output_aliases={i: 2 + i for i in range(2 * m)},
        compiler_params=pltpu.CompilerParams(has_side_effects=_EFFECT),
    )(*[pltpu.with_memory_space_constraint(s, pltpu.HBM) for s in srcs],
      *[pltpu.with_memory_space_constraint(lax.empty(ls, s.dtype), pltpu.HBM) for ls, s in zip(land_shapes, srcs)])
    return res[0], res[1], list(res[2:2 + m]), list(res[2 + m:2 + 2 * m]), res[-1]


def copy_wait(started, after, *, gather, name):
    send_sems, recv_sems, srcs, lands, _ = started
    m = len(srcs)

    def body(*refs):
        send_sems, recv_sems = refs[2 * m], refs[2 * m + 1]
        for a in range(m):
            for k in _RELATIONS:
                cp = _split_copy(refs[a], refs[m + a], send_sems, recv_sems, k, gather, a)
                cp.wait_send()
                cp.wait_recv()

    res = pl.pallas_call(
        body, name=name,
        out_shape=tuple(pltpu.HBM(v.shape, v.dtype) for v in srcs + lands),
        in_specs=(_HBM,) * (2 * m) + (_SEM, _SEM, pl.BlockSpec(memory_space=pl.ANY)), out_specs=(_HBM,) * (2 * m),
        input_output_aliases={i: i for i in range(2 * m)},
        compiler_params=pltpu.CompilerParams(has_side_effects=_EFFECT),
    )(*srcs, *lands, send_sems, recv_sems, after)
    return list(res[:m]), list(res[m:])


WEIGHTS = ['ln1_g', 'ln1_b', 'ffn1_w_gate', 'ffn1_w_up', 'ffn1_w_down', 'mix_w_in', 'pool_w', 'pool_scale',
           'sconv_w', 'cconv_w', 'cconv_b', 'cnorm_g', 'cnorm_b', 'mix_w_out', 'ln2_g', 'ln2_b',
           'ffn2_w_gate', 'ffn2_w_up', 'ffn2_w_down', 'ln3_g', 'ln3_b']


def _pack_small(pieces):
    flat = jnp.concatenate([p.reshape(-1) for p in pieces])
    pad = -flat.shape[0] % 1024
    return jnp.pad(flat, (0, pad)).reshape(-1, 128)


def _unpack_small(flat, shapes):
    out, off = [], 0
    for sh in shapes:
        n = 1
        for s in sh:
            n *= s
        out.append(flat[off:off + n].reshape(sh))
        off += n
    return out


def _pad_rows(a, rows):
    return jnp.pad(a, ((0, rows - a.shape[0]), (0, 0)))


def kernel(x, ln1_g, ln1_b, ffn1_w_gate, ffn1_w_up, ffn1_w_down, mix_w_in, pool_w, pool_scale, sconv_w, cconv_w, cconv_b, cnorm_g, cnorm_b, mix_w_out, ln2_g, ln2_b, ffn2_w_gate, ffn2_w_up, ffn2_w_down, ln3_g, ln3_b, loss_target, m_ln1_g, m_ln1_b, m_ffn1_w_gate, m_ffn1_w_up, m_ffn1_w_down, m_mix_w_in, m_pool_w, m_pool_scale, m_sconv_w, m_cconv_w, m_cconv_b, m_cnorm_g, m_cnorm_b, m_mix_w_out, m_ln2_g, m_ln2_b, m_ffn2_w_gate, m_ffn2_w_up, m_ffn2_w_down, m_ln3_g, m_ln3_b, v_ln1_g, v_ln1_b, v_ffn1_w_gate, v_ffn1_w_up, v_ffn1_w_down, v_mix_w_in, v_pool_w, v_pool_scale, v_sconv_w, v_cconv_w, v_cconv_b, v_cnorm_g, v_cnorm_b, v_mix_w_out, v_ln2_g, v_ln2_b, v_ffn2_w_gate, v_ffn2_w_up, v_ffn2_w_down, v_ln3_g, v_ln3_b):
    a = dict(locals())
    depth, d = ln1_g.shape
    t = x.shape[1]
    fs = ffn1_w_gate.shape[2]
    f = fs * N_DEV
    ins = mix_w_in.shape[2]
    ncols = ins * N_DEV
    outs = mix_w_out.shape[1]
    p, s = d // 4, 3 * d // 8
    pg = p // 4
    cs = sconv_w.shape[2]
    alpha = (2.0 * depth) ** 0.25
    me = 4 * lax.axis_index("x") + 2 * lax.axis_index("y") + lax.axis_index("c")
    tm = min(512, t)
    tm_bwd = min(256, t)
    tm_tn = min(2048, t)
    tm_down = min(1024, t)
    tt_fwd = min(1024, t)
    tt_bwd = min(512, t)

    sizes = {"wg1": fs, "wu1": fs, "wd1": fs, "win": ins, "wout": outs, "wg2": fs, "wu2": fs, "wd2": fs,
             "conv": 16}
    per_layer = ["wg1", "wu1", "wd1", "win", "wout", "wg2", "wu2", "wd2"]
    wire = jnp.dtype(F32).itemsize // jnp.dtype(MM).itemsize
    n_conv = (SCONV_K + CCONV_K) * cs

    def conv_rows(l):
        flat = jnp.concatenate([sconv_w[l].reshape(-1), cconv_w[l].reshape(-1)])
        bits = lax.bitcast_convert_type(flat, MM).reshape(-1)
        return jnp.pad(bits, (0, sizes["conv"] * d - bits.shape[0])).reshape(sizes["conv"], d)

    def layer_block(l, key):
        if key == "conv":
            return conv_rows(l)
        return {"wg1": ffn1_w_gate[l].T, "wu1": ffn1_w_up[l].T, "wd1": ffn1_w_down[l], "win": mix_w_in[l].T,
                "wout": mix_w_out[l], "wg2": ffn2_w_gate[l].T, "wu2": ffn2_w_up[l].T,
                "wd2": ffn2_w_down[l]}[key].astype(MM)

    def gather_groups(l):
        if l == 0:
            return [["wg1", "wu1"], ["wd1"], ["win", "wout", "wg2", "wu2", "wd2", "conv"]]
        return [per_layer + ["conv"]]

    gathers = {}
    for l in range(depth):
        for gi, keys in enumerate(gather_groups(l)):
            srcs = [layer_block(l, k) for k in keys]
            gathers[l, gi] = copy_start(srcs, [(N_DEV,) + v.shape for v in srcs], gather=True,
                                        name=f"gather_start_{l}_{gi}")
    started = sum(st[4][0:1, 0:1] for st in gathers.values())

    def gathered(l, gi, after):
        mine, lands = copy_wait(gathers[l, gi], after, gather=True, name=f"gather_wait_{l}_{gi}")
        return {k: lax.dynamic_update_slice(land, own[None], (me, 0, 0)).reshape(N_DEV * sizes[k], d)
                for k, own, land in zip(gather_groups(l)[gi], mine, lands)}

    def conv_filters(rows):
        bits = rows.reshape(N_DEV, -1)[:, :n_conv * wire]
        vals = lax.bitcast_convert_type(bits.reshape(N_DEV, n_conv, wire) if wire > 1 else bits, F32)
        both = vals.reshape(N_DEV, SCONV_K + CCONV_K, cs).transpose(1, 0, 2).reshape(SCONV_K + CCONV_K, s)
        return _pad_rows(both[:SCONV_K], 8), _pad_rows(both[SCONV_K:], 32)

    eye = jnp.eye(4, dtype=F32)
    wbd_all = (pool_w[:, :, :, None, :] * eye[None, :, None, :, None]).reshape(depth, p, p)

    def row(v):
        return v.reshape(1, -1)

    saved = []
    cur, gam, bet = x[0], jnp.ones((1, d), F32), jnp.zeros((1, d), F32) + started
    for l in range(depth):
        w = gathered(l, 0, started if l == 0 else cur)
        sv = {"w": w}
        sv["xb1"], sv["g1"], sv["u1"], sv["h1"] = ffn_up(cur, gam, bet, w["wg1"], w["wu1"], tm=tm)
        if l == 0:
            w.update(gathered(l, 1, sv["xb1"]))
        sv["xh1"], sv["rs1"] = mm_res_ln(sv["h1"], w["wd1"], cur, gam, bet, alpha=alpha, scale=0.5, tm=tm_down)
        if l == 0:
            w.update(gathered(l, 2, sv["xh1"]))
        g1, b1 = row(ln1_g[l]), row(ln1_b[l])
        sv["proj"], sv["xb2"] = mm_in(sv["xh1"], g1, b1, w["win"], tm=tm_down)
        sv["wbd"] = wbd_all[l].astype(MM)
        sv["sw"], sv["cw"] = conv_filters(w["conv"])
        sv["cat"], sv["conv"] = mixer_fwd(sv["proj"], sv["wbd"], row(pool_scale[l]), sv["sw"], sv["cw"],
                                          row(cconv_b[l]), row(cnorm_g[l]), row(cnorm_b[l]), d=d, tt=tt_fwd)
        sv["xh2"], sv["rs2"] = mm_res_ln(sv["cat"], w["wout"], sv["xh1"], g1, b1, alpha=alpha, scale=1.0, tm=tm_down)
        g2, b2 = row(ln2_g[l]), row(ln2_b[l])
        sv["xb3"], sv["g3"], sv["u3"], sv["h3"] = ffn_up(sv["xh2"], g2, b2, w["wg2"], w["wu2"], tm=tm)
        sv["xh3"], sv["rs3"] = mm_res_ln(sv["h3"], w["wd2"], sv["xh2"], g2, b2, alpha=alpha, scale=0.5, tm=tm_down)
        saved.append(sv)
        cur, gam, bet = sv["xh3"], row(ln3_g[l]), row(ln3_b[l])

    dcur, lsum = loss_head(cur, gam, bet, loss_target[0], tm=tm)
    loss = lax.psum(lsum[0, 0] * (0.5 / d), MESH_AXES)

    exchanges = []
    small = [None] * depth

    def outbox(keys):
        offs, off = {}, 0
        for k in keys:
            off = -(-off // sizes[k]) * sizes[k]
            offs[k] = off
            off += sizes[k]
        if _row_tile(off, 256) < 64:
            off = -(-off // 256) * 256
        return {"keys": keys, "offs": offs, "rows": off, "buf": lax.empty((N_DEV, off, d), MM), "own": {}}

    def grad_into(box, k, lhs, rhs, after=None):
        g, box["buf"] = tn_matmul(lhs, rhs, tm=tm_tn, after=after, into=(box["buf"], box["offs"][k]))
        box["own"][k] = lax.dynamic_slice_in_dim(g, me * sizes[k], sizes[k], axis=0)

    def exchange(l, box):
        mine, at = [], 0
        for k in box["keys"]:
            if box["offs"][k] > at:
                mine.append(jnp.zeros((box["offs"][k] - at, d), F32))
            mine.append(box["own"][k])
            at = box["offs"][k] + sizes[k]
        if box["rows"] > at:
            mine.append(jnp.zeros((box["rows"] - at, d), F32))
        st = copy_start([box["buf"]], [(N_DEV - 1, box["rows"], d)], gather=False,
                        name=f"exchange_start_{l}_{box['keys'][0]}")
        exchanges.append((l, box, st, jnp.concatenate(mine, axis=0)))
        return st[4][0:1, 0:1]

    sent = jnp.zeros((1, 1), F32)
    for l in reversed(range(depth)):
        sv = saved[l]
        w = sv["w"]
        dx, dyb, dg, du, dg3, db3 = ffn_bwd(dcur, sv["xh3"], sv["rs3"], row(ln3_g[l]) + sent, w["wd2"], w["wg2"],
                                             w["wu2"], sv["g3"], sv["u3"], alpha=alpha, tm=tm_bwd)
        box = outbox(["wg2", "wu2", "wd2", "win", "wout"] + ([] if l == 0 else ["wg1", "wu1", "wd1"]))
        grad_into(box, "wg2", dg, sv["xb3"])
        grad_into(box, "wu2", du, sv["xb3"])
        grad_into(box, "wd2", sv["h3"], dyb)
        dz, dzb, dcat, dg2, db2 = lnbwd_mm(dx, sv["xh2"], sv["rs2"], row(ln2_g[l]), w["wout"], tm=tm_down)
        dproj, dwbd, dps, dsw, dcw, dcb, dcg, dcbt = mixer_bwd(
            sv["proj"], dcat, sv["conv"], sv["wbd"], sv["wbd"].T, row(pool_scale[l]), sv["sw"], sv["cw"],
            row(cnorm_g[l]), row(cnorm_b[l]), d=d, tt=tt_bwd)
        grad_into(box, "wout", sv["cat"], dzb)
        grad_into(box, "win", dproj, sv["xb2"])
        if l == 0:
            sent = exchange(l, box)
        dx = mm_add(dproj, w["win"], dz, alpha=alpha, tm=tm_down)
        dx, dyb, dg, du, dg1, db1 = ffn_bwd(dx, sv["xh1"], sv["rs1"], row(ln1_g[l]) + sent, w["wd1"], w["wg1"],
                                             w["wu1"], sv["g1"], sv["u1"], alpha=alpha, tm=tm_bwd)
        dcur = dx
        for k, lhs, rhs in (("wg1", dg, sv["xb1"]), ("wu1", du, sv["xb1"]), ("wd1", sv["h1"], dyb)):
            if l == 0:
                box = outbox([k])
                grad_into(box, k, lhs, rhs, after=sent)
                sent = exchange(l, box)
            else:
                grad_into(box, k, lhs, rhs)
        if l > 0:
            sent = exchange(l, box)
        dpw = jnp.stack([dwbd[g * pg:(g + 1) * pg, g * pg:(g + 1) * pg] for g in range(4)])
        small[l] = [dg1, db1, dg2, db2, dg3, db3, dpw, dps, dsw[:SCONV_K], dcw[:CCONV_K], dcb, dcg, dcbt]
    grad_x = dcur[None]

    by_key = {k: [None] * depth for k in per_layer}

    def finish(ex, after):
        l, box, st, mine = ex
        recv = copy_wait(st, after, gather=False, name=f"exchange_wait_{l}_{box['keys'][0]}")[1][0]
        gsum = sum_parts(mine, recv)
        for k in box["keys"]:
            by_key[k][l] = gsum[box["offs"][k]:box["offs"][k] + sizes[k]]
        return gsum

    chain = exchanges[-1][3]
    for ex in exchanges[:-1]:
        chain = finish(ex, chain)

    small_shapes = [g.shape for g in small[0]]
    small_flat = _pack_small([g for l in range(depth) for g in small[l]])
    small_sum = sum_gathered(gather_small(small_flat)).reshape(-1)
    small_g = _unpack_small(small_sum, small_shapes * depth)
    n_small = len(small_shapes)

    grads, deltas, new_m, new_v = {}, {}, {}, {}

    def update(name, grad):
        w = a[name]
        c = w.shape[-1]
        dl, mn, vn = adamw(w.reshape(-1, c), grad.reshape(-1, c), a["m_" + name].reshape(-1, c),
                           a["v_" + name].reshape(-1, c))
        grads[name] = grad
        deltas[name], new_m[name], new_v[name] = dl.reshape(w.shape), mn.reshape(w.shape), vn.reshape(w.shape)
        return dl

    small_names = ["ln1_g", "ln1_b", "ln2_g", "ln2_b", "ln3_g", "ln3_b", "pool_w", "pool_scale", "sconv_w",
                   "cconv_w", "cconv_b", "cnorm_g", "cnorm_b"]
    for idx, name in enumerate(small_names):
        full = jnp.stack([small_g[l * n_small + idx] for l in range(depth)])
        if name in ("sconv_w", "cconv_w"):
            full = lax.dynamic_slice_in_dim(full, me * cs, cs, axis=2)
        chain = update(name, full.reshape(a[name].shape))

    big_names = {"wg1": "ffn1_w_gate", "wu1": "ffn1_w_up", "wd1": "ffn1_w_down", "win": "mix_w_in",
                 "wout": "mix_w_out", "wg2": "ffn2_w_gate", "wu2": "ffn2_w_up", "wd2": "ffn2_w_down"}
    sent_transposed = ("wg1", "wu1", "win", "wg2", "wu2")

    def update_big(k):
        return update(big_names[k], jnp.stack([g.T if k in sent_transposed else g for g in by_key[k]]))

    last_keys = exchanges[-1][1]["keys"]
    for k in per_layer:
        if k not in last_keys:
            chain = update_big(k)
    finish(exchanges[-1], chain)
    for k in last_keys:
        update_big(k)

    return (loss, grad_x, *[grads[n] for n in WEIGHTS], *[deltas[n] for n in WEIGHTS],
            *[new_m[n] for n in WEIGHTS], *[new_v[n] for n in WEIGHTS])
```

```python
import functools

import jax
import jax.numpy as jnp
from jax import lax
from jax.experimental import pallas as pl
from jax.experimental.pallas import tpu as pltpu

F32 = jnp.float32
MM = jnp.bfloat16
LN_EPS = 1e-5
N_DEV = 8
MESH_AXES = ("x", "y", "c")
HALO = 32
CCONV_K = 31
SCONV_K = 3
CONV_ROWS = 32
FFN_CHUNK = 256
ROW_PIECE = 256
VMEM_LIMIT = 56 * 1024 * 1024
ADAM_LR, ADAM_B1, ADAM_B2, ADAM_EPS, ADAM_WD, ADAM_STEP = 0.001, 0.9, 0.999, 1e-08, 0.01, 10
MESH = pl.DeviceIdType.MESH


def _call(body, *, name, out_shape, in_specs, out_specs, grid=None, scratch=(), sem=None, aliases=None):
    kw = {}
    if grid is not None:
        kw["grid"] = grid
    if aliases:
        kw["input_output_aliases"] = aliases
    params = dict(vmem_limit_bytes=VMEM_LIMIT)
    if sem is not None:
        params["dimension_semantics"] = sem
    return pl.pallas_call(body, name=name, out_shape=out_shape, in_specs=in_specs, out_specs=out_specs,
                          scratch_shapes=list(scratch), compiler_params=pltpu.CompilerParams(**params), **kw)


def _rows(tm, n):
    return pl.BlockSpec((tm, n), lambda i: (i, 0))


def _const(shape):
    nd = len(shape)
    return pl.BlockSpec(shape, lambda *_: (0,) * nd, pipeline_mode=pl.Buffered(1))


def _acc(shape):
    nd = len(shape)
    return pl.BlockSpec(shape, lambda *_: (0,) * nd)


def _row_tile(rows, cap):
    best = None
    for t in range(8, min(rows, cap) + 1, 8):
        if rows % t == 0:
            best = t
    return best if best is not None else rows


def _sig(x):
    return 1.0 / (1.0 + jnp.exp(-x))


def _ln_stats(z):
    mu = jnp.mean(z, axis=-1, keepdims=True)
    zc = z - mu
    var = jnp.mean(zc * zc, axis=-1, keepdims=True)
    rstd = lax.rsqrt(var + LN_EPS)
    return zc * rstd, rstd


def _ln_bwd(dout, xhat, rstd, gamma):
    dxh = dout * gamma
    m1 = jnp.mean(dxh, axis=-1, keepdims=True)
    m2 = jnp.mean(dxh * xhat, axis=-1, keepdims=True)
    return rstd * (dxh - m1 - xhat * m2)


def _colsum(v):
    return jnp.sum(v, axis=0, keepdims=True)


def _row_pieces(tm):
    step = ROW_PIECE if tm % ROW_PIECE == 0 else tm
    return [slice(r, r + step) for r in range(0, tm, step)]


def _f_chunks(f, width=None):
    if width is not None and f % width == 0:
        return f // width, width
    n = 2 if f >= 2048 and f % 256 == 0 else 1
    return n, f // n


def _dot_nt(a, b):
    return lax.dot_general(a, b, (((1,), (1,)), ((), ())), preferred_element_type=F32)


def ffn_up(xin, gam, bet, wgt, wut, *, tm):
    t, d = xin.shape
    f = wgt.shape[0]
    nch, fc = _f_chunks(f, FFN_CHUNK)

    def body(x_ref, g_ref, b_ref, wg_ref, wu_ref, xb_ref, go_ref, uo_ref, h_ref):
        xb = (x_ref[...] * g_ref[...] + b_ref[...]).astype(MM)
        xb_ref[...] = xb
        for c in range(nch):
            sl = slice(c * fc, (c + 1) * fc)
            g = _dot_nt(xb, wg_ref[sl, :])
            u = _dot_nt(xb, wu_ref[sl, :])
            go_ref[:, sl] = g.astype(MM)
            uo_ref[:, sl] = u.astype(MM)
            h_ref[:, sl] = (g * _sig(g) * u).astype(MM)

    return _call(
        body, name="ffn_up", grid=(t // tm,), sem=("parallel",),
        in_specs=[_rows(tm, d), _const((1, d)), _const((1, d)), _const((f, d)), _const((f, d))],
        out_specs=[_rows(tm, d), _rows(tm, f), _rows(tm, f), _rows(tm, f)],
        out_shape=[jax.ShapeDtypeStruct((t, d), MM)] + [jax.ShapeDtypeStruct((t, f), MM)] * 3,
    )(xin, gam, bet, wgt, wut)


def mm_res_ln(a, w, xin, gam, bet, *, alpha, scale, tm):
    t, k = a.shape
    d = w.shape[1]

    def body(a_ref, w_ref, x_ref, g_ref, b_ref, xh_ref, rs_ref):
        for rows in _row_pieces(tm):
            y = jnp.dot(a_ref[rows, :], w_ref[...], preferred_element_type=F32)
            x = x_ref[rows, :] * g_ref[...] + b_ref[...]
            xh, rstd = _ln_stats(alpha * x + scale * y)
            xh_ref[rows, :] = xh
            rs_ref[rows, :] = rstd

    return _call(
        body, name="mm_res_ln", grid=(t // tm,), sem=("parallel",),
        in_specs=[_rows(tm, k), _const((k, d)), _rows(tm, d), _const((1, d)), _const((1, d))],
        out_specs=[_rows(tm, d), _rows(tm, 1)],
        out_shape=[jax.ShapeDtypeStruct((t, d), F32), jax.ShapeDtypeStruct((t, 1), F32)],
    )(a, w, xin, gam, bet)


def mm_in(xin, gam, bet, wt, *, tm):
    t, d = xin.shape
    n = wt.shape[0]

    def body(x_ref, g_ref, b_ref, w_ref, o_ref, xb_ref):
        for rows in _row_pieces(tm):
            xb = (x_ref[rows, :] * g_ref[...] + b_ref[...]).astype(MM)
            xb_ref[rows, :] = xb
            o_ref[rows, :] = _dot_nt(xb, w_ref[...]).astype(MM)

    return _call(
        body, name="mm_in", grid=(t // tm,), sem=("parallel",),
        in_specs=[_rows(tm, d), _const((1, d)), _const((1, d)), _const((n, d))],
        out_specs=[_rows(tm, n), _rows(tm, d)],
        out_shape=[jax.ShapeDtypeStruct((t, n), MM), jax.ShapeDtypeStruct((t, d), MM)],
    )(xin, gam, bet, wt)


def loss_head(xh, gam, bet, target, *, tm):
    t, d = xh.shape

    def body(x_ref, g_ref, b_ref, t_ref, dy_ref, l_ref):
        @pl.when(pl.program_id(0) == 0)
        def _():
            l_ref[...] = jnp.zeros_like(l_ref)

        e = x_ref[...] * g_ref[...] + b_ref[...] - t_ref[...]
        dy_ref[...] = e * (1.0 / d)
        l_ref[...] += jnp.sum(_colsum(e * e), axis=1, keepdims=True)

    return _call(
        body, name="loss_head", grid=(t // tm,), sem=("arbitrary",),
        in_specs=[_rows(tm, d), _const((1, d)), _const((1, d)), _rows(tm, d)],
        out_specs=[_rows(tm, d), _acc((1, 1))],
        out_shape=[jax.ShapeDtypeStruct((t, d), F32), jax.ShapeDtypeStruct((1, 1), F32)],
    )(xh, gam, bet, target)


def _halo_specs(tt, ncols, t):
    per, last = tt // HALO, t // HALO - 1
    return [pl.BlockSpec((HALO, ncols), lambda i: (jnp.maximum(i * per - 1, 0), 0)),
            pl.BlockSpec((tt, ncols), lambda i: (i, 0)),
            pl.BlockSpec((HALO, ncols), lambda i: (jnp.minimum((i + 1) * per, last), 0))]


def _fill_ext(ext, prev_ref, cur_ref, next_ref, i, nt, tt):
    ext[0:HALO, :] = jnp.where(i > 0, prev_ref[...].astype(F32), 0.0)
    ext[HALO:HALO + tt, :] = cur_ref[...].astype(F32)
    ext[HALO + tt:HALO + tt + HALO, :] = jnp.where(i < nt - 1, next_ref[...].astype(F32), 0.0)


def _make_shifts(sh, n):
    for r in range(1, 8):
        sh[r, 0:n, :] = sh[0, r:r + n, :]


def _shift_reader(sh):
    def read(o, rows):
        r = o % 8
        return sh[r, o - r:o - r + rows, :]
    return read


def _conv_chunks_shifted(sh, w_ref, ktaps, src0, nrows, flip=False):
    for r0 in range(0, nrows, CONV_ROWS):
        acc = None
        base = src0 + r0
        for r in range(8):
            ks = [k for k in range(ktaps) if (base + k) % 8 == r]
            if not ks:
                continue
            lo = base + ks[0] - r
            slab = sh[r, lo:lo + CONV_ROWS + 8 * (len(ks) - 1), :]
            for q, k in enumerate(ks):
                kk = ktaps - 1 - k if flip else k
                term = w_ref[kk:kk + 1, :] * slab[8 * q:8 * q + CONV_ROWS, :]
                acc = term if acc is None else acc + term
        yield r0, acc


def _ref_reader(ref):
    def read(o, rows):
        return ref[o:o + rows, :]
    return read


def _conv_chunks(read, w_ref, ktaps, src0, nrows, flip=False):
    for r0 in range(0, nrows, CONV_ROWS):
        acc = None
        for k in range(ktaps):
            kk = ktaps - 1 - k if flip else k
            term = w_ref[kk:kk + 1, :] * read(src0 + r0 + k, CONV_ROWS)
            acc = term if acc is None else acc + term
        yield r0, acc


def _pool_groups(nrows, p):
    lane = lax.broadcasted_iota(jnp.int32, (nrows, p), 1)
    g = p // 4
    return lane < g, lane < 2 * g, lane < 3 * g


def _pool_select(groups, v2, v4, v8, v16):
    g0, g1, g2 = groups
    return jnp.where(g0, v2, jnp.where(g1, v4, jnp.where(g2, v8, v16)))


def _pool_count(groups, i, tt, row0, nrows, p, t):
    pos = i * tt + (row0 - HALO) + lax.broadcasted_iota(jnp.int32, (nrows, p), 0)
    half = _pool_select(groups, 1, 2, 4, 8)
    lo = jnp.clip(pos - half, 0, t)
    hi = jnp.clip(pos + half, 0, t)
    return jnp.maximum(hi - lo, 1).astype(F32)


def _pool_forward(ext, s2, s4, s8, groups, cnt, tt, p):
    e = tt + 2 * HALO
    s2[8:e - 8, :] = ext[7:e - 9, 0:p] + ext[8:e - 8, 0:p]
    s4[16:e - 16, :] = s2[15:e - 17, :] + s2[17:e - 15, :]
    s8[24:e - 24, :] = s4[22:e - 26, :] + s4[26:e - 22, :]
    s16 = s8[28:e - 36, :] + s8[36:e - 28, :]
    c = slice(HALO, HALO + tt)
    tot = _pool_select(groups, s2[c, :], s4[c, :], s8[c, :], s16)
    return tot / cnt - ext[c, 0:p]


def mixer_fwd(proj, wbd, pscale, sw, cw, cb, cg, cbt, *, d, tt):
    t, ncols = proj.shape
    p, s = d // 4, 3 * d // 8
    o_gb, o_gc, o_v, o_cv, o_cg = p, p + s, p + 2 * s, p + 3 * s, p + 4 * s
    nt, e = t // tt, tt + 2 * HALO

    def body(prev_ref, cur_ref, next_ref, wbd_ref, ps_ref, sw_ref, cw_ref, cb_ref, cg_ref, cbt_ref,
             cat_ref, conv_ref, ext, a_sh, cv_s, s2, s4, s8):
        i = pl.program_id(0)
        _fill_ext(ext, prev_ref, cur_ref, next_ref, i, nt, tt)
        a_sh[0, 0:e, :] = ext[:, o_cv:o_cv + s] * _sig(ext[:, o_cg:o_cg + s])
        a_sh[0, e:e + 8, :] = jnp.zeros((8, s), F32)
        _make_shifts(a_sh, e)
        for r0, acc in _conv_chunks_shifted(a_sh, cw_ref, CCONV_K, HALO - CCONV_K // 2, tt):
            b = acc + cb_ref[...]
            conv_ref[r0:r0 + CONV_ROWS, :] = b
            n, _ = _ln_stats(b)
            yn = n * cg_ref[...] + cbt_ref[...]
            cat_ref[r0:r0 + CONV_ROWS, p + s:d] = (yn * _sig(yn)).astype(MM)
        cv_s[...] = ext[:, o_gc:o_gc + s] * ext[:, o_v:o_v + s]
        for r0, acc in _conv_chunks(_ref_reader(cv_s), sw_ref, SCONV_K, HALO - SCONV_K // 2, tt):
            gb = ext[HALO + r0:HALO + r0 + CONV_ROWS, o_gb:o_gb + s]
            cat_ref[r0:r0 + CONV_ROWS, p:p + s] = (gb * acc).astype(MM)
        groups = _pool_groups(tt, p)
        cnt = _pool_count(groups, i, tt, HALO, tt, p, t)
        pooled = _pool_forward(ext, s2, s4, s8, groups, cnt, tt, p)
        ya = jnp.dot(pooled.astype(MM), wbd_ref[...], preferred_element_type=F32) * ps_ref[...]
        cat_ref[:, 0:p] = ya.astype(MM)

    return _call(
        body, name="mixer_fwd", grid=(nt,), sem=("parallel",),
        in_specs=_halo_specs(tt, ncols, t) + [_const((p, p)), _const((1, p)), _const((8, s)), _const((32, s)),
                                               _const((1, s)), _const((1, s)), _const((1, s))],
        out_specs=[_rows(tt, d), _rows(tt, s)],
        out_shape=[jax.ShapeDtypeStruct((t, d), MM), jax.ShapeDtypeStruct((t, s), F32)],
        scratch=[pltpu.VMEM((e, ncols), F32), pltpu.VMEM((8, e + 8, s), F32), pltpu.VMEM((e, s), F32),
                 pltpu.VMEM((e, p), F32), pltpu.VMEM((e, p), F32), pltpu.VMEM((e, p), F32)],
    )(proj, proj, proj, wbd, pscale, sw, cw, cb, cg, cbt)


def mixer_bwd(proj, dcat, conv, wbd, wbdt, pscale, sw, cw, cg, cbt, *, d, tt):
    t, ncols = proj.shape
    p, s = d // 4, 3 * d // 8
    o_gb, o_gc, o_v, o_cv, o_cg = p, p + s, p + 2 * s, p + 3 * s, p + 4 * s
    nt, e = t // tt, tt + 2 * HALO

    def body(pp_ref, pc_ref, pn_ref, dp_ref, dc_ref, dn_ref, bp_ref, bc_ref, bn_ref, wbd_ref, wbdt_ref, ps_ref,
             sw_ref, cw_ref, cg_ref, cbt_ref,
             dproj_ref, dwbd_ref, dps_ref, dsw_ref, dcw_ref, dcb_ref, dcg_ref, dcbt_ref,
             ext, dext, bext, a_sh, b_sh, sg_s, cv_s, ds_s, q_s, r2, r4, r8):
        i = pl.program_id(0)

        @pl.when(i == 0)
        def _():
            for ref in (dwbd_ref, dps_ref, dsw_ref, dcw_ref, dcb_ref, dcg_ref, dcbt_ref):
                ref[...] = jnp.zeros_like(ref)

        _fill_ext(ext, pp_ref, pc_ref, pn_ref, i, nt, tt)
        _fill_ext(dext, dp_ref, dc_ref, dn_ref, i, nt, tt)
        _fill_ext(bext, bp_ref, bc_ref, bn_ref, i, nt, tt)
        c = slice(HALO, HALO + tt)

        sg_s[...] = _sig(ext[:, o_cg:o_cg + s])
        a_sh[0, 0:e, :] = ext[:, o_cv:o_cv + s] * sg_s[...]
        a_sh[0, e:e + 8, :] = jnp.zeros((8, s), F32)
        _make_shifts(a_sh, e)
        read_a = _shift_reader(a_sh)
        for r0 in range(0, e, CONV_ROWS):
            b = bext[r0:r0 + CONV_ROWS, :]
            n, rstd = _ln_stats(b)
            yn = n * cg_ref[...] + cbt_ref[...]
            sy = _sig(yn)
            rows = slice(r0, r0 + CONV_ROWS)
            dyn = dext[rows, p + s:d] * (sy * (1.0 + yn * (1.0 - sy)))
            db = _ln_bwd(dyn, n, rstd, cg_ref[...])
            b_sh[0, rows, :] = db
            if HALO <= r0 < HALO + tt:
                dcg_ref[...] += _colsum(dyn * n)
                dcbt_ref[...] += _colsum(dyn)
                dcb_ref[...] += _colsum(db)
        b_sh[0, e:e + 8, :] = jnp.zeros((8, s), F32)
        _make_shifts(b_sh, e)
        for r0, da in _conv_chunks_shifted(b_sh, cw_ref, CCONV_K, HALO - CCONV_K // 2, tt, flip=True):
            rows = slice(HALO + r0, HALO + r0 + CONV_ROWS)
            sg = sg_s[rows, :]
            dproj_ref[r0:r0 + CONV_ROWS, o_cv:o_cv + s] = (da * sg).astype(MM)
            dproj_ref[r0:r0 + CONV_ROWS, o_cg:o_cg + s] = (
                da * ext[rows, o_cv:o_cv + s] * sg * (1.0 - sg)).astype(MM)
        for k in range(CCONV_K):
            lo = HALO + k - CCONV_K // 2
            dcw_ref[k:k + 1, :] += _colsum(b_sh[0, c, :] * read_a(lo, tt))

        cv_s[...] = ext[:, o_gc:o_gc + s] * ext[:, o_v:o_v + s]
        ds_s[...] = dext[:, p:p + s] * ext[:, o_gb:o_gb + s]
        for r0, acc in _conv_chunks(_ref_reader(cv_s), sw_ref, SCONV_K, HALO - SCONV_K // 2, tt):
            rows = slice(HALO + r0, HALO + r0 + CONV_ROWS)
            dproj_ref[r0:r0 + CONV_ROWS, o_gb:o_gb + s] = (dext[rows, p:p + s] * acc).astype(MM)
        for r0, dcv in _conv_chunks(_ref_reader(ds_s), sw_ref, SCONV_K, HALO - SCONV_K // 2, tt, flip=True):
            rows = slice(HALO + r0, HALO + r0 + CONV_ROWS)
            dproj_ref[r0:r0 + CONV_ROWS, o_gc:o_gc + s] = (dcv * ext[rows, o_v:o_v + s]).astype(MM)
            dproj_ref[r0:r0 + CONV_ROWS, o_v:o_v + s] = (dcv * ext[rows, o_gc:o_gc + s]).astype(MM)
        for k in range(SCONV_K):
            lo = HALO + k - SCONV_K // 2
            dsw_ref[k:k + 1, :] += _colsum(ds_s[c, :] * cv_s[lo:lo + tt, :])

        groups = _pool_groups(tt, p)
        cnt = _pool_count(groups, i, tt, HALO, tt, p, t)
        pooled = _pool_forward(ext, r2, r4, r8, groups, cnt, tt, p).astype(MM)
        ta = jnp.dot(pooled, wbd_ref[...], preferred_element_type=F32)
        dps_ref[...] += _colsum(dext[c, 0:p] * ta)
        dta = (dext[:, 0:p] * ps_ref[...]).astype(MM)
        dwbd_ref[...] += lax.dot_general(pooled, dta[HALO:HALO + tt, :], (((0,), (0,)), ((), ())),
                                         preferred_element_type=F32)
        dpool = jnp.dot(dta, wbdt_ref[...], preferred_element_type=F32)
        groups_e = _pool_groups(e, p)
        q_s[...] = dpool / _pool_count(groups_e, i, tt, 0, e, p, t)
        r2[8:e - 8, :] = q_s[8:e - 8, :] + q_s[9:e - 7, :]
        r4[16:e - 16, :] = r2[15:e - 17, :] + r2[17:e - 15, :]
        r8[24:e - 24, :] = r4[22:e - 26, :] + r4[26:e - 22, :]
        r16 = r8[28:e - 36, :] + r8[36:e - 28, :]
        du = _pool_select(groups, r2[c, :], r4[c, :], r8[c, :], r16) - dpool[HALO:HALO + tt, :]
        dproj_ref[:, 0:p] = du.astype(MM)

    small = [(p, p), (1, p), (8, s), (32, s), (1, s), (1, s), (1, s)]
    return _call(
        body, name="mixer_bwd", grid=(nt,), sem=("arbitrary",),
        in_specs=_halo_specs(tt, ncols, t) + _halo_specs(tt, d, t) + _halo_specs(tt, s, t) + [
            _const((p, p)), _const((p, p)), _const((1, p)), _const((8, s)), _const((32, s)),
            _const((1, s)), _const((1, s))],
        out_specs=[_rows(tt, ncols)] + [_acc(sh) for sh in small],
        out_shape=[jax.ShapeDtypeStruct((t, ncols), MM)] + [jax.ShapeDtypeStruct(sh, F32) for sh in small],
        scratch=[pltpu.VMEM((e, ncols), F32), pltpu.VMEM((e, d), F32), pltpu.VMEM((e, s), F32),
                 pltpu.VMEM((8, e + 8, s), F32), pltpu.VMEM((8, e + 8, s), F32)]
        + [pltpu.VMEM((e, s), F32)] * 3 + [pltpu.VMEM((e, p), F32)] * 4,
    )(proj, proj, proj, dcat, dcat, dcat, conv, conv, conv, wbd, wbdt, pscale, sw, cw, cg, cbt)


def ffn_bwd(dout, xh, rs, gam, wd, wgt, wut, g, u, *, alpha, tm):
    t, d = dout.shape
    f = g.shape[1]
    nch, fc = _f_chunks(f, FFN_CHUNK)

    def body(do_ref, xh_ref, rs_ref, gm_ref, wd_ref, wgt_ref, wut_ref, g_ref, u_ref,
             dx_ref, dyb_ref, dg_ref, du_ref, dgam_ref, dbet_ref):
        @pl.when(pl.program_id(0) == 0)
        def _():
            dgam_ref[...] = jnp.zeros_like(dgam_ref)
            dbet_ref[...] = jnp.zeros_like(dbet_ref)

        for rows in _row_pieces(tm):
            dout_v, xhat = do_ref[rows, :], xh_ref[rows, :]
            dgam_ref[...] += _colsum(dout_v * xhat)
            dbet_ref[...] += _colsum(dout_v)
            dz = _ln_bwd(dout_v, xhat, rs_ref[rows, :], gm_ref[...])
            dyb = (0.5 * dz).astype(MM)
            dyb_ref[rows, :] = dyb
            for c in range(nch):
                sl = slice(c * fc, (c + 1) * fc)
                dh = _dot_nt(dyb, wd_ref[sl, :])
                gv, uv = g_ref[rows, sl].astype(F32), u_ref[rows, sl].astype(F32)
                sg = _sig(gv)
                dg_ref[rows, sl] = (dh * uv * (sg * (1.0 + gv * (1.0 - sg)))).astype(MM)
                du_ref[rows, sl] = (dh * (gv * sg)).astype(MM)
            dx_ref[rows, :] = (alpha * dz + jnp.dot(dg_ref[rows, :], wgt_ref[...], preferred_element_type=F32)
                               + jnp.dot(du_ref[rows, :], wut_ref[...], preferred_element_type=F32))

    return _call(
        body, name="ffn_bwd", grid=(t // tm,), sem=("arbitrary",),
        in_specs=[_rows(tm, d), _rows(tm, d), _rows(tm, 1), _const((1, d)), _const((f, d)), _const((f, d)),
                  _const((f, d)), _rows(tm, f), _rows(tm, f)],
        out_specs=[_rows(tm, d), _rows(tm, d), _rows(tm, f), _rows(tm, f), _acc((1, d)), _acc((1, d))],
        out_shape=[jax.ShapeDtypeStruct((t, d), F32), jax.ShapeDtypeStruct((t, d), MM),
                   jax.ShapeDtypeStruct((t, f), MM), jax.ShapeDtypeStruct((t, f), MM),
                   jax.ShapeDtypeStruct((1, d), F32), jax.ShapeDtypeStruct((1, d), F32)],
    )(dout, xh, rs, gam, wd, wgt, wut, g, u)


def lnbwd_mm(dout, xh, rs, gam, w, *, tm):
    t, d = dout.shape
    n = w.shape[0]

    def body(do_ref, xh_ref, rs_ref, gm_ref, wt_ref, dz_ref, dzb_ref, da_ref, dgam_ref, dbet_ref):
        @pl.when(pl.program_id(0) == 0)
        def _():
            dgam_ref[...] = jnp.zeros_like(dgam_ref)
            dbet_ref[...] = jnp.zeros_like(dbet_ref)

        for rows in _row_pieces(tm):
            dout_v, xhat = do_ref[rows, :], xh_ref[rows, :]
            dgam_ref[...] += _colsum(dout_v * xhat)
            dbet_ref[...] += _colsum(dout_v)
            dz = _ln_bwd(dout_v, xhat, rs_ref[rows, :], gm_ref[...])
            dz_ref[rows, :] = dz
            dzb = dz.astype(MM)
            dzb_ref[rows, :] = dzb
            da_ref[rows, :] = _dot_nt(dzb, wt_ref[...]).astype(MM)

    return _call(
        body, name="lnbwd_mm", grid=(t // tm,), sem=("arbitrary",),
        in_specs=[_rows(tm, d), _rows(tm, d), _rows(tm, 1), _const((1, d)), _const((n, d))],
        out_specs=[_rows(tm, d), _rows(tm, d), _rows(tm, n), _acc((1, d)), _acc((1, d))],
        out_shape=[jax.ShapeDtypeStruct((t, d), F32), jax.ShapeDtypeStruct((t, d), MM),
                   jax.ShapeDtypeStruct((t, n), MM), jax.ShapeDtypeStruct((1, d), F32),
                   jax.ShapeDtypeStruct((1, d), F32)],
    )(dout, xh, rs, gam, w)


def mm_add(a, w, r, *, alpha, tm):
    t, k = a.shape
    d = w.shape[1]

    def body(a_ref, w_ref, r_ref, o_ref):
        for rows in _row_pieces(tm):
            o_ref[rows, :] = (jnp.dot(a_ref[rows, :], w_ref[...], preferred_element_type=F32)
                              + alpha * r_ref[rows, :])

    return _call(
        body, name="mm_add", grid=(t // tm,), sem=("parallel",),
        in_specs=[_rows(tm, k), _const((k, d)), _rows(tm, d)],
        out_specs=_rows(tm, d),
        out_shape=jax.ShapeDtypeStruct((t, d), F32),
    )(a, w, r)


def tn_matmul(a, b, *, tm, after=None, into=None):
    t, n = a.shape
    d = b.shape[1]
    nch, nc = _f_chunks(n)
    steps = t // tm
    per = n // N_DEV
    slots = N_DEV // nch

    def body(a_ref, b_ref, *rest):
        o_ref, ob_ref = rest[-2:]

        @pl.when(pl.program_id(1) == 0)
        def _():
            o_ref[...] = jnp.zeros_like(o_ref)

        o_ref[...] += lax.dot_general(a_ref[...], b_ref[...], (((0,), (0,)), ((), ())),
                                      preferred_element_type=F32)

        @pl.when(pl.program_id(1) == steps - 1)
        def _():
            if into is None:
                ob_ref[...] = o_ref[...].astype(MM)
            else:
                for s in range(slots):
                    ob_ref[s] = o_ref[s * per:(s + 1) * per, :].astype(MM)

    extra = [] if after is None else [after]
    if into is None:
        ob_spec, ob_shape, alias = pl.BlockSpec((nc, d), lambda j, i: (j, 0)), jax.ShapeDtypeStruct((n, d), MM), {}
    else:
        buf, off = into
        assert off % per == 0 and buf.shape[0] == N_DEV and buf.dtype == MM
        extra.append(buf)
        ob_spec = pl.BlockSpec((slots, per, d), lambda j, i: (j, off // per, 0))
        ob_shape, alias = jax.ShapeDtypeStruct(buf.shape, MM), {1 + len(extra): 1}
    return _call(
        body, name="tn_matmul", grid=(nch, steps), sem=("parallel", "arbitrary"), aliases=alias,
        in_specs=[pl.BlockSpec((tm, nc), lambda j, i: (i, j)), pl.BlockSpec((tm, d), lambda j, i: (i, 0))]
        + [pl.BlockSpec(memory_space=pl.ANY)] * len(extra),
        out_specs=[pl.BlockSpec((nc, d), lambda j, i: (j, 0)), ob_spec],
        out_shape=[jax.ShapeDtypeStruct((n, d), F32), ob_shape],
    )(a, b, *extra)


def sum_parts(own, recv):
    r, d = own.shape
    n = recv.shape[0]
    tr = _row_tile(r, 256)

    def body(own_ref, recv_ref, o_ref):
        acc = own_ref[...]
        for k in range(n):
            acc = acc + recv_ref[k].astype(F32)
        o_ref[...] = acc

    return _call(
        body, name="sum_parts", grid=(r // tr,), sem=("parallel",),
        in_specs=[_rows(tr, d), pl.BlockSpec((n, tr, d), lambda i: (0, i, 0))],
        out_specs=_rows(tr, d),
        out_shape=jax.ShapeDtypeStruct((r, d), F32),
    )(own, recv)


def sum_gathered(parts):
    _, r, n = parts.shape

    def body(p_ref, o_ref):
        acc = p_ref[0]
        for j in range(1, N_DEV):
            acc = acc + p_ref[j]
        o_ref[...] = acc

    return _call(
        body, name="sum_gathered",
        in_specs=[pl.BlockSpec(memory_space=pltpu.VMEM)], out_specs=pl.BlockSpec(memory_space=pltpu.VMEM),
        out_shape=jax.ShapeDtypeStruct((r, n), F32),
    )(parts)


def adamw(w, g, m, v):
    r, c = w.shape
    tr = _row_tile(r, 512)

    def body(w_ref, g_ref, m_ref, v_ref, d_ref, mo_ref, vo_ref):
        gv = g_ref[...]
        mn = ADAM_B1 * m_ref[...] + (1.0 - ADAM_B1) * gv
        vn = ADAM_B2 * v_ref[...] + (1.0 - ADAM_B2) * (gv * gv)
        m_hat = mn / (1.0 - ADAM_B1 ** ADAM_STEP)
        v_hat = vn / (1.0 - ADAM_B2 ** ADAM_STEP)
        d_ref[...] = -ADAM_LR * (m_hat / (jnp.sqrt(v_hat) + ADAM_EPS) + ADAM_WD * w_ref[...])
        mo_ref[...] = mn
        vo_ref[...] = vn

    return _call(
        body, name="adamw", grid=(r // tr,), sem=("parallel",),
        in_specs=[_rows(tr, c)] * 4, out_specs=[_rows(tr, c)] * 3,
        out_shape=[jax.ShapeDtypeStruct((r, c), F32)] * 3,
    )(w, g, m, v)


def _mesh_pos():
    return lax.axis_index("x"), lax.axis_index("y"), lax.axis_index("c")


def _two_level_gather(x_ref, out_ref, send_sems, recv_sems, local_sem):
    x, y, c = _mesh_pos()
    me, sibling = (x, y, c), (x, y, 1 - c)
    chips = [(1 - x, y), (x, 1 - y), (1 - x, 1 - y)]

    def slot(px, py, pc):
        return out_ref.at[4 * px + 2 * py + pc]

    def copy(k, block, to, src=None):
        return pltpu.make_async_remote_copy(
            src_ref=slot(*block) if src is None else src, dst_ref=slot(*block),
            send_sem=send_sems.at[k], recv_sem=recv_sems.at[k], device_id=to, device_id_type=MESH)

    mine = pltpu.make_async_copy(x_ref, slot(*me), local_sem)
    mine.start()
    first = [copy(1 + j, me, (*chip, c), src=x_ref) for j, chip in enumerate(chips)]
    first.append(copy(0, me, sibling, src=x_ref))
    for cp in first:
        cp.start()
    passed = [copy(4 + j, (*chip, c), sibling) for j, chip in enumerate(chips)]
    for j, chip in enumerate(chips):
        copy(1 + j, (*chip, c), me).wait_recv()
        passed[j].start()
    copy(0, sibling, me).wait_recv()
    for j, chip in enumerate(chips):
        copy(4 + j, (*chip, 1 - c), me).wait_recv()
    for cp in first + passed:
        cp.wait_send()
    mine.wait()


def gather_small(x):
    return _call(
        _two_level_gather_body(), name="gather_small",
        in_specs=[pl.BlockSpec(memory_space=pltpu.VMEM)], out_specs=pl.BlockSpec(memory_space=pltpu.VMEM),
        out_shape=jax.ShapeDtypeStruct((N_DEV,) + x.shape, x.dtype),
        scratch=[pltpu.SemaphoreType.DMA((7,)), pltpu.SemaphoreType.DMA((7,)), pltpu.SemaphoreType.DMA(())],
    )(x)


def _two_level_gather_body():
    def body(x_ref, out_ref, send_sems, recv_sems, local_sem):
        _two_level_gather(x_ref, out_ref, send_sems, recv_sems, local_sem)
    return body


_HBM = pl.BlockSpec(memory_space=pltpu.HBM)
_SEM = pl.BlockSpec(memory_space=pltpu.SEMAPHORE)
_EFFECT = pltpu.SideEffectType.DATAFLOW_SIDE_EFFECTING
_RELATIONS = (4, 2, 6, 5, 3, 7, 1)


def _split_copy(src_ref, land_ref, send_sems, recv_sems, k, gather, a):
    x, y, c = _mesh_pos()
    px, py, pc = (1 - x if k & 4 else x), (1 - y if k & 2 else y), (1 - c if k & 1 else c)
    if gather:
        src, dst = src_ref, land_ref.at[4 * x + 2 * y + c]
    else:
        src, dst = src_ref.at[4 * px + 2 * py + pc], land_ref.at[k - 1]
    return pltpu.make_async_remote_copy(src_ref=src, dst_ref=dst, send_sem=send_sems.at[7 * a + k - 1],
                                        recv_sem=recv_sems.at[7 * a + k - 1], device_id=(px, py, pc),
                                        device_id_type=MESH)


def copy_start(srcs, land_shapes, *, gather, name):
    m = len(srcs)

    def body(*refs):
        send_sems, recv_sems = refs[2 * m], refs[2 * m + 1]
        for a in range(m):
            for k in _RELATIONS:
                _split_copy(refs[a], refs[m + a], send_sems, recv_sems, k, gather, a).start()
        refs[-1][...] = jnp.zeros_like(refs[-1])

    res = pl.pallas_call(
        body, name=name,
        out_shape=(pltpu.SemaphoreType.DMA((7 * m,)), pltpu.SemaphoreType.DMA((7 * m,)),
                   *[pltpu.HBM(s.shape, s.dtype) for s in srcs],
                   *[pltpu.HBM(ls, s.dtype) for ls, s in zip(land_shapes, srcs)], jax.ShapeDtypeStruct((8, 128), F32)),
        in_specs=(_HBM,) * (2 * m),
        out_specs=(_SEM, _SEM) + (_HBM,) * (2 * m) + (pl.BlockSpec(memory_space=pltpu.VMEM),),
        input_output_aliases={i: 2 + i for i in range(2 * m)},
        compiler_params=pltpu.CompilerParams(has_side_effects=_EFFECT),
    )(*[pltpu.with_memory_space_constraint(s, pltpu.HBM) for s in srcs],
      *[pltpu.with_memory_space_constraint(lax.empty(ls, s.dtype), pltpu.HBM) for ls, s in zip(land_shapes, srcs)])
    return res[0], res[1], list(res[2:2 + m]), list(res[2 + m:2 + 2 * m]), res[-1]


def copy_wait(started, after, *, gather, name):
    send_sems, recv_sems, srcs, lands, _ = started
    m = len(srcs)

    def body(*refs):
        send_sems, recv_sems = refs[2 * m], refs[2 * m + 1]
        for a in range(m):
            for k in _RELATIONS:
                cp = _split_copy(refs[a], refs[m + a], send_sems, recv_sems, k, gather, a)
                cp.wait_send()
                cp.wait_recv()

    res = pl.pallas_call(
        body, name=name,
        out_shape=tuple(pltpu.HBM(v.shape, v.dtype) for v in srcs + lands),
        in_specs=(_HBM,) * (2 * m) + (_SEM, _SEM, pl.BlockSpec(memory_space=pl.ANY)), out_specs=(_HBM,) * (2 * m),
        input_output_aliases={i: i for i in range(2 * m)},
        compiler_params=pltpu.CompilerParams(has_side_effects=_EFFECT),
    )(*srcs, *lands, send_sems, recv_sems, after)
    return list(res[:m]), list(res[m:])


WEIGHTS = ['ln1_g', 'ln1_b', 'ffn1_w_gate', 'ffn1_w_up', 'ffn1_w_down', 'mix_w_in', 'pool_w', 'pool_scale',
           'sconv_w', 'cconv_w', 'cconv_b', 'cnorm_g', 'cnorm_b', 'mix_w_out', 'ln2_g', 'ln2_b',
           'ffn2_w_gate', 'ffn2_w_up', 'ffn2_w_down', 'ln3_g', 'ln3_b']


def _pack_small(pieces):
    flat = jnp.concatenate([p.reshape(-1) for p in pieces])
    pad = -flat.shape[0] % 1024
    return jnp.pad(flat, (0, pad)).reshape(-1, 128)


def _unpack_small(flat, shapes):
    out, off = [], 0
    for sh in shapes:
        n = 1
        for s in sh:
            n *= s
        out.append(flat[off:off + n].reshape(sh))
        off += n
    return out


def _pad_rows(a, rows):
    return jnp.pad(a, ((0, rows - a.shape[0]), (0, 0)))


def kernel(x, ln1_g, ln1_b, ffn1_w_gate, ffn1_w_up, ffn1_w_down, mix_w_in, pool_w, pool_scale, sconv_w, cconv_w, cconv_b, cnorm_g, cnorm_b, mix_w_out, ln2_g, ln2_b, ffn2_w_gate, ffn2_w_up, ffn2_w_down, ln3_g, ln3_b, loss_target, m_ln1_g, m_ln1_b, m_ffn1_w_gate, m_ffn1_w_up, m_ffn1_w_down, m_mix_w_in, m_pool_w, m_pool_scale, m_sconv_w, m_cconv_w, m_cconv_b, m_cnorm_g, m_cnorm_b, m_mix_w_out, m_ln2_g, m_ln2_b, m_ffn2_w_gate, m_ffn2_w_up, m_ffn2_w_down, m_ln3_g, m_ln3_b, v_ln1_g, v_ln1_b, v_ffn1_w_gate, v_ffn1_w_up, v_ffn1_w_down, v_mix_w_in, v_pool_w, v_pool_scale, v_sconv_w, v_cconv_w, v_cconv_b, v_cnorm_g, v_cnorm_b, v_mix_w_out, v_ln2_g, v_ln2_b, v_ffn2_w_gate, v_ffn2_w_up, v_ffn2_w_down, v_ln3_g, v_ln3_b):
    a = dict(locals())
    depth, d = ln1_g.shape
    t = x.shape[1]
    fs = ffn1_w_gate.shape[2]
    f = fs * N_DEV
    ins = mix_w_in.shape[2]
    ncols = ins * N_DEV
    outs = mix_w_out.shape[1]
    p, s = d // 4, 3 * d // 8
    pg = p // 4
    cs = sconv_w.shape[2]
    alpha = (2.0 * depth) ** 0.25
    me = 4 * lax.axis_index("x") + 2 * lax.axis_index("y") + lax.axis_index("c")
    tm = min(512, t)
    tm_bwd = min(256, t)
    tm_tn = min(2048, t)
    tm_down = min(1024, t)
    tt_fwd = min(1024, t)
    tt_bwd = min(512, t)

    sizes = {"wg1": fs, "wu1": fs, "wd1": fs, "win": ins, "wout": outs, "wg2": fs, "wu2": fs, "wd2": fs,
             "conv": 16}
    per_layer = ["wg1", "wu1", "wd1", "win", "wout", "wg2", "wu2", "wd2"]
    wire = jnp.dtype(F32).itemsize // jnp.dtype(MM).itemsize
    n_conv = (SCONV_K + CCONV_K) * cs

    def conv_rows(l):
        flat = jnp.concatenate([sconv_w[l].reshape(-1), cconv_w[l].reshape(-1)])
        bits = lax.bitcast_convert_type(flat, MM).reshape(-1)
        return jnp.pad(bits, (0, sizes["conv"] * d - bits.shape[0])).reshape(sizes["conv"], d)

    def layer_block(l, key):
        if key == "conv":
            return conv_rows(l)
        return {"wg1": ffn1_w_gate[l].T, "wu1": ffn1_w_up[l].T, "wd1": ffn1_w_down[l], "win": mix_w_in[l].T,
                "wout": mix_w_out[l], "wg2": ffn2_w_gate[l].T, "wu2": ffn2_w_up[l].T,
                "wd2": ffn2_w_down[l]}[key].astype(MM)

    def gather_groups(l):
        if l == 0:
            return [["wg1", "wu1"], ["wd1"], ["win", "wout", "wg2", "wu2", "wd2", "conv"]]
        return [per_layer + ["conv"]]

    gathers = {}
    for l in range(depth):
        for gi, keys in enumerate(gather_groups(l)):
            srcs = [layer_block(l, k) for k in keys]
            gathers[l, gi] = copy_start(srcs, [(N_DEV,) + v.shape for v in srcs], gather=True,
                                        name=f"gather_start_{l}_{gi}")
    started = sum(st[4][0:1, 0:1] for st in gathers.values())

    def gathered(l, gi, after):
        mine, lands = copy_wait(gathers[l, gi], after, gather=True, name=f"gather_wait_{l}_{gi}")
        return {k: lax.dynamic_update_slice(land, own[None], (me, 0, 0)).reshape(N_DEV * sizes[k], d)
                for k, own, land in zip(gather_groups(l)[gi], mine, lands)}

    def conv_filters(rows):
        bits = rows.reshape(N_DEV, -1)[:, :n_conv * wire]
        vals = lax.bitcast_convert_type(bits.reshape(N_DEV, n_conv, wire) if wire > 1 else bits, F32)
        both = vals.reshape(N_DEV, SCONV_K + CCONV_K, cs).transpose(1, 0, 2).reshape(SCONV_K + CCONV_K, s)
        return _pad_rows(both[:SCONV_K], 8), _pad_rows(both[SCONV_K:], 32)

    eye = jnp.eye(4, dtype=F32)
    wbd_all = (pool_w[:, :, :, None, :] * eye[None, :, None, :, None]).reshape(depth, p, p)

    def row(v):
        return v.reshape(1, -1)

    saved = []
    cur, gam, bet = x[0], jnp.ones((1, d), F32), jnp.zeros((1, d), F32) + started
    for l in range(depth):
        w = gathered(l, 0, started if l == 0 else cur)
        sv = {"w": w}
        sv["xb1"], sv["g1"], sv["u1"], sv["h1"] = ffn_up(cur, gam, bet, w["wg1"], w["wu1"], tm=tm)
        if l == 0:
            w.update(gathered(l, 1, sv["xb1"]))
        sv["xh1"], sv["rs1"] = mm_res_ln(sv["h1"], w["wd1"], cur, gam, bet, alpha=alpha, scale=0.5, tm=tm_down)
        if l == 0:
            w.update(gathered(l, 2, sv["xh1"]))
        g1, b1 = row(ln1_g[l]), row(ln1_b[l])
        sv["proj"], sv["xb2"] = mm_in(sv["xh1"], g1, b1, w["win"], tm=tm_down)
        sv["wbd"] = wbd_all[l].astype(MM)
        sv["sw"], sv["cw"] = conv_filters(w["conv"])
        sv["cat"], sv["conv"] = mixer_fwd(sv["proj"], sv["wbd"], row(pool_scale[l]), sv["sw"], sv["cw"],
                                          row(cconv_b[l]), row(cnorm_g[l]), row(cnorm_b[l]), d=d, tt=tt_fwd)
        sv["xh2"], sv["rs2"] = mm_res_ln(sv["cat"], w["wout"], sv["xh1"], g1, b1, alpha=alpha, scale=1.0, tm=tm_down)
        g2, b2 = row(ln2_g[l]), row(ln2_b[l])
        sv["xb3"], sv["g3"], sv["u3"], sv["h3"] = ffn_up(sv["xh2"], g2, b2, w["wg2"], w["wu2"], tm=tm)
        sv["xh3"], sv["rs3"] = mm_res_ln(sv["h3"], w["wd2"], sv["xh2"], g2, b2, alpha=alpha, scale=0.5, tm=tm_down)
        saved.append(sv)
        cur, gam, bet = sv["xh3"], row(ln3_g[l]), row(ln3_b[l])

    dcur, lsum = loss_head(cur, gam, bet, loss_target[0], tm=tm)
    loss = lax.psum(lsum[0, 0] * (0.5 / d), MESH_AXES)

    exchanges = []
    small = [None] * depth

    def outbox(keys):
        offs, off = {}, 0
        for k in keys:
            off = -(-off // sizes[k]) * sizes[k]
            offs[k] = off
            off += sizes[k]
        if _row_tile(off, 256) < 64:
            off = -(-off // 256) * 256
        return {"keys": keys, "offs": offs, "rows": off, "buf": lax.empty((N_DEV, off, d), MM), "own": {}}

    def grad_into(box, k, lhs, rhs, after=None):
        g, box["buf"] = tn_matmul(lhs, rhs, tm=tm_tn, after=after, into=(box["buf"], box["offs"][k]))
        box["own"][k] = lax.dynamic_slice_in_dim(g, me * sizes[k], sizes[k], axis=0)

    def exchange(l, box):
        mine, at = [], 0
        for k in box["keys"]:
            if box["offs"][k] > at:
                mine.append(jnp.zeros((box["offs"][k] - at, d), F32))
            mine.append(box["own"][k])
            at = box["offs"][k] + sizes[k]
        if box["rows"] > at:
            mine.append(jnp.zeros((box["rows"] - at, d), F32))
        st = copy_start([box["buf"]], [(N_DEV - 1, box["rows"], d)], gather=False,
                        name=f"exchange_start_{l}_{box['keys'][0]}")
        exchanges.append((l, box, st, jnp.concatenate(mine, axis=0)))
        return st[4][0:1, 0:1]

    sent = jnp.zeros((1, 1), F32)
    for l in reversed(range(depth)):
        sv = saved[l]
        w = sv["w"]
        dx, dyb, dg, du, dg3, db3 = ffn_bwd(dcur, sv["xh3"], sv["rs3"], row(ln3_g[l]) + sent, w["wd2"], w["wg2"],
                                             w["wu2"], sv["g3"], sv["u3"], alpha=alpha, tm=tm_bwd)
        box = outbox(["wg2", "wu2", "wd2", "win", "wout"] + ([] if l == 0 else ["wg1", "wu1", "wd1"]))
        grad_into(box, "wg2", dg, sv["xb3"])
        grad_into(box, "wu2", du, sv["xb3"])
        grad_into(box, "wd2", sv["h3"], dyb)
        dz, dzb, dcat, dg2, db2 = lnbwd_mm(dx, sv["xh2"], sv["rs2"], row(ln2_g[l]), w["wout"], tm=tm_down)
        dproj, dwbd, dps, dsw, dcw, dcb, dcg, dcbt = mixer_bwd(
            sv["proj"], dcat, sv["conv"], sv["wbd"], sv["wbd"].T, row(pool_scale[l]), sv["sw"], sv["cw"],
            row(cnorm_g[l]), row(cnorm_b[l]), d=d, tt=tt_bwd)
        grad_into(box, "wout", sv["cat"], dzb)
        grad_into(box, "win", dproj, sv["xb2"])
        if l == 0:
            sent = exchange(l, box)
        dx = mm_add(dproj, w["win"], dz, alpha=alpha, tm=tm_down)
        dx, dyb, dg, du, dg1, db1 = ffn_bwd(dx, sv["xh1"], sv["rs1"], row(ln1_g[l]) + sent, w["wd1"], w["wg1"],
                                             w["wu1"], sv["g1"], sv["u1"], alpha=alpha, tm=tm_bwd)
        dcur = dx
        for k, lhs, rhs in (("wg1", dg, sv["xb1"]), ("wu1", du, sv["xb1"]), ("wd1", sv["h1"], dyb)):
            if l == 0:
                box = outbox([k])
                grad_into(box, k, lhs, rhs, after=sent)
                sent = exchange(l, box)
            else:
                grad_into(box, k, lhs, rhs)
        if l > 0:
            sent = exchange(l, box)
        dpw = jnp.stack([dwbd[g * pg:(g + 1) * pg, g * pg:(g + 1) * pg] for g in range(4)])
        small[l] = [dg1, db1, dg2, db2, dg3, db3, dpw, dps, dsw[:SCONV_K], dcw[:CCONV_K], dcb, dcg, dcbt]
    grad_x = dcur[None]

    by_key = {k: [None] * depth for k in per_layer}

    def finish(ex, after):
        l, box, st, mine = ex
        recv = copy_wait(st, after, gather=False, name=f"exchange_wait_{l}_{box['keys'][0]}")[1][0]
        gsum = sum_parts(mine, recv)
        for k in box["keys"]:
            by_key[k][l] = gsum[box["offs"][k]:box["offs"][k] + sizes[k]]
        return gsum

    chain = exchanges[-1][3]
    for ex in exchanges[:-1]:
        chain = finish(ex, chain)

    small_shapes = [g.shape for g in small[0]]
    small_flat = _pack_small([g for l in range(depth) for g in small[l]])
    small_sum = sum_gathered(gather_small(small_flat)).reshape(-1)
    small_g = _unpack_small(small_sum, small_shapes * depth)
    n_small = len(small_shapes)

    grads, deltas, new_m, new_v = {}, {}, {}, {}

    def update(name, grad):
        w = a[name]
        c = w.shape[-1]
        dl, mn, vn = adamw(w.reshape(-1, c), grad.reshape(-1, c), a["m_" + name].reshape(-1, c),
                           a["v_" + name].reshape(-1, c))
        grads[name] = grad
        deltas[name], new_m[name], new_v[name] = dl.reshape(w.shape), mn.reshape(w.shape), vn.reshape(w.shape)
        return dl

    small_names = ["ln1_g", "ln1_b", "ln2_g", "ln2_b", "ln3_g", "ln3_b", "pool_w", "pool_scale", "sconv_w",
                   "cconv_w", "cconv_b", "cnorm_g", "cnorm_b"]
    for idx, name in enumerate(small_names):
        full = jnp.stack([small_g[l * n_small + idx] for l in range(depth)])
        if name in ("sconv_w", "cconv_w"):
            full = lax.dynamic_slice_in_dim(full, me * cs, cs, axis=2)
        chain = update(name, full.reshape(a[name].shape))

    big_names = {"wg1": "ffn1_w_gate", "wu1": "ffn1_w_up", "wd1": "ffn1_w_down", "win": "mix_w_in",
                 "wout": "mix_w_out", "wg2": "ffn2_w_gate", "wu2": "ffn2_w_up", "wd2": "ffn2_w_down"}
    sent_transposed = ("wg1", "wu1", "win", "wg2", "wu2")

    def update_big(k):
        return update(big_names[k], jnp.stack([g.T if k in sent_transposed else g for g in by_key[k]]))

    last_keys = exchanges[-1][1]["keys"]
    for k in per_layer:
        if k not in last_keys:
            chain = update_big(k)
    finish(exchanges[-1], chain)
    for k in last_keys:
        update_big(k)

    return (loss, grad_x, *[grads[n] for n in WEIGHTS], *[deltas[n] for n in WEIGHTS],
            *[new_m[n] for n in WEIGHTS], *[new_v[n] for n in WEIGHTS])
```

```python
import functools

import jax
import jax.numpy as jnp
from jax import lax
from jax.experimental import pallas as pl
from jax.experimental.pallas import tpu as pltpu

F32 = jnp.float32
MM = jnp.bfloat16
LN_EPS = 1e-5
N_DEV = 8
MESH_AXES = ("x", "y", "c")
HALO = 32
CCONV_K = 31
SCONV_K = 3
CONV_ROWS = 64
FFN_CHUNK = 256
ROW_PIECE = 256
VMEM_LIMIT = 56 * 1024 * 1024
ADAM_LR, ADAM_B1, ADAM_B2, ADAM_EPS, ADAM_WD, ADAM_STEP = 0.001, 0.9, 0.999, 1e-08, 0.01, 10
MESH = pl.DeviceIdType.MESH


def _call(body, *, name, out_shape, in_specs, out_specs, grid=None, scratch=(), sem=None, aliases=None):
    kw = {}
    if grid is not None:
        kw["grid"] = grid
    if aliases:
        kw["input_output_aliases"] = aliases
    params = dict(vmem_limit_bytes=VMEM_LIMIT)
    if sem is not None:
        params["dimension_semantics"] = sem
    return pl.pallas_call(body, name=name, out_shape=out_shape, in_specs=in_specs, out_specs=out_specs,
                          scratch_shapes=list(scratch), compiler_params=pltpu.CompilerParams(**params), **kw)


def _rows(tm, n):
    return pl.BlockSpec((tm, n), lambda i: (i, 0))


def _const(shape):
    nd = len(shape)
    return pl.BlockSpec(shape, lambda *_: (0,) * nd, pipeline_mode=pl.Buffered(1))


def _acc(shape):
    nd = len(shape)
    return pl.BlockSpec(shape, lambda *_: (0,) * nd)


def _row_tile(rows, cap):
    best = None
    for t in range(8, min(rows, cap) + 1, 8):
        if rows % t == 0:
            best = t
    return best if best is not None else rows


def _sig(x):
    return 1.0 / (1.0 + jnp.exp(-x))


def _ln_stats(z):
    mu = jnp.mean(z, axis=-1, keepdims=True)
    zc = z - mu
    var = jnp.mean(zc * zc, axis=-1, keepdims=True)
    rstd = lax.rsqrt(var + LN_EPS)
    return zc * rstd, rstd


def _ln_bwd(dout, xhat, rstd, gamma):
    dxh = dout * gamma
    m1 = jnp.mean(dxh, axis=-1, keepdims=True)
    m2 = jnp.mean(dxh * xhat, axis=-1, keepdims=True)
    return rstd * (dxh - m1 - xhat * m2)


def _colsum(v):
    return jnp.sum(v, axis=0, keepdims=True)


def _row_pieces(tm):
    step = ROW_PIECE if tm % ROW_PIECE == 0 else tm
    return [slice(r, r + step) for r in range(0, tm, step)]


def _f_chunks(f, width=None):
    if width is not None and f % width == 0:
        return f // width, width
    n = 2 if f >= 2048 and f % 256 == 0 else 1
    return n, f // n


def _dot_nt(a, b):
    return lax.dot_general(a, b, (((1,), (1,)), ((), ())), preferred_element_type=F32)


def ffn_up(xin, gam, bet, wgt, wut, *, tm):
    t, d = xin.shape
    f = wgt.shape[0]
    nch, fc = _f_chunks(f, FFN_CHUNK)

    def body(x_ref, g_ref, b_ref, wg_ref, wu_ref, xb_ref, go_ref, uo_ref, h_ref):
        xb = (x_ref[...] * g_ref[...] + b_ref[...]).astype(MM)
        xb_ref[...] = xb
        for c in range(nch):
            sl = slice(c * fc, (c + 1) * fc)
            g = _dot_nt(xb, wg_ref[sl, :])
            u = _dot_nt(xb, wu_ref[sl, :])
            go_ref[:, sl] = g.astype(MM)
            uo_ref[:, sl] = u.astype(MM)
            h_ref[:, sl] = (g * _sig(g) * u).astype(MM)

    return _call(
        body, name="ffn_up", grid=(t // tm,), sem=("parallel",),
        in_specs=[_rows(tm, d), _const((1, d)), _const((1, d)), _const((f, d)), _const((f, d))],
        out_specs=[_rows(tm, d), _rows(tm, f), _rows(tm, f), _rows(tm, f)],
        out_shape=[jax.ShapeDtypeStruct((t, d), MM)] + [jax.ShapeDtypeStruct((t, f), MM)] * 3,
    )(xin, gam, bet, wgt, wut)


def mm_res_ln(a, w, xin, gam, bet, *, alpha, scale, tm):
    t, k = a.shape
    d = w.shape[1]

    def body(a_ref, w_ref, x_ref, g_ref, b_ref, xh_ref, rs_ref):
        for rows in _row_pieces(tm):
            y = jnp.dot(a_ref[rows, :], w_ref[...], preferred_element_type=F32)
            x = x_ref[rows, :] * g_ref[...] + b_ref[...]
            xh, rstd = _ln_stats(alpha * x + scale * y)
            xh_ref[rows, :] = xh
            rs_ref[rows, :] = rstd

    return _call(
        body, name="mm_res_ln", grid=(t // tm,), sem=("parallel",),
        in_specs=[_rows(tm, k), _const((k, d)), _rows(tm, d), _const((1, d)), _const((1, d))],
        out_specs=[_rows(tm, d), _rows(tm, 1)],
        out_shape=[jax.ShapeDtypeStruct((t, d), F32), jax.ShapeDtypeStruct((t, 1), F32)],
    )(a, w, xin, gam, bet)


def mm_in(xin, gam, bet, wt, *, tm):
    t, d = xin.shape
    n = wt.shape[0]

    def body(x_ref, g_ref, b_ref, w_ref, o_ref, xb_ref):
        for rows in _row_pieces(tm):
            xb = (x_ref[rows, :] * g_ref[...] + b_ref[...]).astype(MM)
            xb_ref[rows, :] = xb
            o_ref[rows, :] = _dot_nt(xb, w_ref[...]).astype(MM)

    return _call(
        body, name="mm_in", grid=(t // tm,), sem=("parallel",),
        in_specs=[_rows(tm, d), _const((1, d)), _const((1, d)), _const((n, d))],
        out_specs=[_rows(tm, n), _rows(tm, d)],
        out_shape=[jax.ShapeDtypeStruct((t, n), MM), jax.ShapeDtypeStruct((t, d), MM)],
    )(xin, gam, bet, wt)


def loss_head(xh, gam, bet, target, *, tm):
    t, d = xh.shape

    def body(x_ref, g_ref, b_ref, t_ref, dy_ref, l_ref):
        @pl.when(pl.program_id(0) == 0)
        def _():
            l_ref[...] = jnp.zeros_like(l_ref)

        e = x_ref[...] * g_ref[...] + b_ref[...] - t_ref[...]
        dy_ref[...] = e * (1.0 / d)
        l_ref[...] += jnp.sum(_colsum(e * e), axis=1, keepdims=True)

    return _call(
        body, name="loss_head", grid=(t // tm,), sem=("arbitrary",),
        in_specs=[_rows(tm, d), _const((1, d)), _const((1, d)), _rows(tm, d)],
        out_specs=[_rows(tm, d), _acc((1, 1))],
        out_shape=[jax.ShapeDtypeStruct((t, d), F32), jax.ShapeDtypeStruct((1, 1), F32)],
    )(xh, gam, bet, target)


def _halo_specs(tt, ncols, t):
    per, last = tt // HALO, t // HALO - 1
    return [pl.BlockSpec((HALO, ncols), lambda i: (jnp.maximum(i * per - 1, 0), 0)),
            pl.BlockSpec((tt, ncols), lambda i: (i, 0)),
            pl.BlockSpec((HALO, ncols), lambda i: (jnp.minimum((i + 1) * per, last), 0))]


def _fill_ext(ext, prev_ref, cur_ref, next_ref, i, nt, tt):
    ext[0:HALO, :] = jnp.where(i > 0, prev_ref[...].astype(F32), 0.0)
    ext[HALO:HALO + tt, :] = cur_ref[...].astype(F32)
    ext[HALO + tt:HALO + tt + HALO, :] = jnp.where(i < nt - 1, next_ref[...].astype(F32), 0.0)


def _make_shifts(sh, n):
    for r in range(1, 8):
        sh[r, 0:n, :] = sh[0, r:r + n, :]


def _shift_reader(sh):
    def read(o, rows):
        r = o % 8
        return sh[r, o - r:o - r + rows, :]
    return read


def _conv_chunks_shifted(sh, w_ref, ktaps, src0, nrows, flip=False):
    for r0 in range(0, nrows, CONV_ROWS):
        acc = None
        base = src0 + r0
        for r in range(8):
            ks = [k for k in range(ktaps) if (base + k) % 8 == r]
            if not ks:
                continue
            lo = base + ks[0] - r
            slab = sh[r, lo:lo + CONV_ROWS + 8 * (len(ks) - 1), :]
            for q, k in enumerate(ks):
                kk = ktaps - 1 - k if flip else k
                term = w_ref[kk:kk + 1, :] * slab[8 * q:8 * q + CONV_ROWS, :]
                acc = term if acc is None else acc + term
        yield r0, acc


def _ref_reader(ref):
    def read(o, rows):
        return ref[o:o + rows, :]
    return read


def _conv_chunks(read, w_ref, ktaps, src0, nrows, flip=False):
    for r0 in range(0, nrows, CONV_ROWS):
        acc = None
        for k in range(ktaps):
            kk = ktaps - 1 - k if flip else k
            term = w_ref[kk:kk + 1, :] * read(src0 + r0 + k, CONV_ROWS)
            acc = term if acc is None else acc + term
        yield r0, acc


def _pool_groups(nrows, p):
    lane = lax.broadcasted_iota(jnp.int32, (nrows, p), 1)
    g = p // 4
    return lane < g, lane < 2 * g, lane < 3 * g


def _pool_select(groups, v2, v4, v8, v16):
    g0, g1, g2 = groups
    return jnp.where(g0, v2, jnp.where(g1, v4, jnp.where(g2, v8, v16)))


def _pool_count(groups, i, tt, row0, nrows, p, t):
    pos = i * tt + (row0 - HALO) + lax.broadcasted_iota(jnp.int32, (nrows, p), 0)
    half = _pool_select(groups, 1, 2, 4, 8)
    lo = jnp.clip(pos - half, 0, t)
    hi = jnp.clip(pos + half, 0, t)
    return jnp.maximum(hi - lo, 1).astype(F32)


def _pool_forward(ext, s2, s4, s8, groups, cnt, tt, p):
    e = tt + 2 * HALO
    s2[8:e - 8, :] = ext[7:e - 9, 0:p] + ext[8:e - 8, 0:p]
    s4[16:e - 16, :] = s2[15:e - 17, :] + s2[17:e - 15, :]
    s8[24:e - 24, :] = s4[22:e - 26, :] + s4[26:e - 22, :]
    s16 = s8[28:e - 36, :] + s8[36:e - 28, :]
    c = slice(HALO, HALO + tt)
    tot = _pool_select(groups, s2[c, :], s4[c, :], s8[c, :], s16)
    return tot / cnt - ext[c, 0:p]


def mixer_fwd(proj, wbd, pscale, sw, cw, cb, cg, cbt, *, d, tt):
    t, ncols = proj.shape
    p, s = d // 4, 3 * d // 8
    o_gb, o_gc, o_v, o_cv, o_cg = p, p + s, p + 2 * s, p + 3 * s, p + 4 * s
    nt, e = t // tt, tt + 2 * HALO

    def body(prev_ref, cur_ref, next_ref, wbd_ref, ps_ref, sw_ref, cw_ref, cb_ref, cg_ref, cbt_ref,
             cat_ref, conv_ref, ext, a_sh, cv_s, s2, s4, s8):
        i = pl.program_id(0)
        _fill_ext(ext, prev_ref, cur_ref, next_ref, i, nt, tt)
        a_sh[0, 0:e, :] = ext[:, o_cv:o_cv + s] * _sig(ext[:, o_cg:o_cg + s])
        a_sh[0, e:e + 8, :] = jnp.zeros((8, s), F32)
        _make_shifts(a_sh, e)
        for r0, acc in _conv_chunks_shifted(a_sh, cw_ref, CCONV_K, HALO - CCONV_K // 2, tt):
            b = acc + cb_ref[...]
            conv_ref[r0:r0 + CONV_ROWS, :] = b
            n, _ = _ln_stats(b)
            yn = n * cg_ref[...] + cbt_ref[...]
            cat_ref[r0:r0 + CONV_ROWS, p + s:d] = (yn * _sig(yn)).astype(MM)
        cv_s[...] = ext[:, o_gc:o_gc + s] * ext[:, o_v:o_v + s]
        for r0, acc in _conv_chunks(_ref_reader(cv_s), sw_ref, SCONV_K, HALO - SCONV_K // 2, tt):
            gb = ext[HALO + r0:HALO + r0 + CONV_ROWS, o_gb:o_gb + s]
            cat_ref[r0:r0 + CONV_ROWS, p:p + s] = (gb * acc).astype(MM)
        groups = _pool_groups(tt, p)
        cnt = _pool_count(groups, i, tt, HALO, tt, p, t)
        pooled = _pool_forward(ext, s2, s4, s8, groups, cnt, tt, p)
        ya = jnp.dot(pooled.astype(MM), wbd_ref[...], preferred_element_type=F32) * ps_ref[...]
        cat_ref[:, 0:p] = ya.astype(MM)

    return _call(
        body, name="mixer_fwd", grid=(nt,), sem=("parallel",),
        in_specs=_halo_specs(tt, ncols, t) + [_const((p, p)), _const((1, p)), _const((8, s)), _const((32, s)),
                                               _const((1, s)), _const((1, s)), _const((1, s))],
        out_specs=[_rows(tt, d), _rows(tt, s)],
        out_shape=[jax.ShapeDtypeStruct((t, d), MM), jax.ShapeDtypeStruct((t, s), F32)],
        scratch=[pltpu.VMEM((e, ncols), F32), pltpu.VMEM((8, e + 8, s), F32), pltpu.VMEM((e, s), F32),
                 pltpu.VMEM((e, p), F32), pltpu.VMEM((e, p), F32), pltpu.VMEM((e, p), F32)],
    )(proj, proj, proj, wbd, pscale, sw, cw, cb, cg, cbt)


def mixer_bwd(proj, dcat, conv, wbd, wbdt, pscale, sw, cw, cg, cbt, *, d, tt):
    t, ncols = proj.shape
    p, s = d // 4, 3 * d // 8
    o_gb, o_gc, o_v, o_cv, o_cg = p, p + s, p + 2 * s, p + 3 * s, p + 4 * s
    nt, e = t // tt, tt + 2 * HALO

    def body(pp_ref, pc_ref, pn_ref, dp_ref, dc_ref, dn_ref, bp_ref, bc_ref, bn_ref, wbd_ref, wbdt_ref, ps_ref,
             sw_ref, cw_ref, cg_ref, cbt_ref,
             dproj_ref, dwbd_ref, dps_ref, dsw_ref, dcw_ref, dcb_ref, dcg_ref, dcbt_ref,
             ext, dext, bext, a_sh, b_sh, sg_s, cv_s, ds_s, q_s, r2, r4, r8):
        i = pl.program_id(0)

        @pl.when(i == 0)
        def _():
            for ref in (dwbd_ref, dps_ref, dsw_ref, dcw_ref, dcb_ref, dcg_ref, dcbt_ref):
                ref[...] = jnp.zeros_like(ref)

        _fill_ext(ext, pp_ref, pc_ref, pn_ref, i, nt, tt)
        _fill_ext(dext, dp_ref, dc_ref, dn_ref, i, nt, tt)
        _fill_ext(bext, bp_ref, bc_ref, bn_ref, i, nt, tt)
        c = slice(HALO, HALO + tt)

        sg_s[...] = _sig(ext[:, o_cg:o_cg + s])
        a_sh[0, 0:e, :] = ext[:, o_cv:o_cv + s] * sg_s[...]
        a_sh[0, e:e + 8, :] = jnp.zeros((8, s), F32)
        _make_shifts(a_sh, e)
        read_a = _shift_reader(a_sh)
        for r0 in range(0, e, CONV_ROWS):
            b = bext[r0:r0 + CONV_ROWS, :]
            n, rstd = _ln_stats(b)
            yn = n * cg_ref[...] + cbt_ref[...]
            sy = _sig(yn)
            rows = slice(r0, r0 + CONV_ROWS)
            dyn = dext[rows, p + s:d] * (sy * (1.0 + yn * (1.0 - sy)))
            db = _ln_bwd(dyn, n, rstd, cg_ref[...])
            b_sh[0, rows, :] = db
            lo, hi = max(HALO - r0, 0), min(HALO + tt - r0, CONV_ROWS)
            if lo < hi:
                dcg_ref[...] += _colsum((dyn * n)[lo:hi])
                dcbt_ref[...] += _colsum(dyn[lo:hi])
                dcb_ref[...] += _colsum(db[lo:hi])
        b_sh[0, e:e + 8, :] = jnp.zeros((8, s), F32)
        _make_shifts(b_sh, e)
        for r0, da in _conv_chunks_shifted(b_sh, cw_ref, CCONV_K, HALO - CCONV_K // 2, tt, flip=True):
            rows = slice(HALO + r0, HALO + r0 + CONV_ROWS)
            sg = sg_s[rows, :]
            dproj_ref[r0:r0 + CONV_ROWS, o_cv:o_cv + s] = (da * sg).astype(MM)
            dproj_ref[r0:r0 + CONV_ROWS, o_cg:o_cg + s] = (
                da * ext[rows, o_cv:o_cv + s] * sg * (1.0 - sg)).astype(MM)
        for k in range(CCONV_K):
            lo = HALO + k - CCONV_K // 2
            dcw_ref[k:k + 1, :] += _colsum(b_sh[0, c, :] * read_a(lo, tt))

        cv_s[...] = ext[:, o_gc:o_gc + s] * ext[:, o_v:o_v + s]
        ds_s[...] = dext[:, p:p + s] * ext[:, o_gb:o_gb + s]
        for r0, acc in _conv_chunks(_ref_reader(cv_s), sw_ref, SCONV_K, HALO - SCONV_K // 2, tt):
            rows = slice(HALO + r0, HALO + r0 + CONV_ROWS)
            dproj_ref[r0:r0 + CONV_ROWS, o_gb:o_gb + s] = (dext[rows, p:p + s] * acc).astype(MM)
        for r0, dcv in _conv_chunks(_ref_reader(ds_s), sw_ref, SCONV_K, HALO - SCONV_K // 2, tt, flip=True):
            rows = slice(HALO + r0, HALO + r0 + CONV_ROWS)
            dproj_ref[r0:r0 + CONV_ROWS, o_gc:o_gc + s] = (dcv * ext[rows, o_v:o_v + s]).astype(MM)
            dproj_ref[r0:r0 + CONV_ROWS, o_v:o_v + s] = (dcv * ext[rows, o_gc:o_gc + s]).astype(MM)
        for k in range(SCONV_K):
            lo = HALO + k - SCONV_K // 2
            dsw_ref[k:k + 1, :] += _colsum(ds_s[c, :] * cv_s[lo:lo + tt, :])

        groups = _pool_groups(tt, p)
        cnt = _pool_count(groups, i, tt, HALO, tt, p, t)
        pooled = _pool_forward(ext, r2, r4, r8, groups, cnt, tt, p).astype(MM)
        ta = jnp.dot(pooled, wbd_ref[...], preferred_element_type=F32)
        dps_ref[...] += _colsum(dext[c, 0:p] * ta)
        dta = (dext[:, 0:p] * ps_ref[...]).astype(MM)
        dwbd_ref[...] += lax.dot_general(pooled, dta[HALO:HALO + tt, :], (((0,), (0,)), ((), ())),
                                         preferred_element_type=F32)
        dpool = jnp.dot(dta, wbdt_ref[...], preferred_element_type=F32)
        groups_e = _pool_groups(e, p)
        q_s[...] = dpool / _pool_count(groups_e, i, tt, 0, e, p, t)
        r2[8:e - 8, :] = q_s[8:e - 8, :] + q_s[9:e - 7, :]
        r4[16:e - 16, :] = r2[15:e - 17, :] + r2[17:e - 15, :]
        r8[24:e - 24, :] = r4[22:e - 26, :] + r4[26:e - 22, :]
        r16 = r8[28:e - 36, :] + r8[36:e - 28, :]
        du = _pool_select(groups, r2[c, :], r4[c, :], r8[c, :], r16) - dpool[HALO:HALO + tt, :]
        dproj_ref[:, 0:p] = du.astype(MM)

    small = [(p, p), (1, p), (8, s), (32, s), (1, s), (1, s), (1, s)]
    return _call(
        body, name="mixer_bwd", grid=(nt,), sem=("arbitrary",),
        in_specs=_halo_specs(tt, ncols, t) + _halo_specs(tt, d, t) + _halo_specs(tt, s, t) + [
            _const((p, p)), _const((p, p)), _const((1, p)), _const((8, s)), _const((32, s)),
            _const((1, s)), _const((1, s))],
        out_specs=[_rows(tt, ncols)] + [_acc(sh) for sh in small],
        out_shape=[jax.ShapeDtypeStruct((t, ncols), MM)] + [jax.ShapeDtypeStruct(sh, F32) for sh in small],
        scratch=[pltpu.VMEM((e, ncols), F32), pltpu.VMEM((e, d), F32), pltpu.VMEM((e, s), F32),
                 pltpu.VMEM((8, e + 8, s), F32), pltpu.VMEM((8, e + 8, s), F32)]
        + [pltpu.VMEM((e, s), F32)] * 3 + [pltpu.VMEM((e, p), F32)] * 4,
    )(proj, proj, proj, dcat, dcat, dcat, conv, conv, conv, wbd, wbdt, pscale, sw, cw, cg, cbt)


def ffn_bwd(dout, xh, rs, gam, wd, wgt, wut, g, u, *, alpha, tm):
    t, d = dout.shape
    f = g.shape[1]
    nch, fc = _f_chunks(f, FFN_CHUNK)

    def body(do_ref, xh_ref, rs_ref, gm_ref, wd_ref, wgt_ref, wut_ref, g_ref, u_ref,
             dx_ref, dyb_ref, dg_ref, du_ref, dgam_ref, dbet_ref):
        @pl.when(pl.program_id(0) == 0)
        def _():
            dgam_ref[...] = jnp.zeros_like(dgam_ref)
            dbet_ref[...] = jnp.zeros_like(dbet_ref)

        for rows in _row_pieces(tm):
            dout_v, xhat = do_ref[rows, :], xh_ref[rows, :]
            dgam_ref[...] += _colsum(dout_v * xhat)
            dbet_ref[...] += _colsum(dout_v)
            dz = _ln_bwd(dout_v, xhat, rs_ref[rows, :], gm_ref[...])
            dyb = (0.5 * dz).astype(MM)
            dyb_ref[rows, :] = dyb
            for c in range(nch):
                sl = slice(c * fc, (c + 1) * fc)
                dh = _dot_nt(dyb, wd_ref[sl, :])
                gv, uv = g_ref[rows, sl].astype(F32), u_ref[rows, sl].astype(F32)
                sg = _sig(gv)
                dg_ref[rows, sl] = (dh * uv * (sg * (1.0 + gv * (1.0 - sg)))).astype(MM)
                du_ref[rows, sl] = (dh * (gv * sg)).astype(MM)
            dx_ref[rows, :] = (alpha * dz + jnp.dot(dg_ref[rows, :], wgt_ref[...], preferred_element_type=F32)
                               + jnp.dot(du_ref[rows, :], wut_ref[...], preferred_element_type=F32))

    return _call(
        body, name="ffn_bwd", grid=(t // tm,), sem=("arbitrary",),
        in_specs=[_rows(tm, d), _rows(tm, d), _rows(tm, 1), _const((1, d)), _const((f, d)), _const((f, d)),
                  _const((f, d)), _rows(tm, f), _rows(tm, f)],
        out_specs=[_rows(tm, d), _rows(tm, d), _rows(tm, f), _rows(tm, f), _acc((1, d)), _acc((1, d))],
        out_shape=[jax.ShapeDtypeStruct((t, d), F32), jax.ShapeDtypeStruct((t, d), MM),
                   jax.ShapeDtypeStruct((t, f), MM), jax.ShapeDtypeStruct((t, f), MM),
                   jax.ShapeDtypeStruct((1, d), F32), jax.ShapeDtypeStruct((1, d), F32)],
    )(dout, xh, rs, gam, wd, wgt, wut, g, u)


def lnbwd_mm(dout, xh, rs, gam, w, *, tm):
    t, d = dout.shape
    n = w.shape[0]

    def body(do_ref, xh_ref, rs_ref, gm_ref, wt_ref, dz_ref, dzb_ref, da_ref, dgam_ref, dbet_ref):
        @pl.when(pl.program_id(0) == 0)
        def _():
            dgam_ref[...] = jnp.zeros_like(dgam_ref)
            dbet_ref[...] = jnp.zeros_like(dbet_ref)

        for rows in _row_pieces(tm):
            dout_v, xhat = do_ref[rows, :], xh_ref[rows, :]
            dgam_ref[...] += _colsum(dout_v * xhat)
            dbet_ref[...] += _colsum(dout_v)
            dz = _ln_bwd(dout_v, xhat, rs_ref[rows, :], gm_ref[...])
            dz_ref[rows, :] = dz
            dzb = dz.astype(MM)
            dzb_ref[rows, :] = dzb
            da_ref[rows, :] = _dot_nt(dzb, wt_ref[...]).astype(MM)

    return _call(
        body, name="lnbwd_mm", grid=(t // tm,), sem=("arbitrary",),
        in_specs=[_rows(tm, d), _rows(tm, d), _rows(tm, 1), _const((1, d)), _const((n, d))],
        out_specs=[_rows(tm, d), _rows(tm, d), _rows(tm, n), _acc((1, d)), _acc((1, d))],
        out_shape=[jax.ShapeDtypeStruct((t, d), F32), jax.ShapeDtypeStruct((t, d), MM),
                   jax.ShapeDtypeStruct((t, n), MM), jax.ShapeDtypeStruct((1, d), F32),
                   jax.ShapeDtypeStruct((1, d), F32)],
    )(dout, xh, rs, gam, w)


def mm_add(a, w, r, *, alpha, tm):
    t, k = a.shape
    d = w.shape[1]

    def body(a_ref, w_ref, r_ref, o_ref):
        for rows in _row_pieces(tm):
            o_ref[rows, :] = (jnp.dot(a_ref[rows, :], w_ref[...], preferred_element_type=F32)
                              + alpha * r_ref[rows, :])

    return _call(
        body, name="mm_add", grid=(t // tm,), sem=("parallel",),
        in_specs=[_rows(tm, k), _const((k, d)), _rows(tm, d)],
        out_specs=_rows(tm, d),
        out_shape=jax.ShapeDtypeStruct((t, d), F32),
    )(a, w, r)


def tn_matmul(a, b, *, tm, after=None, into=None):
    t, n = a.shape
    d = b.shape[1]
    nch, nc = _f_chunks(n)
    steps = t // tm
    per = n // N_DEV
    slots = N_DEV // nch

    def body(a_ref, b_ref, *rest):
        o_ref, ob_ref = rest[-2:]

        @pl.when(pl.program_id(1) == 0)
        def _():
            o_ref[...] = jnp.zeros_like(o_ref)

        o_ref[...] += lax.dot_general(a_ref[...], b_ref[...], (((0,), (0,)), ((), ())),
                                      preferred_element_type=F32)

        @pl.when(pl.program_id(1) == steps - 1)
        def _():
            if into is None:
                ob_ref[...] = o_ref[...].astype(MM)
            else:
                for s in range(slots):
                    ob_ref[s] = o_ref[s * per:(s + 1) * per, :].astype(MM)

    extra = [] if after is None else [after]
    if into is None:
        ob_spec, ob_shape, alias = pl.BlockSpec((nc, d), lambda j, i: (j, 0)), jax.ShapeDtypeStruct((n, d), MM), {}
    else:
        buf, off = into
        assert off % per == 0 and buf.shape[0] == N_DEV and buf.dtype == MM
        extra.append(buf)
        ob_spec = pl.BlockSpec((slots, per, d), lambda j, i: (j, off // per, 0))
        ob_shape, alias = jax.ShapeDtypeStruct(buf.shape, MM), {1 + len(extra): 1}
    return _call(
        body, name="tn_matmul", grid=(nch, steps), sem=("parallel", "arbitrary"), aliases=alias,
        in_specs=[pl.BlockSpec((tm, nc), lambda j, i: (i, j)), pl.BlockSpec((tm, d), lambda j, i: (i, 0))]
        + [pl.BlockSpec(memory_space=pl.ANY)] * len(extra),
        out_specs=[pl.BlockSpec((nc, d), lambda j, i: (j, 0)), ob_spec],
        out_shape=[jax.ShapeDtypeStruct((n, d), F32), ob_shape],
    )(a, b, *extra)


def sum_parts(own, recv):
    r, d = own.shape
    n = recv.shape[0]
    tr = _row_tile(r, 256)

    def body(own_ref, recv_ref, o_ref):
        acc = own_ref[...]
        for k in range(n):
            acc = acc + recv_ref[k].astype(F32)
        o_ref[...] = acc

    return _call(
        body, name="sum_parts", grid=(r // tr,), sem=("parallel",),
        in_specs=[_rows(tr, d), pl.BlockSpec((n, tr, d), lambda i: (0, i, 0))],
        out_specs=_rows(tr, d),
        out_shape=jax.ShapeDtypeStruct((r, d), F32),
    )(own, recv)


def sum_gathered(parts):
    _, r, n = parts.shape

    def body(p_ref, o_ref):
        acc = p_ref[0]
        for j in range(1, N_DEV):
            acc = acc + p_ref[j]
        o_ref[...] = acc

    return _call(
        body, name="sum_gathered",
        in_specs=[pl.BlockSpec(memory_space=pltpu.VMEM)], out_specs=pl.BlockSpec(memory_space=pltpu.VMEM),
        out_shape=jax.ShapeDtypeStruct((r, n), F32),
    )(parts)


def adamw(w, g, m, v):
    r, c = w.shape
    tr = _row_tile(r, 512)

    def body(w_ref, g_ref, m_ref, v_ref, d_ref, mo_ref, vo_ref):
        gv = g_ref[...]
        mn = ADAM_B1 * m_ref[...] + (1.0 - ADAM_B1) * gv
        vn = ADAM_B2 * v_ref[...] + (1.0 - ADAM_B2) * (gv * gv)
        m_hat = mn / (1.0 - ADAM_B1 ** ADAM_STEP)
        v_hat = vn / (1.0 - ADAM_B2 ** ADAM_STEP)
        d_ref[...] = -ADAM_LR * (m_hat / (jnp.sqrt(v_hat) + ADAM_EPS) + ADAM_WD * w_ref[...])
        mo_ref[...] = mn
        vo_ref[...] = vn

    return _call(
        body, name="adamw", grid=(r // tr,), sem=("parallel",),
        in_specs=[_rows(tr, c)] * 4, out_specs=[_rows(tr, c)] * 3,
        out_shape=[jax.ShapeDtypeStruct((r, c), F32)] * 3,
    )(w, g, m, v)


def _mesh_pos():
    return lax.axis_index("x"), lax.axis_index("y"), lax.axis_index("c")


def _two_level_gather(x_ref, out_ref, send_sems, recv_sems, local_sem):
    x, y, c = _mesh_pos()
    me, sibling = (x, y, c), (x, y, 1 - c)
    chips = [(1 - x, y), (x, 1 - y), (1 - x, 1 - y)]

    def slot(px, py, pc):
        return out_ref.at[4 * px + 2 * py + pc]

    def copy(k, block, to, src=None):
        return pltpu.make_async_remote_copy(
            src_ref=slot(*block) if src is None else src, dst_ref=slot(*block),
            send_sem=send_sems.at[k], recv_sem=recv_sems.at[k], device_id=to, device_id_type=MESH)

    mine = pltpu.make_async_copy(x_ref, slot(*me), local_sem)
    mine.start()
    first = [copy(1 + j, me, (*chip, c), src=x_ref) for j, chip in enumerate(chips)]
    first.append(copy(0, me, sibling, src=x_ref))
    for cp in first:
        cp.start()
    passed = [copy(4 + j, (*chip, c), sibling) for j, chip in enumerate(chips)]
    for j, chip in enumerate(chips):
        copy(1 + j, (*chip, c), me).wait_recv()
        passed[j].start()
    copy(0, sibling, me).wait_recv()
    for j, chip in enumerate(chips):
        copy(4 + j, (*chip, 1 - c), me).wait_recv()
    for cp in first + passed:
        cp.wait_send()
    mine.wait()


def gather_small(x):
    return _call(
        _two_level_gather_body(), name="gather_small",
        in_specs=[pl.BlockSpec(memory_space=pltpu.VMEM)], out_specs=pl.BlockSpec(memory_space=pltpu.VMEM),
        out_shape=jax.ShapeDtypeStruct((N_DEV,) + x.shape, x.dtype),
        scratch=[pltpu.SemaphoreType.DMA((7,)), pltpu.SemaphoreType.DMA((7,)), pltpu.SemaphoreType.DMA(())],
    )(x)


def _two_level_gather_body():
    def body(x_ref, out_ref, send_sems, recv_sems, local_sem):
        _two_level_gather(x_ref, out_ref, send_sems, recv_sems, local_sem)
    return body


_HBM = pl.BlockSpec(memory_space=pltpu.HBM)
_SEM = pl.BlockSpec(memory_space=pltpu.SEMAPHORE)
_EFFECT = pltpu.SideEffectType.DATAFLOW_SIDE_EFFECTING
_RELATIONS = (4, 2, 6, 5, 3, 7, 1)


def _split_copy(src_ref, land_ref, send_sems, recv_sems, k, gather, a):
    x, y, c = _mesh_pos()
    px, py, pc = (1 - x if k & 4 else x), (1 - y if k & 2 else y), (1 - c if k & 1 else c)
    if gather:
        src, dst = src_ref, land_ref.at[4 * x + 2 * y + c]
    else:
        src, dst = src_ref.at[4 * px + 2 * py + pc], land_ref.at[k - 1]
    return pltpu.make_async_remote_copy(src_ref=src, dst_ref=dst, send_sem=send_sems.at[7 * a + k - 1],
                                        recv_sem=recv_sems.at[7 * a + k - 1], device_id=(px, py, pc),
                                        device_id_type=MESH)


def copy_start(srcs, land_shapes, *, gather, name):
    m = len(srcs)

    def body(*refs):
        send_sems, recv_sems = refs[2 * m], refs[2 * m + 1]
        for a in range(m):
            for k in _RELATIONS:
                _split_copy(refs[a], refs[m + a], send_sems, recv_sems, k, gather, a).start()
        refs[-1][...] = jnp.zeros_like(refs[-1])

    res = pl.pallas_call(
        body, name=name,
        out_shape=(pltpu.SemaphoreType.DMA((7 * m,)), pltpu.SemaphoreType.DMA((7 * m,)),
                   *[pltpu.HBM(s.shape, s.dtype) for s in srcs],
                   *[pltpu.HBM(ls, s.dtype) for ls, s in zip(land_shapes, srcs)], jax.ShapeDtypeStruct((8, 128), F32)),
        in_specs=(_HBM,) * (2 * m),
        out_specs=(_SEM, _SEM) + (_HBM,) * (2 * m) + (pl.BlockSpec(memory_space=pltpu.VMEM),),
        input_output_aliases={i: 2 + i for i in range(2 * m)},
        compiler_params=pltpu.CompilerParams(has_side_effects=_EFFECT),
    )(*[pltpu.with_memory_space_constraint(s, pltpu.HBM) for s in srcs],
      *[pltpu.with_memory_space_constraint(lax.empty(ls, s.dtype), pltpu.HBM) for ls, s in zip(land_shapes, srcs)])
    return res[0], res[1], list(res[2:2 + m]), list(res[2 + m:2 + 2 * m]), res[-1]


def copy_wait(started, after, *, gather, name):
    send_sems, recv_sems, srcs, lands, _ = started
    m = len(srcs)

    def body(*refs):
        send_sems, recv_sems = refs[2 * m], refs[2 * m + 1]
        for a in range(m):
            for k in _RELATIONS:
                cp = _split_copy(refs[a], refs[m + a], send_sems, recv_sems, k, gather, a)
                cp.wait_send()
                cp.wait_recv()

    res = pl.pallas_call(
        body, name=name,
        out_shape=tuple(pltpu.HBM(v.shape, v.dtype) for v in srcs + lands),
        in_specs=(_HBM,) * (2 * m) + (_SEM, _SEM, pl.BlockSpec(memory_space=pl.ANY)), out_specs=(_HBM,) * (2 * m),
        input_output_aliases={i: i for i in range(2 * m)},
        compiler_params=pltpu.CompilerParams(has_side_effects=_EFFECT),
    )(*srcs, *lands, send_sems, recv_sems, after)
    return list(res[:m]), list(res[m:])


WEIGHTS = ['ln1_g', 'ln1_b', 'ffn1_w_gate', 'ffn1_w_up', 'ffn1_w_down', 'mix_w_in', 'pool_w', 'pool_scale',
           'sconv_w', 'cconv_w', 'cconv_b', 'cnorm_g', 'cnorm_b', 'mix_w_out', 'ln2_g', 'ln2_b',
           'ffn2_w_gate', 'ffn2_w_up', 'ffn2_w_down', 'ln3_g', 'ln3_b']


def _pack_small(pieces):
    flat = jnp.concatenate([p.reshape(-1) for p in pieces])
    pad = -flat.shape[0] % 1024
    return jnp.pad(flat, (0, pad)).reshape(-1, 128)


def _unpack_small(flat, shapes):
    out, off = [], 0
    for sh in shapes:
        n = 1
        for s in sh:
            n *= s
        out.append(flat[off:off + n].reshape(sh))
        off += n
    return out


def _pad_rows(a, rows):
    return jnp.pad(a, ((0, rows - a.shape[0]), (0, 0)))


def kernel(x, ln1_g, ln1_b, ffn1_w_gate, ffn1_w_up, ffn1_w_down, mix_w_in, pool_w, pool_scale, sconv_w, cconv_w, cconv_b, cnorm_g, cnorm_b, mix_w_out, ln2_g, ln2_b, ffn2_w_gate, ffn2_w_up, ffn2_w_down, ln3_g, ln3_b, loss_target, m_ln1_g, m_ln1_b, m_ffn1_w_gate, m_ffn1_w_up, m_ffn1_w_down, m_mix_w_in, m_pool_w, m_pool_scale, m_sconv_w, m_cconv_w, m_cconv_b, m_cnorm_g, m_cnorm_b, m_mix_w_out, m_ln2_g, m_ln2_b, m_ffn2_w_gate, m_ffn2_w_up, m_ffn2_w_down, m_ln3_g, m_ln3_b, v_ln1_g, v_ln1_b, v_ffn1_w_gate, v_ffn1_w_up, v_ffn1_w_down, v_mix_w_in, v_pool_w, v_pool_scale, v_sconv_w, v_cconv_w, v_cconv_b, v_cnorm_g, v_cnorm_b, v_mix_w_out, v_ln2_g, v_ln2_b, v_ffn2_w_gate, v_ffn2_w_up, v_ffn2_w_down, v_ln3_g, v_ln3_b):
    a = dict(locals())
    depth, d = ln1_g.shape
    t = x.shape[1]
    fs = ffn1_w_gate.shape[2]
    f = fs * N_DEV
    ins = mix_w_in.shape[2]
    ncols = ins * N_DEV
    outs = mix_w_out.shape[1]
    p, s = d // 4, 3 * d // 8
    pg = p // 4
    cs = sconv_w.shape[2]
    alpha = (2.0 * depth) ** 0.25
    me = 4 * lax.axis_index("x") + 2 * lax.axis_index("y") + lax.axis_index("c")
    tm = min(512, t)
    tm_bwd = min(256, t)
    tm_tn = min(2048, t)
    tm_down = min(1024, t)
    tt_fwd = min(1024, t)
    tt_bwd = min(512, t)

    sizes = {"wg1": fs, "wu1": fs, "wd1": fs, "win": ins, "wout": outs, "wg2": fs, "wu2": fs, "wd2": fs,
             "conv": 16}
    per_layer = ["wg1", "wu1", "wd1", "win", "wout", "wg2", "wu2", "wd2"]
    wire = jnp.dtype(F32).itemsize // jnp.dtype(MM).itemsize
    n_conv = (SCONV_K + CCONV_K) * cs

    def conv_rows(l):
        flat = jnp.concatenate([sconv_w[l].reshape(-1), cconv_w[l].reshape(-1)])
        bits = lax.bitcast_convert_type(flat, MM).reshape(-1)
        return jnp.pad(bits, (0, sizes["conv"] * d - bits.shape[0])).reshape(sizes["conv"], d)

    def layer_block(l, key):
        if key == "conv":
            return conv_rows(l)
        return {"wg1": ffn1_w_gate[l].T, "wu1": ffn1_w_up[l].T, "wd1": ffn1_w_down[l], "win": mix_w_in[l].T,
                "wout": mix_w_out[l], "wg2": ffn2_w_gate[l].T, "wu2": ffn2_w_up[l].T,
                "wd2": ffn2_w_down[l]}[key].astype(MM)

    def gather_groups(l):
        if l == 0:
            return [["wg1", "wu1"], ["wd1"], ["win", "wout", "wg2", "wu2", "wd2", "conv"]]
        return [per_layer + ["conv"]]

    gathers = {}
    for l in range(depth):
        for gi, keys in enumerate(gather_groups(l)):
            srcs = [layer_block(l, k) for k in keys]
            gathers[l, gi] = copy_start(srcs, [(N_DEV,) + v.shape for v in srcs], gather=True,
                                        name=f"gather_start_{l}_{gi}")
    started = sum(st[4][0:1, 0:1] for st in gathers.values())

    def gathered(l, gi, after):
        mine, lands = copy_wait(gathers[l, gi], after, gather=True, name=f"gather_wait_{l}_{gi}")
        return {k: lax.dynamic_update_slice(land, own[None], (me, 0, 0)).reshape(N_DEV * sizes[k], d)
                for k, own, land in zip(gather_groups(l)[gi], mine, lands)}

    def conv_filters(rows):
        bits = rows.reshape(N_DEV, -1)[:, :n_conv * wire]
        vals = lax.bitcast_convert_type(bits.reshape(N_DEV, n_conv, wire) if wire > 1 else bits, F32)
        both = vals.reshape(N_DEV, SCONV_K + CCONV_K, cs).transpose(1, 0, 2).reshape(SCONV_K + CCONV_K, s)
        return _pad_rows(both[:SCONV_K], 8), _pad_rows(both[SCONV_K:], 32)

    eye = jnp.eye(4, dtype=F32)
    wbd_all = (pool_w[:, :, :, None, :] * eye[None, :, None, :, None]).reshape(depth, p, p)

    def row(v):
        return v.reshape(1, -1)

    saved = []
    cur, gam, bet = x[0], jnp.ones((1, d), F32), jnp.zeros((1, d), F32) + started
    for l in range(depth):
        w = gathered(l, 0, started if l == 0 else cur)
        sv = {"w": w}
        sv["xb1"], sv["g1"], sv["u1"], sv["h1"] = ffn_up(cur, gam, bet, w["wg1"], w["wu1"], tm=tm)
        if l == 0:
            w.update(gathered(l, 1, sv["xb1"]))
        sv["xh1"], sv["rs1"] = mm_res_ln(sv["h1"], w["wd1"], cur, gam, bet, alpha=alpha, scale=0.5, tm=tm_down)
        if l == 0:
            w.update(gathered(l, 2, sv["xh1"]))
        g1, b1 = row(ln1_g[l]), row(ln1_b[l])
        sv["proj"], sv["xb2"] = mm_in(sv["xh1"], g1, b1, w["win"], tm=tm_down)
        sv["wbd"] = wbd_all[l].astype(MM)
        sv["sw"], sv["cw"] = conv_filters(w["conv"])
        sv["cat"], sv["conv"] = mixer_fwd(sv["proj"], sv["wbd"], row(pool_scale[l]), sv["sw"], sv["cw"],
                                          row(cconv_b[l]), row(cnorm_g[l]), row(cnorm_b[l]), d=d, tt=tt_fwd)
        sv["xh2"], sv["rs2"] = mm_res_ln(sv["cat"], w["wout"], sv["xh1"], g1, b1, alpha=alpha, scale=1.0, tm=tm_down)
        g2, b2 = row(ln2_g[l]), row(ln2_b[l])
        sv["xb3"], sv["g3"], sv["u3"], sv["h3"] = ffn_up(sv["xh2"], g2, b2, w["wg2"], w["wu2"], tm=tm)
        sv["xh3"], sv["rs3"] = mm_res_ln(sv["h3"], w["wd2"], sv["xh2"], g2, b2, alpha=alpha, scale=0.5, tm=tm_down)
        saved.append(sv)
        cur, gam, bet = sv["xh3"], row(ln3_g[l]), row(ln3_b[l])

    dcur, lsum = loss_head(cur, gam, bet, loss_target[0], tm=tm)
    loss = lax.psum(lsum[0, 0] * (0.5 / d), MESH_AXES)

    exchanges = []
    small = [None] * depth

    def outbox(keys):
        offs, off = {}, 0
        for k in keys:
            off = -(-off // sizes[k]) * sizes[k]
            offs[k] = off
            off += sizes[k]
        if _row_tile(off, 256) < 64:
            off = -(-off // 256) * 256
        return {"keys": keys, "offs": offs, "rows": off, "buf": lax.empty((N_DEV, off, d), MM), "own": {}}

    def grad_into(box, k, lhs, rhs, after=None):
        g, box["buf"] = tn_matmul(lhs, rhs, tm=tm_tn, after=after, into=(box["buf"], box["offs"][k]))
        box["own"][k] = lax.dynamic_slice_in_dim(g, me * sizes[k], sizes[k], axis=0)

    def exchange(l, box):
        mine, at = [], 0
        for k in box["keys"]:
            if box["offs"][k] > at:
                mine.append(jnp.zeros((box["offs"][k] - at, d), F32))
            mine.append(box["own"][k])
            at = box["offs"][k] + sizes[k]
        if box["rows"] > at:
            mine.append(jnp.zeros((box["rows"] - at, d), F32))
        st = copy_start([box["buf"]], [(N_DEV - 1, box["rows"], d)], gather=False,
                        name=f"exchange_start_{l}_{box['keys'][0]}")
        exchanges.append((l, box, st, jnp.concatenate(mine, axis=0)))
        return st[4][0:1, 0:1]

    sent = jnp.zeros((1, 1), F32)
    for l in reversed(range(depth)):
        sv = saved[l]
        w = sv["w"]
        dx, dyb, dg, du, dg3, db3 = ffn_bwd(dcur, sv["xh3"], sv["rs3"], row(ln3_g[l]) + sent, w["wd2"], w["wg2"],
                                             w["wu2"], sv["g3"], sv["u3"], alpha=alpha, tm=tm_bwd)
        box = outbox(["wg2", "wu2", "wd2", "win", "wout"] + ([] if l == 0 else ["wg1", "wu1", "wd1"]))
        grad_into(box, "wg2", dg, sv["xb3"])
        grad_into(box, "wu2", du, sv["xb3"])
        grad_into(box, "wd2", sv["h3"], dyb)
        dz, dzb, dcat, dg2, db2 = lnbwd_mm(dx, sv["xh2"], sv["rs2"], row(ln2_g[l]), w["wout"], tm=tm_down)
        dproj, dwbd, dps, dsw, dcw, dcb, dcg, dcbt = mixer_bwd(
            sv["proj"], dcat, sv["conv"], sv["wbd"], sv["wbd"].T, row(pool_scale[l]), sv["sw"], sv["cw"],
            row(cnorm_g[l]), row(cnorm_b[l]), d=d, tt=tt_bwd)
        grad_into(box, "wout", sv["cat"], dzb)
        grad_into(box, "win", dproj, sv["xb2"])
        if l == 0:
            sent = exchange(l, box)
        dx = mm_add(dproj, w["win"], dz, alpha=alpha, tm=tm_down)
        dx, dyb, dg, du, dg1, db1 = ffn_bwd(dx, sv["xh1"], sv["rs1"], row(ln1_g[l]) + sent, w["wd1"], w["wg1"],
                                             w["wu1"], sv["g1"], sv["u1"], alpha=alpha, tm=tm_bwd)
        dcur = dx
        for k, lhs, rhs in (("wg1", dg, sv["xb1"]), ("wu1", du, sv["xb1"]), ("wd1", sv["h1"], dyb)):
            if l == 0:
                box = outbox([k])
                grad_into(box, k, lhs, rhs, after=sent)
                sent = exchange(l, box)
            else:
                grad_into(box, k, lhs, rhs)
        if l > 0:
            sent = exchange(l, box)
        dpw = jnp.stack([dwbd[g * pg:(g + 1) * pg, g * pg:(g + 1) * pg] for g in range(4)])
        small[l] = [dg1, db1, dg2, db2, dg3, db3, dpw, dps, dsw[:SCONV_K], dcw[:CCONV_K], dcb, dcg, dcbt]
    grad_x = dcur[None]

    by_key = {k: [None] * depth for k in per_layer}

    def finish(ex, after):
        l, box, st, mine = ex
        recv = copy_wait(st, after, gather=False, name=f"exchange_wait_{l}_{box['keys'][0]}")[1][0]
        gsum = sum_parts(mine, recv)
        for k in box["keys"]:
            by_key[k][l] = gsum[box["offs"][k]:box["offs"][k] + sizes[k]]
        return gsum

    chain = exchanges[-1][3]
    for ex in exchanges[:-1]:
        chain = finish(ex, chain)

    small_shapes = [g.shape for g in small[0]]
    small_flat = _pack_small([g for l in range(depth) for g in small[l]])
    small_sum = sum_gathered(gather_small(small_flat)).reshape(-1)
    small_g = _unpack_small(small_sum, small_shapes * depth)
    n_small = len(small_shapes)

    grads, deltas, new_m, new_v = {}, {}, {}, {}

    def update(name, grad):
        w = a[name]
        c = w.shape[-1]
        dl, mn, vn = adamw(w.reshape(-1, c), grad.reshape(-1, c), a["m_" + name].reshape(-1, c),
                           a["v_" + name].reshape(-1, c))
        grads[name] = grad
        deltas[name], new_m[name], new_v[name] = dl.reshape(w.shape), mn.reshape(w.shape), vn.reshape(w.shape)
        return dl

    small_names = ["ln1_g", "ln1_b", "ln2_g", "ln2_b", "ln3_g", "ln3_b", "pool_w", "pool_scale", "sconv_w",
                   "cconv_w", "cconv_b", "cnorm_g", "cnorm_b"]
    for idx, name in enumerate(small_names):
        full = jnp.stack([small_g[l * n_small + idx] for l in range(depth)])
        if name in ("sconv_w", "cconv_w"):
            full = lax.dynamic_slice_in_dim(full, me * cs, cs, axis=2)
        chain = update(name, full.reshape(a[name].shape))

    big_names = {"wg1": "ffn1_w_gate", "wu1": "ffn1_w_up", "wd1": "ffn1_w_down", "win": "mix_w_in",
                 "wout": "mix_w_out", "wg2": "ffn2_w_gate", "wu2": "ffn2_w_up", "wd2": "ffn2_w_down"}
    sent_transposed = ("wg1", "wu1", "win", "wg2", "wu2")

    def update_big(k):
        return update(big_names[k], jnp.stack([g.T if k in sent_transposed else g for g in by_key[k]]))

    last_keys = exchanges[-1][1]["keys"]
    for k in per_layer:
        if k not in last_keys:
            chain = update_big(k)
    finish(exchanges[-1], chain)
    for k in last_keys:
        update_big(k)

    return (loss, grad_x, *[grads[n] for n in WEIGHTS], *[deltas[n] for n in WEIGHTS],
            *[new_m[n] for n in WEIGHTS], *[new_v[n] for n in WEIGHTS])
```
